```python
import math, functools
import jax, jax.numpy as jnp
from jax import lax
import numpy as np

D_MODEL = 1024
BATCH = 8
SEQ = 4096
DEPTH = 2

GRID_W = 64
CTX_LEN = 256

SSM_WIDTH = 512
SSM_GROUP = 16
SSM_GROUPS = SSM_WIDTH // SSM_GROUP
SSM_STATE = 64
DT_MIN = 1e-3
DT_MAX = 1e-1

HEAD_DIM = 64
GQA_HEADS = 8
GQA_KV_HEADS = 2
GQA_GROUP = GQA_HEADS // GQA_KV_HEADS
Q_BLOCK = 128
ROPE_BASE = 10000.0
NA_HEADS = 8
NA_WIN_R = 8
NA_WIN_C = 16
ATTN_SCALE = HEAD_DIM ** -0.5

GQA_Q_W = GQA_HEADS * HEAD_DIM
GQA_KV_W = GQA_KV_HEADS * HEAD_DIM
NA_W = NA_HEADS * HEAD_DIM
N_BRANCHES = 3

KV_COLS = SSM_WIDTH + 2 * GQA_KV_W + 2 * NA_W
IN_COLS = KV_COLS + GQA_Q_W + NA_W + N_BRANCHES * D_MODEL
IN_SPLITS = [
    SSM_WIDTH,
    SSM_WIDTH + GQA_KV_W,
    SSM_WIDTH + 2 * GQA_KV_W,
    SSM_WIDTH + 2 * GQA_KV_W + NA_W,
    KV_COLS,
    KV_COLS + GQA_Q_W,
    KV_COLS + GQA_Q_W + NA_W,
]

FFN_DIM = 2816
N_EXPERTS = 8
TOP_K = 2
EXPERT_DIM = 3584
N_DENSE = (DEPTH + 1) // 2
N_MOE = DEPTH // 2

NORM_EPS = 1e-6

kernel_name = "hybrid_s5_gqa_natten_moe_dit"


def rms_norm(x, g):
    xf = x.astype(jnp.float32)
    y = xf * lax.rsqrt(jnp.mean(xf * xf, axis=-1, keepdims=True) + NORM_EPS)
    return (y * g.astype(jnp.float32)).astype(x.dtype)


def modulate(h, shift, scale):
    return h * (1.0 + scale) + shift


def adaln(cvec, w, b, n_chunks):
    m = jax.nn.silu(cvec) @ w[:, :n_chunks * D_MODEL] + b[:n_chunks * D_MODEL]
    return jnp.split(m, n_chunks, axis=-1)


def axial_rope(n_tokens, dtype):
    t = jnp.arange(n_tokens)
    pos = jnp.stack([t // GRID_W, t % GRID_W], axis=-1).astype(jnp.float32)
    half = HEAD_DIM // 2
    inv = 1.0 / (ROPE_BASE ** (jnp.arange(0, half, 2, dtype=jnp.float32) / half))
    ang = pos[:, :, None] * inv
    ang = jnp.concatenate([ang, ang], axis=-1).reshape(n_tokens, HEAD_DIM)
    return jnp.cos(ang).astype(dtype), jnp.sin(ang).astype(dtype)


def apply_rope(x, cos, sin):
    xs = x.reshape(x.shape[:-1] + (2, 2, HEAD_DIM // 4))
    rot = jnp.stack([-xs[..., 1, :], xs[..., 0, :]], axis=-2).reshape(x.shape)
    return x * cos[:, None, :] + rot * sin[:, None, :]


def s5_discretize(a_re, a_im, b_re, b_im, log_dt):
    lam = lax.complex(a_re.astype(jnp.float32), a_im.astype(jnp.float32))
    dt = jnp.exp(log_dt.astype(jnp.float32))[:, None]
    a_bar = jnp.exp(lam * dt)
    b = lax.complex(b_re.astype(jnp.float32), b_im.astype(jnp.float32))
    b_bar = ((a_bar - 1.0) / lam)[..., None] * b
    return a_bar, b_bar


def diag_scan(a_bar, bu, s0, reverse):
    if s0 is not None:
        idx = -1 if reverse else 0
        bu = bu.at[:, idx].add(a_bar * s0)
    a = jnp.broadcast_to(a_bar, bu.shape)

    def combine(e1, e2):
        a1, b1 = e1
        a2, b2 = e2
        return a1 * a2, a2 * b1 + b2

    _, s = lax.associative_scan(combine, (a, bu), reverse=reverse, axis=1)
    return s


def s5_glu(y, w, b):
    y = jax.nn.gelu(y)
    return y * jax.nn.sigmoid(y @ w + b)


def s5_bidirectional(u, uc, a_re, a_im, b_re, b_im, c_re, c_im, log_dt, d_skip, glu_w, glu_b, need_ctx):
    bsz, n_lat, _ = u.shape
    n_ctx = uc.shape[1]
    dtype = u.dtype
    uf = u.astype(jnp.float32)
    ucf = uc.astype(jnp.float32)
    ug = uf.reshape(bsz, n_lat, SSM_GROUPS, SSM_GROUP)
    ucg = ucf.reshape(bsz, n_ctx, SSM_GROUPS, SSM_GROUP)
    d = d_skip.astype(jnp.float32)
    y = uf * d
    yc = ucf * d if need_ctx else None
    for direction in range(2):
        rev = direction == 1
        a_bar, b_bar = s5_discretize(a_re[direction], a_im[direction], b_re[direction],
                                     b_im[direction], log_dt[direction])
        c_mat = lax.complex(c_re[direction].astype(jnp.float32), c_im[direction].astype(jnp.float32))
        s_ctx = diag_scan(a_bar, jnp.einsum('bcgh,gph->bcgp', ucg, b_bar), None, rev)
        s_init = s_ctx[:, 0] if rev else s_ctx[:, -1]
        s_lat = diag_scan(a_bar, jnp.einsum('blgh,gph->blgp', ug, b_bar), s_init, rev)
        y = y + jnp.real(jnp.einsum('blgp,ghp->blgh', s_lat, c_mat)).reshape(bsz, n_lat, SSM_WIDTH)
        if need_ctx:
            yc = yc + jnp.real(jnp.einsum('bcgp,ghp->bcgh', s_ctx, c_mat)).reshape(bsz, n_ctx, SSM_WIDTH)
    y_out = s5_glu(y.astype(dtype), glu_w, glu_b)
    yc_out = s5_glu(yc.astype(dtype), glu_w, glu_b) if need_ctx else None
    return y_out, yc_out


def gqa_branch(q, k, v, qc, kc, vc, q_g, k_g, cos, sin):
    bsz, n_lat, _ = q.shape
    n_ctx = kc.shape[1]
    q = apply_rope(rms_norm(q.reshape(bsz, n_lat, GQA_HEADS, HEAD_DIM), q_g), cos, sin)
    k = apply_rope(rms_norm(k.reshape(bsz, n_lat, GQA_KV_HEADS, HEAD_DIM), k_g), cos, sin)
    v = v.reshape(bsz, n_lat, GQA_KV_HEADS, HEAD_DIM)
    kc = rms_norm(kc.reshape(bsz, n_ctx, GQA_KV_HEADS, HEAD_DIM), k_g)
    vc = vc.reshape(bsz, n_ctx, GQA_KV_HEADS, HEAD_DIM)
    k_all = jnp.concatenate([k, kc], axis=1)
    v_all = jnp.concatenate([v, vc], axis=1)
    n_blk = n_lat // Q_BLOCK
    qb = q.reshape(bsz, n_blk, Q_BLOCK, GQA_KV_HEADS, GQA_GROUP, HEAD_DIM).transpose(1, 0, 2, 3, 4, 5)

    def block(qi):
        s = jnp.einsum('bqkgd,bskd->bkgqs', qi, k_all).astype(jnp.float32) * ATTN_SCALE
        p = jax.nn.softmax(s, axis=-1).astype(v_all.dtype)
        return jnp.einsum('bkgqs,bskd->bqkgd', p, v_all)

    o = lax.map(block, qb)
    o = o.transpose(1, 0, 2, 3, 4, 5).reshape(bsz, n_lat, GQA_Q_W)
    oc = None
    if qc is not None:
        qc = rms_norm(qc.reshape(bsz, n_ctx, GQA_KV_HEADS, GQA_GROUP, HEAD_DIM), q_g)
        s = jnp.einsum('bqkgd,bskd->bkgqs', qc, kc).astype(jnp.float32) * ATTN_SCALE
        p = jax.nn.softmax(s, axis=-1).astype(vc.dtype)
        oc = jnp.einsum('bkgqs,bskd->bqkgd', p, vc).reshape(bsz, n_ctx, GQA_Q_W)
    return o, oc


def na_branch(q, k, v, qc, kc, vc, rpb):
    bsz, n_lat, _ = q.shape
    n_ctx = kc.shape[1]
    rows = n_lat // GRID_W
    win_r = min(NA_WIN_R, rows)
    n_win = win_r * NA_WIN_C
    qg = q.reshape(bsz, rows, GRID_W, NA_HEADS, HEAD_DIM)
    kg = k.reshape(bsz, rows, GRID_W, NA_HEADS, HEAD_DIM)
    vg = v.reshape(bsz, rows, GRID_W, NA_HEADS, HEAD_DIM)
    kc = kc.reshape(bsz, n_ctx, NA_HEADS, HEAD_DIM)
    vc = vc.reshape(bsz, n_ctx, NA_HEADS, HEAD_DIM)
    cols = jnp.arange(GRID_W)
    col_start = jnp.clip(cols - NA_WIN_C // 2, 0, GRID_W - NA_WIN_C)
    col_idx = col_start[:, None] + jnp.arange(NA_WIN_C)
    col_bias_idx = col_idx - cols[:, None] + NA_WIN_C - 1

    def row_block(r):
        rs = jnp.clip(r - win_r // 2, 0, rows - win_r)
        q_r = lax.dynamic_index_in_dim(qg, r, axis=1, keepdims=False)
        k_win = lax.dynamic_slice_in_dim(kg, rs, win_r, axis=1)[:, :, col_idx]
        v_win = lax.dynamic_slice_in_dim(vg, rs, win_r, axis=1)[:, :, col_idx]
        s_win = jnp.einsum('bqhd,brqkhd->bhqrk', q_r, k_win).astype(jnp.float32) * ATTN_SCALE
        row_bias_idx = rs + jnp.arange(win_r) - r + NA_WIN_R - 1
        bias = rpb[:, row_bias_idx][:, :, col_bias_idx]
        s_win = s_win + bias.transpose(0, 2, 1, 3)[None].astype(jnp.float32)
        s_ctx = jnp.einsum('bqhd,bshd->bhqs', q_r, kc).astype(jnp.float32) * ATTN_SCALE
        s = jnp.concatenate([s_win.reshape(bsz, NA_HEADS, GRID_W, n_win), s_ctx], axis=-1)
        p = jax.nn.softmax(s, axis=-1).astype(v.dtype)
        p_win = p[..., :n_win].reshape(bsz, NA_HEADS, GRID_W, win_r, NA_WIN_C)
        p_ctx = p[..., n_win:]
        return (jnp.einsum('bhqrk,brqkhd->bqhd', p_win, v_win)
                + jnp.einsum('bhqs,bshd->bqhd', p_ctx, vc))

    o = lax.map(row_block, jnp.arange(rows))
    o = o.transpose(1, 0, 2, 3, 4).reshape(bsz, n_lat, NA_W)
    oc = None
    if qc is not None:
        qc = qc.reshape(bsz, n_ctx, NA_HEADS, HEAD_DIM)
        s = jnp.einsum('bqhd,bshd->bhqs', qc, kc).astype(jnp.float32) * ATTN_SCALE
        p = jax.nn.softmax(s, axis=-1).astype(vc.dtype)
        oc = jnp.einsum('bhqs,bshd->bqhd', p, vc).reshape(bsz, n_ctx, NA_W)
    return o, oc


def merge_branches(y_ssm, y_gqa, y_na, gate_logits, w_b_ssm, w_b_gqa, w_b_na, w_o):
    g = jax.nn.sigmoid(gate_logits.astype(jnp.float32)).astype(y_ssm.dtype)
    g = g.reshape(g.shape[:-1] + (N_BRANCHES, D_MODEL))
    m = (g[..., 0, :] * (y_ssm @ w_b_ssm)
         + g[..., 1, :] * (y_gqa @ w_b_gqa)
         + g[..., 2, :] * (y_na @ w_b_na))
    return m @ w_o


def swiglu(h, w_gate, w_up, w_down):
    return (jax.nn.silu(h @ w_gate) * (h @ w_up)) @ w_down


def moe_swiglu(h, router_w, w_gate, w_up, w_down):
    shp = h.shape
    t = h.reshape(-1, shp[-1])
    logits = (t @ router_w).astype(jnp.float32)
    top_v, top_i = lax.top_k(logits, TOP_K)
    top_w = jax.nn.softmax(top_v, axis=-1)
    gates = jnp.sum(jax.nn.one_hot(top_i, N_EXPERTS, dtype=jnp.float32) * top_w[..., None], axis=1)
    out = jnp.zeros_like(t)
    for e in range(N_EXPERTS):
        out = out + gates[:, e:e + 1].astype(t.dtype) * swiglu(t, w_gate[e], w_up[e], w_down[e])
    return out.reshape(shp)


def setup_inputs(seed: int = 0) -> dict:
    key = jax.random.key(seed)
    ks = iter(jax.random.split(key, 48))
    f32 = jnp.float32
    D = D_MODEL
    G, P, H = SSM_GROUPS, SSM_STATE, SSM_GROUP

    def nrm(shape, scale):
        return jax.random.normal(next(ks), shape, f32) * scale

    n_idx = jnp.arange(P, dtype=f32)
    return {
        "x": nrm((BATCH, SEQ, D), 1.0),
        "c": nrm((BATCH, D), 1.0),
        "ctx": nrm((BATCH, CTX_LEN, D), 1.0),
        "c_ctx": nrm((D,), 1.0),
        "w_mod": nrm((DEPTH, D, 6 * D), 0.5 * D ** -0.5),
        "b_mod": nrm((DEPTH, 6 * D), 0.01),
        "norm1_g": 1.0 + nrm((DEPTH, D), 0.02),
        "w_in": nrm((DEPTH, D, IN_COLS), D ** -0.5),
        "ssm_a_re": -0.5 + nrm((DEPTH, 2, G, P), 0.01),
        "ssm_a_im": math.pi * n_idx + nrm((DEPTH, 2, G, P), 0.01),
        "ssm_b_re": nrm((DEPTH, 2, G, P, H), (2 * H) ** -0.5),
        "ssm_b_im": nrm((DEPTH, 2, G, P, H), (2 * H) ** -0.5),
        "ssm_c_re": nrm((DEPTH, 2, G, H, P), P ** -0.5),
        "ssm_c_im": nrm((DEPTH, 2, G, H, P), P ** -0.5),
        "ssm_log_dt": jax.random.uniform(next(ks), (DEPTH, 2, G), f32, math.log(DT_MIN), math.log(DT_MAX)),
        "ssm_d": nrm((DEPTH, SSM_WIDTH), 1.0),
        "glu_w": nrm((DEPTH, SSM_WIDTH, SSM_WIDTH), SSM_WIDTH ** -0.5),
        "glu_b": nrm((DEPTH, SSM_WIDTH), 0.01),
        "q_norm_g": 1.0 + nrm((DEPTH, HEAD_DIM), 0.02),
        "k_norm_g": 1.0 + nrm((DEPTH, HEAD_DIM), 0.02),
        "na_rpb": nrm((DEPTH, NA_HEADS, 2 * NA_WIN_R - 1, 2 * NA_WIN_C - 1), 0.02),
        "w_branch_ssm": nrm((DEPTH, SSM_WIDTH, D), SSM_WIDTH ** -0.5),
        "w_branch_gqa": nrm((DEPTH, GQA_Q_W, D), GQA_Q_W ** -0.5),
        "w_branch_na": nrm((DEPTH, NA_W, D), NA_W ** -0.5),
        "w_out": nrm((DEPTH, D, D), D ** -0.5),
        "norm2_g": 1.0 + nrm((DEPTH, D), 0.02),
        "ffn_w_gate": nrm((N_DENSE, D, FFN_DIM), D ** -0.5),
        "ffn_w_up": nrm((N_DENSE, D, FFN_DIM), D ** -0.5),
        "ffn_w_down": nrm((N_DENSE, FFN_DIM, D), FFN_DIM ** -0.5),
        "router_w": nrm((N_MOE, D, N_EXPERTS), D ** -0.5),
        "moe_w_gate": nrm((N_MOE, N_EXPERTS, D, EXPERT_DIM), D ** -0.5),
        "moe_w_up": nrm((N_MOE, N_EXPERTS, D, EXPERT_DIM), D ** -0.5),
        "moe_w_down": nrm((N_MOE, N_EXPERTS, EXPERT_DIM, D), EXPERT_DIM ** -0.5),
        "final_norm_g": 1.0 + nrm((D,), 0.02),
    }


def reference(x, c, ctx, c_ctx, w_mod, b_mod, norm1_g, w_in, ssm_a_re, ssm_a_im, ssm_b_re, ssm_b_im,
              ssm_c_re, ssm_c_im, ssm_log_dt, ssm_d, glu_w, glu_b, q_norm_g, k_norm_g, na_rpb,
              w_branch_ssm, w_branch_gqa, w_branch_na, w_out, norm2_g, ffn_w_gate, ffn_w_up, ffn_w_down,
              router_w, moe_w_gate, moe_w_up, moe_w_down, final_norm_g):
    n_lat = x.shape[1]
    cos, sin = axial_rope(n_lat, x.dtype)
    xc = ctx
    for i in range(DEPTH):
        need_ctx = i < DEPTH - 1
        sh1, sc1, g1, sh2, sc2, g2 = [m[:, None, :] for m in adaln(c, w_mod[i], b_mod[i], 6)]
        mod_c = adaln(c_ctx, w_mod[i], b_mod[i], 6 if need_ctx else 2)

        h = modulate(rms_norm(x, norm1_g[i]), sh1, sc1)
        hc = modulate(rms_norm(xc, norm1_g[i]), mod_c[0], mod_c[1])
        z = h @ w_in[i]
        zc = hc @ (w_in[i] if need_ctx else w_in[i][:, :KV_COLS])
        u, ka, va, kn, vn, qa, qn, gl = jnp.split(z, IN_SPLITS, axis=-1)
        zc_parts = jnp.split(zc, IN_SPLITS if need_ctx else IN_SPLITS[:4], axis=-1)
        uc, kac, vac, knc, vnc = zc_parts[:5]
        qac, qnc, glc = zc_parts[5:] if need_ctx else (None, None, None)

        y_ssm, y_ssm_c = s5_bidirectional(u, uc, ssm_a_re[i], ssm_a_im[i], ssm_b_re[i], ssm_b_im[i],
                                          ssm_c_re[i], ssm_c_im[i], ssm_log_dt[i], ssm_d[i],
                                          glu_w[i], glu_b[i], need_ctx)
        y_gqa, y_gqa_c = gqa_branch(qa, ka, va, qac, kac, vac, q_norm_g[i], k_norm_g[i], cos, sin)
        y_na, y_na_c = na_branch(qn, kn, vn, qnc, knc, vnc, na_rpb[i])

        x = x + g1 * merge_branches(y_ssm, y_gqa, y_na, gl, w_branch_ssm[i], w_branch_gqa[i],
                                    w_branch_na[i], w_out[i])
        if need_ctx:
            xc = xc + mod_c[2] * merge_branches(y_ssm_c, y_gqa_c, y_na_c, glc, w_branch_ssm[i],
                                                w_branch_gqa[i], w_branch_na[i], w_out[i])

        j = i // 2
        if i % 2 == 0:
            ffn = functools.partial(swiglu, w_gate=ffn_w_gate[j], w_up=ffn_w_up[j], w_down=ffn_w_down[j])
        else:
            ffn = functools.partial(moe_swiglu, router_w=router_w[j], w_gate=moe_w_gate[j],
                                    w_up=moe_w_up[j], w_down=moe_w_down[j])
        x = x + g2 * ffn(modulate(rms_norm(x, norm2_g[i]), sh2, sc2))
        if need_ctx:
            xc = xc + mod_c[5] * ffn(modulate(rms_norm(xc, norm2_g[i]), mod_c[3], mod_c[4]))
    return rms_norm(x, final_norm_g)
```

```python
import functools
import math

import jax
import jax.numpy as jnp
from jax import lax
from jax.experimental import pallas as pl
from jax.experimental.pallas import tpu as pltpu

D_MODEL = 1024
GRID_W = 64
SSM_WIDTH = 512
SSM_GROUP = 16
SSM_GROUPS = SSM_WIDTH // SSM_GROUP
SSM_STATE = 64
HEAD_DIM = 64
GQA_HEADS = 8
GQA_KV_HEADS = 2
GQA_GROUP = GQA_HEADS // GQA_KV_HEADS
ROPE_BASE = 10000.0
NA_HEADS = 8
NA_WIN_R = 8
NA_WIN_C = 16
ATTN_SCALE = HEAD_DIM ** -0.5
GQA_Q_W = GQA_HEADS * HEAD_DIM
GQA_KV_W = GQA_KV_HEADS * HEAD_DIM
NA_W = NA_HEADS * HEAD_DIM
KV_COLS = SSM_WIDTH + 2 * GQA_KV_W + 2 * NA_W
N_EXPERTS = 8
NORM_EPS = 1e-6

LANES = 128
TOKEN_TILE = 256
SSM_CHUNK = 16
SSM_QT = SSM_WIDTH // LANES
SSM_GPT = LANES // SSM_GROUP
SSM_ST = SSM_GPT * SSM_STATE
NA_QROWS = 4
NA_UNION = 12
MOE_TILE = 512
MOE_FCHUNK = 896
NEG = -1e30

BF = jnp.bfloat16
F32 = jnp.float32


def _cp(sem, vmem_mb=48):
    return pltpu.CompilerParams(dimension_semantics=sem, vmem_limit_bytes=vmem_mb * 1024 * 1024)


def _const_spec(shape):
    nd = len(shape)
    return pl.BlockSpec(shape, lambda *_: (0,) * nd, pipeline_mode=pl.Buffered(1))


def _dot(a, b):
    return jnp.dot(a, b, preferred_element_type=F32)


def _dot_nt(a, b):
    return lax.dot_general(a, b, (((1,), (1,)), ((), ())), preferred_element_type=F32)


def _rms(x, g):
    return x * lax.rsqrt(jnp.mean(x * x, axis=-1, keepdims=True) + NORM_EPS) * g


def _silu(x):
    return x * jax.nn.sigmoid(x)


def _gelu_tanh(x):
    return 0.5 * x * (1.0 + jnp.tanh(math.sqrt(2.0 / math.pi) * (x + 0.044715 * x * x * x)))


def _adaln_kernel(cv_ref, w_ref, b_ref, o_ref):
    a = _silu(cv_ref[...])
    w = w_ref[0]
    a_hi = a.astype(BF)
    a_lo = (a - a_hi.astype(F32)).astype(BF)
    w_hi = w.astype(BF)
    w_lo = (w - w_hi.astype(F32)).astype(BF)
    o_ref[0] = _dot(a_hi, w_hi) + _dot(a_hi, w_lo) + _dot(a_lo, w_hi) + b_ref[0]


def adaln(cv, w_mod, b_mod):
    depth, d, n = w_mod.shape
    r = cv.shape[0]
    tn = 512
    return pl.pallas_call(
        _adaln_kernel,
        grid=(depth, n // tn),
        in_specs=[pl.BlockSpec((r, d), lambda i, j: (0, 0)),
                  pl.BlockSpec((1, d, tn), lambda i, j: (i, 0, j)),
                  pl.BlockSpec((1, 1, tn), lambda i, j: (i, 0, j))],
        out_specs=pl.BlockSpec((1, r, tn), lambda i, j: (i, 0, j)),
        out_shape=jax.ShapeDtypeStruct((depth, r, n), F32),
        compiler_params=_cp(("arbitrary", "arbitrary")),
        name="adaln",
    )(cv, w_mod, b_mod.reshape(depth, 1, n))


def _rope(xn, cos, sin):
    lane = lax.broadcasted_iota(jnp.int32, xn.shape, 1)
    first = (lane % 32) < 16
    rot = jnp.where(first, -pltpu.roll(xn, LANES - 16, 1), pltpu.roll(xn, 16, 1))
    return xn * cos + rot * sin


def _inproj_kernel(x_ref, mod_ref, g_ref, cos_ref, sin_ref, wu_ref, wkv_ref, wqg_ref, wqn_ref, wgt_ref,
                   qgain_ref, kgain_ref,
                   u_ref, kg_ref, vg_ref, kn_ref, vn_ref, qg_ref, qn_ref, gt_ref):
    x = x_ref[0]
    mod = mod_ref[0]
    h = _rms(x, g_ref[...]) * (1.0 + mod[1:2]) + mod[0:1]
    hb = h.astype(BF)
    cos = cos_ref[...]
    sin = sin_ref[...]

    u = _dot(hb, wu_ref[...])
    for q in range(SSM_QT):
        u_ref[q, 0] = u[:, q * LANES:(q + 1) * LANES].astype(BF)

    kv = _dot(hb, wkv_ref[...])
    kg = kv[:, :LANES]
    lane = lax.broadcasted_iota(jnp.int32, kg.shape, 1)
    lo = lane < HEAD_DIM
    sq = kg * kg
    ms_lo = jnp.sum(jnp.where(lo, sq, 0.0), axis=-1, keepdims=True)
    ms_hi = jnp.sum(jnp.where(lo, 0.0, sq), axis=-1, keepdims=True)
    ms = jnp.where(lo, ms_lo, ms_hi) * (1.0 / HEAD_DIM)
    kgn = kg * lax.rsqrt(ms + NORM_EPS) * kgain_ref[...]
    kg_ref[0] = _rope(kgn, cos, sin).astype(BF)
    vg_ref[0] = kv[:, LANES:2 * LANES].astype(BF)
    kn_ref[0] = kv[:, 2 * LANES:2 * LANES + NA_W].astype(BF)
    vn_ref[0] = kv[:, 2 * LANES + NA_W:].astype(BF)

    qg = _dot(hb, wqg_ref[...])
    for s in range(GQA_HEADS):
        qs = qg[:, s * LANES:(s + 1) * LANES]
        ms = jnp.sum(qs * qs, axis=-1, keepdims=True) * (1.0 / HEAD_DIM)
        qsn = qs * lax.rsqrt(ms + NORM_EPS) * qgain_ref[:, s * LANES:(s + 1) * LANES]
        qg_ref[0, :, s * LANES:(s + 1) * LANES] = (_rope(qsn, cos, sin) * ATTN_SCALE).astype(BF)

    qn_ref[0] = (_dot(hb, wqn_ref[...]) * ATTN_SCALE).astype(BF)
    gt_ref[0] = jax.nn.sigmoid(_dot(hb, wgt_ref[...])).astype(BF)


def inproj(xall, mod, g, cos, sin, wu, wkv, wqg, wqn, wgt, qgain, kgain, n_lat_tiles):
    b, s, d = xall.shape
    tm = TOKEN_TILE
    nt = s // tm
    nb = mod.shape[0] - 1

    def row(bi, j):
        return (bi, j, 0)

    def modrow(bi, j):
        return (jnp.where(j < n_lat_tiles, bi, nb), 0, 0)

    outs = [
        jax.ShapeDtypeStruct((SSM_QT, b, s, LANES), BF),
        jax.ShapeDtypeStruct((b, s, LANES), BF),
        jax.ShapeDtypeStruct((b, s, LANES), BF),
        jax.ShapeDtypeStruct((b, s, NA_W), BF),
        jax.ShapeDtypeStruct((b, s, NA_W), BF),
        jax.ShapeDtypeStruct((b, s, GQA_HEADS * LANES), BF),
        jax.ShapeDtypeStruct((b, s, NA_HEADS * LANES), BF),
        jax.ShapeDtypeStruct((b, s, 3 * d), BF),
    ]
    out_specs = [
        pl.BlockSpec((SSM_QT, 1, tm, LANES), lambda bi, j: (0, bi, j, 0)),
        pl.BlockSpec((1, tm, LANES), row),
        pl.BlockSpec((1, tm, LANES), row),
        pl.BlockSpec((1, tm, NA_W), row),
        pl.BlockSpec((1, tm, NA_W), row),
        pl.BlockSpec((1, tm, GQA_HEADS * LANES), row),
        pl.BlockSpec((1, tm, NA_HEADS * LANES), row),
        pl.BlockSpec((1, tm, 3 * d), row),
    ]
    in_specs = [
        pl.BlockSpec((1, tm, d), row),
        pl.BlockSpec((1, 6, d), modrow),
        _const_spec(g.shape),
        pl.BlockSpec((tm, LANES), lambda bi, j: (j, 0)),
        pl.BlockSpec((tm, LANES), lambda bi, j: (j, 0)),
        _const_spec(wu.shape), _const_spec(wkv.shape), _const_spec(wqg.shape),
        _const_spec(wqn.shape), _const_spec(wgt.shape), _const_spec(qgain.shape), _const_spec(kgain.shape),
    ]
    return pl.pallas_call(
        _inproj_kernel, grid=(b, nt), in_specs=in_specs, out_specs=out_specs, out_shape=outs,
        compiler_params=_cp(("arbitrary", "arbitrary"), 56), name="inproj",
    )(xall, mod, g, cos, sin, wu, wkv, wqg, wqn, wgt, qgain, kgain)


def _ssm_in_kernel(u_ref, m_ref, zf_ref, zr_ref):
    z = _dot(u_ref[0], m_ref[0])
    half = z.shape[1] // 2
    zf_ref[...] = z[:, :half]
    zr_ref[...] = z[:, half:]


def ssm_in(uq, m1, rows_blk):
    qt, r, k = uq.shape
    w = 2 * SSM_ST
    return pl.pallas_call(
        _ssm_in_kernel, grid=(qt, r // rows_blk),
        in_specs=[pl.BlockSpec((1, rows_blk, k), lambda q, i: (q, i, 0)),
                  pl.BlockSpec((1, k, 2 * w), lambda q, i: (q, 0, 0))],
        out_specs=[pl.BlockSpec((rows_blk, w), lambda q, i: (i, q)),
                   pl.BlockSpec((rows_blk, w), lambda q, i: (i, q))],
        out_shape=[jax.ShapeDtypeStruct((r, qt * w), F32)] * 2,
        compiler_params=_cp(("arbitrary", "arbitrary")), name="ssm_in",
    )(uq, m1)


def _ssm_scan_kernel(zf_ref, zr_ref, af_ref, ar_ref, sf_ref, sr_ref, *, n_lat_chunks):
    nk = zf_ref.shape[1]
    st = SSM_ST
    af_re, af_im = af_ref[:, :st], af_ref[:, st:]
    ar_re, ar_im = ar_ref[:, :st], ar_ref[:, st:]

    def step(i, carry):
        fre, fim, rre, rim = carry
        kf = lax.rem(i + n_lat_chunks, nk)
        kr = nk - 1 - i
        sf_ref[0, pl.ds(kf, 1), :] = jnp.concatenate([fre, fim], axis=1)
        sr_ref[0, pl.ds(kr, 1), :] = jnp.concatenate([rre, rim], axis=1)
        zf = zf_ref[0, pl.ds(kf, 1), :]
        zr = zr_ref[0, pl.ds(kr, 1), :]
        nfre = af_re * fre - af_im * fim + zf[:, :st]
        nfim = af_re * fim + af_im * fre + zf[:, st:]
        nrre = ar_re * rre - ar_im * rim + zr[:, :st]
        nrim = ar_re * rim + ar_im * rre + zr[:, st:]
        return nfre, nfim, nrre, nrim

    z0 = jnp.zeros((1, st), F32)
    lax.fori_loop(0, nk, step, (z0, z0, z0, z0))


def ssm_scan(zf, zr, af, ar, n_lat_chunks):
    b, nk, n = zf.shape
    w = 2 * SSM_ST
    spec = pl.BlockSpec((1, nk, w), lambda bi, q: (bi, 0, q))
    aspec = pl.BlockSpec((1, w), lambda bi, q: (0, q))
    return pl.pallas_call(
        functools.partial(_ssm_scan_kernel, n_lat_chunks=n_lat_chunks),
        grid=(b, n // w), in_specs=[spec, spec, aspec, aspec], out_specs=[spec, spec],
        out_shape=[jax.ShapeDtypeStruct((b, nk, n), F32)] * 2,
        compiler_params=_cp(("arbitrary", "arbitrary")), name="ssm_scan",
    )(zf, zr, af, ar)


def _ssm_out_kernel(u_ref, sf_ref, sr_ref, mu_ref, mf_ref, mr_ref, y_ref):
    y = _dot(u_ref[0], mu_ref[0])
    y += _dot(sf_ref[...].astype(BF), mf_ref[0])
    y += _dot(sr_ref[...].astype(BF), mr_ref[0])
    y_ref[0] = y.astype(y_ref.dtype)


def ssm_out(uq, sf, sr, m3u, m3f, m3r, rows_blk):
    qt, r, k = uq.shape
    w = 2 * SSM_ST
    nh = 2
    kn = k // nh
    return pl.pallas_call(
        _ssm_out_kernel, grid=(qt, nh, r // rows_blk),
        in_specs=[pl.BlockSpec((1, rows_blk, k), lambda q, n, i: (q, i, 0)),
                  pl.BlockSpec((rows_blk, w), lambda q, n, i: (i, q)),
                  pl.BlockSpec((rows_blk, w), lambda q, n, i: (i, q)),
                  pl.BlockSpec((1, k, kn), lambda q, n, i: (q, 0, n)),
                  pl.BlockSpec((1, w, kn), lambda q, n, i: (q, 0, n)),
                  pl.BlockSpec((1, w, kn), lambda q, n, i: (q, 0, n))],
        out_specs=pl.BlockSpec((1, rows_blk, kn), lambda q, n, i: (q, i, n)),
        out_shape=jax.ShapeDtypeStruct((qt, r, k), BF),
        compiler_params=_cp(("arbitrary", "arbitrary", "arbitrary")), name="ssm_out",
    )(uq, sf, sr, m3u, m3f, m3r)


def s5_matrices(a_re, a_im, b_re, b_im, c_re, c_im, log_dt, d_skip):
    tc, g, p, hh = SSM_CHUNK, SSM_GROUPS, SSM_STATE, SSM_GROUP
    lam = lax.complex(a_re.astype(F32), a_im.astype(F32))
    dt = jnp.exp(log_dt.astype(F32))[..., None]
    ldt = lam * dt
    a_bar = jnp.exp(ldt)
    b_bar = ((a_bar - 1.0) / lam)[..., None] * lax.complex(b_re.astype(F32), b_im.astype(F32))
    cm = lax.complex(c_re.astype(F32), c_im.astype(F32))
    pw = jnp.exp(ldt[None] * jnp.arange(tc + 1, dtype=F32)[:, None, None, None])
    eye = jnp.eye(SSM_GPT, dtype=F32)

    kk = jnp.real(jnp.einsum('dghp,ldgp,dgpk->dlghk', cm, pw[:tc], b_bar))
    jj = jnp.arange(tc)[:, None]
    ii = jnp.arange(tc)[None, :]
    kf = jnp.where((ii >= jj)[..., None, None, None], kk[0][jnp.clip(ii - jj, 0, tc - 1)], 0.0)
    kr = jnp.where((jj >= ii)[..., None, None, None], kk[1][jnp.clip(jj - ii, 0, tc - 1)], 0.0)
    dd = (jnp.eye(tc, dtype=F32)[:, :, None, None, None] * jnp.eye(hh, dtype=F32)[None, None, None]
          * d_skip.astype(F32).reshape(g, hh)[None, None, :, :, None])
    tt = (kf + kr + dd).reshape(tc, tc, SSM_QT, SSM_GPT, hh, hh)
    m3u = jnp.einsum('jiqghk,gm->qjgkimh', tt, eye).reshape(SSM_QT, tc * LANES, tc * LANES)

    inf = pw[tc - 1 - jnp.arange(tc), 0][..., None] * b_bar[0][None]
    inr = pw[jnp.arange(tc), 1][..., None] * b_bar[1][None]
    m1 = jnp.stack([jnp.real(inf), jnp.imag(inf), jnp.real(inr), jnp.imag(inr)], axis=0)
    m1 = m1.reshape(2, 2, tc, SSM_QT, SSM_GPT, p, hh)
    m1 = jnp.einsum('drjqgpk,gm->qjgkdrmp', m1, eye).reshape(SSM_QT, tc * LANES, 4 * SSM_ST)

    def out_mat(d, powers):
        ca = cm[d][None] * pw[powers, d][:, :, None, :]
        mm = jnp.stack([jnp.real(ca), -jnp.imag(ca)], axis=0)
        mm = mm.reshape(2, tc, SSM_QT, SSM_GPT, hh, p)
        return jnp.einsum('riqghp,gm->qrgpimh', mm, eye).reshape(SSM_QT, 2 * SSM_ST, tc * LANES)

    m3f = out_mat(0, jnp.arange(tc) + 1)
    m3r = out_mat(1, tc - jnp.arange(tc))

    def decay(d):
        a16 = pw[tc, d].reshape(SSM_QT, SSM_ST)
        return jnp.concatenate([jnp.real(a16), jnp.imag(a16)], axis=1).reshape(1, SSM_QT * 2 * SSM_ST)

    return m1.astype(BF), m3u.astype(BF), m3f.astype(BF), m3r.astype(BF), decay(0), decay(1)


def s5_branch(uq4, mats, n_lat_chunks):
    m1, m3u, m3f, m3r, a16f, a16r = mats
    qt, b, s, _ = uq4.shape
    nk = s // SSM_CHUNK
    uq = uq4.reshape(qt, b * nk, SSM_CHUNK * LANES)
    zf, zr = ssm_in(uq, m1, nk)
    sf, sr = ssm_scan(zf.reshape(b, nk, -1), zr.reshape(b, nk, -1), a16f, a16r, n_lat_chunks)
    y = ssm_out(uq, sf.reshape(b * nk, -1), sr.reshape(b * nk, -1), m3u, m3f, m3r, nk)
    return y.reshape(qt, b, s, LANES)


def _gqa_kernel(q_ref, k_ref, v_ref, o_ref, q4_ref, m_ref, l_ref, acc_ref, *, n_lat, n_ctx, tk, n_lat_tiles):
    tq = q_ref.shape[1]
    qi = pl.program_id(2)
    for s in range(GQA_GROUP):
        q4_ref[s * tq:(s + 1) * tq, :] = q_ref[0, :, s * LANES:(s + 1) * LANES]
    m_ref[...] = jnp.full(m_ref.shape, NEG, F32)
    l_ref[...] = jnp.zeros(l_ref.shape, F32)
    acc_ref[...] = jnp.zeros(acc_ref.shape, F32)

    def update(kc, vc):
        s = _dot_nt(q4_ref[...], kc)
        m_old = m_ref[...]
        m_new = jnp.maximum(m_old, jnp.max(s, axis=-1, keepdims=True))
        alpha = jnp.exp(m_old - m_new)
        p = jnp.exp(s - m_new)
        l_ref[...] = alpha * l_ref[...] + jnp.sum(p, axis=-1, keepdims=True)
        acc_ref[...] = alpha * acc_ref[...] + _dot(p.astype(BF), vc)
        m_ref[...] = m_new

    def chunk(c, carry):
        off = pl.multiple_of(c * tk, tk)
        update(k_ref[0, pl.ds(off, tk), :], v_ref[0, pl.ds(off, tk), :])
        return carry

    lax.fori_loop(0, jnp.where(qi < n_lat_tiles, n_lat // tk, 0), chunk, 0)
    update(k_ref[0, n_lat:n_lat + n_ctx, :], v_ref[0, n_lat:n_lat + n_ctx, :])

    o = acc_ref[...] / l_ref[...]
    for s in range(GQA_GROUP):
        o_ref[0, :, s * LANES:(s + 1) * LANES] = o[s * tq:(s + 1) * tq].astype(o_ref.dtype)


def gqa_attention(qg, kg, vg, n_lat, n_q_tiles):
    b, s, _ = qg.shape
    tq = TOKEN_TILE
    tk = 512
    gw = GQA_GROUP * LANES
    kern = functools.partial(_gqa_kernel, n_lat=n_lat, n_ctx=s - n_lat, tk=tk, n_lat_tiles=n_lat // tq)
    return pl.pallas_call(
        kern, grid=(b, GQA_KV_HEADS, n_q_tiles),
        in_specs=[pl.BlockSpec((1, tq, gw), lambda bi, g, i: (bi, i, g)),
                  pl.BlockSpec((1, s, LANES), lambda bi, g, i: (bi, 0, 0)),
                  pl.BlockSpec((1, s, LANES), lambda bi, g, i: (bi, 0, 0))],
        out_specs=pl.BlockSpec((1, tq, gw), lambda bi, g, i: (bi, i, g)),
        out_shape=jax.ShapeDtypeStruct((b, n_q_tiles * tq, GQA_HEADS * LANES), BF),
        scratch_shapes=[pltpu.VMEM((GQA_GROUP * tq, LANES), BF),
                        pltpu.VMEM((GQA_GROUP * tq, 1), F32),
                        pltpu.VMEM((GQA_GROUP * tq, 1), F32),
                        pltpu.VMEM((GQA_GROUP * tq, LANES), F32)],
        compiler_params=_cp(("arbitrary", "arbitrary", "arbitrary")), name="gqa",
    )(qg, kg, vg)


def _na_kernel(q_ref, k_ref, v_ref, tab_ref, o_ref, *, n_lat, n_ctx, rows):
    rb = pl.program_id(1)
    ws = jnp.clip(rb * NA_QROWS - NA_WIN_R // 2, 0, rows - NA_UNION)
    off = pl.multiple_of(ws * GRID_W, GRID_W)
    nwin = NA_UNION * GRID_W
    for pair in range(NA_HEADS // 2):
        ls = slice(pair * LANES, (pair + 1) * LANES)
        kcat = jnp.concatenate([k_ref[0, pl.ds(off, nwin), ls], k_ref[0, n_lat:n_lat + n_ctx, ls]], axis=0)
        vcat = jnp.concatenate([v_ref[0, pl.ds(off, nwin), ls], v_ref[0, n_lat:n_lat + n_ctx, ls]], axis=0)
        for half in range(2):
            h = pair * 2 + half
            hs = slice(h * LANES, (h + 1) * LANES)
            s = _dot_nt(q_ref[0, :, hs], kcat) + tab_ref[0, h].astype(F32)
            m = jnp.max(s, axis=-1, keepdims=True)
            p = jnp.exp(s - m)
            l = jnp.sum(p, axis=-1, keepdims=True)
            o_ref[0, :, hs] = (_dot(p.astype(BF), vcat) / l).astype(o_ref.dtype)


def na_attention(qn, kn, vn, table, n_lat, n_q_tiles):
    b, s, _ = qn.shape
    tq = NA_QROWS * GRID_W
    rows = n_lat // GRID_W
    n_lat_tiles = n_lat // tq
    tw = table.shape[-1]

    def tab_idx(bi, rb):
        case = jnp.where(rb == 0, 0, jnp.where(rb < n_lat_tiles - 1, 1, jnp.where(rb == n_lat_tiles - 1, 2, 3)))
        return (case, 0, 0, 0)

    kern = functools.partial(_na_kernel, n_lat=n_lat, n_ctx=s - n_lat, rows=rows)
    return pl.pallas_call(
        kern, grid=(b, n_q_tiles),
        in_specs=[pl.BlockSpec((1, tq, NA_HEADS * LANES), lambda bi, rb: (bi, rb, 0)),
                  pl.BlockSpec((1, s, NA_W), lambda bi, rb: (bi, 0, 0)),
                  pl.BlockSpec((1, s, NA_W), lambda bi, rb: (bi, 0, 0)),
                  pl.BlockSpec((1, NA_HEADS, tq, tw), tab_idx)],
        out_specs=pl.BlockSpec((1, tq, NA_HEADS * LANES), lambda bi, rb: (bi, rb, 0)),
        out_shape=jax.ShapeDtypeStruct((b, n_q_tiles * tq, NA_HEADS * LANES), BF),
        compiler_params=_cp(("arbitrary", "arbitrary"), 56), name="na",
    )(qn, kn, vn, table)


def na_table(rpb, rows, n_ctx):
    tq = NA_QROWS * GRID_W
    qr = jnp.arange(NA_QROWS)[:, None, None, None]
    qc = jnp.arange(GRID_W)[None, :, None, None]
    kr = jnp.arange(NA_UNION)[None, None, :, None]
    kc = jnp.arange(GRID_W)[None, None, None, :]
    cs = jnp.clip(qc - NA_WIN_C // 2, 0, GRID_W - NA_WIN_C)
    col_ok = (kc >= cs) & (kc < cs + NA_WIN_C)
    col_idx = jnp.clip(kc - qc + NA_WIN_C - 1, 0, 2 * NA_WIN_C - 2)
    n_blocks = rows // NA_QROWS
    tabs = []
    for blk in (0, 1, n_blocks - 1):
        r = blk * NA_QROWS + qr
        ws = min(max(blk * NA_QROWS - NA_WIN_R // 2, 0), rows - NA_UNION)
        rs = jnp.clip(r - NA_WIN_R // 2, 0, rows - NA_WIN_R)
        ka = ws + kr
        ok = (ka >= rs) & (ka < rs + NA_WIN_R) & col_ok
        row_idx = jnp.clip(ka - r + NA_WIN_R - 1, 0, 2 * NA_WIN_R - 2)
        bias = rpb.astype(F32)[:, row_idx, col_idx]
        tabs.append(jnp.where(ok[None], bias, NEG).reshape(NA_HEADS, tq, NA_UNION * GRID_W))
    tabs.append(jnp.full((NA_HEADS, tq, NA_UNION * GRID_W), NEG, F32))
    win = jnp.stack(tabs, axis=0)
    return jnp.concatenate([win, jnp.zeros(win.shape[:3] + (n_ctx,), F32)], axis=-1).astype(BF)


def _merge_kernel(x_ref, y_ref, og_ref, on_ref, gt_ref, mod_ref, gw_ref, gb_ref, ws_ref, wg_ref, wn_ref, wo_ref,
                  o_ref):
    d = x_ref.shape[-1]
    y = jnp.concatenate([y_ref[q, 0].astype(F32) for q in range(SSM_QT)], axis=1)
    gy = _gelu_tanh(y)
    ys = gy * jax.nn.sigmoid(_dot(gy.astype(BF), gw_ref[...]) + gb_ref[...])
    gt = gt_ref[0]
    m = gt[:, :d].astype(F32) * _dot(ys.astype(BF), ws_ref[...])
    m += gt[:, d:2 * d].astype(F32) * _dot(og_ref[0], wg_ref[...])
    m += gt[:, 2 * d:].astype(F32) * _dot(on_ref[0], wn_ref[...])
    o_ref[0] = x_ref[0] + mod_ref[0][2:3] * _dot(m.astype(BF), wo_ref[...])


def merge(xall, y4, og, on, gt, mod, gw, gb, ws, wg, wn, wo, n_tiles, n_lat_tiles):
    b, s, d = xall.shape
    tm = TOKEN_TILE
    nb = mod.shape[0] - 1

    def row(bi, j):
        return (bi, j, 0)

    return pl.pallas_call(
        _merge_kernel, grid=(b, n_tiles),
        in_specs=[pl.BlockSpec((1, tm, d), row),
                  pl.BlockSpec((SSM_QT, 1, tm, LANES), lambda bi, j: (0, bi, j, 0)),
                  pl.BlockSpec((1, tm, GQA_HEADS * LANES), row),
                  pl.BlockSpec((1, tm, NA_HEADS * LANES), row),
                  pl.BlockSpec((1, tm, 3 * d), row),
                  pl.BlockSpec((1, 6, d), lambda bi, j: (jnp.where(j < n_lat_tiles, bi, nb), 0, 0)),
                  _const_spec(gw.shape), _const_spec(gb.shape), _const_spec(ws.shape),
                  _const_spec(wg.shape), _const_spec(wn.shape), _const_spec(wo.shape)],
        out_specs=pl.BlockSpec((1, tm, d), row),
        out_shape=jax.ShapeDtypeStruct((b, n_tiles * tm, d), F32),
        compiler_params=_cp(("arbitrary", "arbitrary")), name="merge",
    )(xall, y4, og, on, gt, mod, gw, gb, ws, wg, wn, wo)


def _ffn_kernel(x_ref, mod_ref, g_ref, wg_ref, wu_ref, wd_ref, o_ref):
    x = x_ref[0]
    mod = mod_ref[0]
    hb = (_rms(x, g_ref[...]) * (1.0 + mod[4:5]) + mod[3:4]).astype(BF)
    a = _silu(_dot(hb, wg_ref[...])) * _dot(hb, wu_ref[...])
    o_ref[0] = x + mod[5:6] * _dot(a.astype(BF), wd_ref[...])


def ffn(xall, mod, g, wg, wu, wd, n_lat_tiles):
    b, s, d = xall.shape
    tm = TOKEN_TILE
    nb = mod.shape[0] - 1

    def row(bi, j):
        return (bi, j, 0)

    return pl.pallas_call(
        _ffn_kernel, grid=(b, s // tm),
        in_specs=[pl.BlockSpec((1, tm, d), row),
                  pl.BlockSpec((1, 6, d), lambda bi, j: (jnp.where(j < n_lat_tiles, bi, nb), 0, 0)),
                  _const_spec(g.shape), _const_spec(wg.shape), _const_spec(wu.shape), _const_spec(wd.shape)],
        out_specs=pl.BlockSpec((1, tm, d), row),
        out_shape=jax.ShapeDtypeStruct((b, s, d), F32),
        compiler_params=_cp(("arbitrary", "arbitrary"), 56), name="ffn",
    )(xall, mod, g, wg, wu, wd)


def _route_kernel(x_ref, mod_ref, g_ref, rw_ref, h_ref, info_ref):
    x = x_ref[0]
    mod = mod_ref[0]
    h = _rms(x, g_ref[...]) * (1.0 + mod[4:5]) + mod[3:4]
    h_ref[0] = h
    logits = jnp.dot(h, rw_ref[...], preferred_element_type=F32, precision=lax.Precision.HIGHEST)
    lane = lax.broadcasted_iota(jnp.int32, logits.shape, 1)
    lanef = lane.astype(F32)
    logits = jnp.where(lane < N_EXPERTS, logits, -jnp.inf)
    m1 = jnp.max(logits, axis=-1, keepdims=True)
    i1 = jnp.min(jnp.where(logits == m1, lanef, float(LANES)), axis=-1, keepdims=True)
    rest = jnp.where(lanef == i1, -jnp.inf, logits)
    m2 = jnp.max(rest, axis=-1, keepdims=True)
    i2 = jnp.min(jnp.where(rest == m2, lanef, float(LANES)), axis=-1, keepdims=True)
    e2 = jnp.exp(m2 - m1)
    w1 = 1.0 / (1.0 + e2)
    w2 = e2 / (1.0 + e2)
    info_ref[0] = jnp.where(lane == 0, i1, jnp.where(lane == 1, i2, jnp.where(lane == 2, w1,
                            jnp.where(lane == 3, w2, 0.0))))


def moe_route(xall, mod, g, rw, n_tiles):
    b, s, d = xall.shape
    tm = TOKEN_TILE

    def row(bi, j):
        return (bi, j, 0)

    return pl.pallas_call(
        _route_kernel, grid=(b, n_tiles),
        in_specs=[pl.BlockSpec((1, tm, d), row), pl.BlockSpec((1, 6, d), lambda bi, j: (bi, 0, 0)),
                  _const_spec(g.shape), _const_spec(rw.shape)],
        out_specs=[pl.BlockSpec((1, tm, d), row), pl.BlockSpec((1, tm, LANES), row)],
        out_shape=[jax.ShapeDtypeStruct((b, n_tiles * tm, d), F32),
                   jax.ShapeDtypeStruct((b, n_tiles * tm, LANES), F32)],
        compiler_params=_cp(("arbitrary", "arbitrary")), name="moe_route",
    )(xall, mod, g, rw)


def _dispatch_kernel(pos_ref, h_ref, xs_in_ref, xs_ref, sem):
    del xs_in_ref
    tm = h_ref.shape[0]

    def row_copy(r, k):
        return pltpu.make_async_copy(h_ref.at[pl.ds(r, 1)], xs_ref.at[pl.ds(pos_ref[0, 0, 2 * r + k], 1)], sem)

    def issue(r, c):
        row_copy(r, 0).start()
        row_copy(r, 1).start()
        return c

    def drain(r, c):
        row_copy(r, 0).wait()
        row_copy(r, 1).wait()
        return c

    lax.fori_loop(0, tm, issue, 0)
    lax.fori_loop(0, tm, drain, 0)


def moe_dispatch(h2, pos, n_slots):
    t, d = h2.shape
    tm = TOKEN_TILE
    zeros = jnp.zeros((n_slots, d), F32)
    return pl.pallas_call(
        _dispatch_kernel, grid=(t // tm,),
        in_specs=[pl.BlockSpec((1, 1, 2 * tm), lambda i: (i, 0, 0), memory_space=pltpu.SMEM),
                  pl.BlockSpec((tm, d), lambda i: (i, 0)),
                  pl.BlockSpec(memory_space=pl.ANY)],
        out_specs=pl.BlockSpec(memory_space=pl.ANY),
        out_shape=jax.ShapeDtypeStruct((n_slots, d), F32),
        scratch_shapes=[pltpu.SemaphoreType.DMA(())],
        input_output_aliases={2: 0},
        compiler_params=_cp(("arbitrary",)), name="moe_dispatch",
    )(pos, h2, zeros)


def _experts_kernel(te_ref, nv_ref, x_ref, wg_ref, wu_ref, wd_ref, o_ref, acc_ref):
    i = pl.program_id(0)
    j = pl.program_id(1)

    @pl.when(i < nv_ref[0])
    def _():
        xb = x_ref[...].astype(BF)
        a = _silu(_dot(xb, wg_ref[0])) * _dot(xb, wu_ref[0])
        part = _dot(a.astype(BF), wd_ref[0])

        @pl.when(j == 0)
        def _():
            acc_ref[...] = part

        @pl.when(j > 0)
        def _():
            acc_ref[...] += part

        @pl.when(j == pl.num_programs(1) - 1)
        def _():
            o_ref[...] = acc_ref[...]

    @pl.when((i >= nv_ref[0]) & (j == pl.num_programs(1) - 1))
    def _():
        o_ref[...] = jnp.zeros(o_ref.shape, o_ref.dtype)


def moe_experts(xs, tile_expert, n_valid, wg, wu, wd):
    n_slots, d = xs.shape
    tm = MOE_TILE
    tf = MOE_FCHUNK
    f = wg.shape[-1]
    nf = f // tf

    def xrow(i, j, te, nv):
        return (jnp.minimum(i, nv[0] - 1), 0)

    def fcol(i, j, nv):
        return jnp.where(i < nv[0], j, nf - 1)

    grid_spec = pltpu.PrefetchScalarGridSpec(
        num_scalar_prefetch=2, grid=(n_slots // tm, nf),
        in_specs=[pl.BlockSpec((tm, d), xrow),
                  pl.BlockSpec((1, d, tf), lambda i, j, te, nv: (te[i], 0, fcol(i, j, nv))),
                  pl.BlockSpec((1, d, tf), lambda i, j, te, nv: (te[i], 0, fcol(i, j, nv))),
                  pl.BlockSpec((1, tf, d), lambda i, j, te, nv: (te[i], fcol(i, j, nv), 0))],
        out_specs=pl.BlockSpec((tm, d), lambda i, j, te, nv: (i, 0)),
        scratch_shapes=[pltpu.VMEM((tm, d), F32)])
    return pl.pallas_call(
        _experts_kernel, grid_spec=grid_spec,
        out_shape=jax.ShapeDtypeStruct((n_slots, d), F32),
        compiler_params=_cp(("arbitrary", "arbitrary"), 56), name="moe_experts",
    )(tile_expert, n_valid, xs, wg, wu, wd)


def _combine_kernel(pos_ref, x_ref, info_ref, mod_ref, fg_ref, ys_ref, o_ref, y1_ref, y2_ref, sem):
    tm = x_ref.shape[1]

    def row_copy(r, k, dst):
        return pltpu.make_async_copy(ys_ref.at[pl.ds(pos_ref[0, 0, 2 * r + k], 1)], dst.at[pl.ds(r, 1)], sem)

    def issue(r, c):
        row_copy(r, 0, y1_ref).start()
        row_copy(r, 1, y2_ref).start()
        return c

    def drain(r, c):
        row_copy(r, 0, y1_ref).wait()
        row_copy(r, 1, y2_ref).wait()
        return c

    lax.fori_loop(0, tm, issue, 0)
    lax.fori_loop(0, tm, drain, 0)
    info = info_ref[0]
    y = info[:, 2:3] * y1_ref[...] + info[:, 3:4] * y2_ref[...]
    xn = x_ref[0] + mod_ref[0][5:6] * y
    o_ref[0] = _rms(xn, fg_ref[...])


def moe_combine(x, info, mod, fg, ys, pos, n_tiles):
    b, s, d = x.shape
    tm = TOKEN_TILE

    def row(bi, j):
        return (bi, j, 0)

    return pl.pallas_call(
        _combine_kernel, grid=(b, n_tiles),
        in_specs=[pl.BlockSpec((1, 1, 2 * tm), lambda bi, j: (bi * n_tiles + j, 0, 0), memory_space=pltpu.SMEM),
                  pl.BlockSpec((1, tm, d), row), pl.BlockSpec((1, tm, LANES), row),
                  pl.BlockSpec((1, 6, d), lambda bi, j: (bi, 0, 0)), _const_spec(fg.shape),
                  pl.BlockSpec(memory_space=pl.ANY)],
        out_specs=pl.BlockSpec((1, tm, d), row),
        out_shape=jax.ShapeDtypeStruct((b, n_tiles * tm, d), F32),
        scratch_shapes=[pltpu.VMEM((tm, d), F32), pltpu.VMEM((tm, d), F32), pltpu.SemaphoreType.DMA(())],
        compiler_params=_cp(("arbitrary", "arbitrary")), name="moe_combine",
    )(pos, x, info, mod, fg, ys)


def moe_layer(x, mod, g2, rw, wg, wu, wd, fg, n_lat):
    b, s, d = x.shape
    n_tiles = n_lat // TOKEN_TILE
    t = b * n_lat
    rw_pad = jnp.zeros((d, LANES), F32).at[:, :N_EXPERTS].set(rw.astype(F32))
    h2, info = moe_route(x, mod, g2, rw_pad, n_tiles)
    info2 = info.reshape(t, LANES)

    e_pair = info2[:, :2].astype(jnp.int32).reshape(2 * t)
    onehot = (e_pair[:, None] == jnp.arange(N_EXPERTS)[None, :]).astype(jnp.int32)
    csum = jnp.cumsum(onehot, axis=0)
    rank = jnp.sum((csum - onehot) * onehot, axis=1)
    counts = csum[-1]
    tiles_e = (counts + MOE_TILE - 1) // MOE_TILE
    tile_end = jnp.cumsum(tiles_e)
    slot_off = (tile_end - tiles_e) * MOE_TILE
    pos = (slot_off[e_pair] + rank).astype(jnp.int32).reshape(t // TOKEN_TILE, 1, 2 * TOKEN_TILE)
    n_tiles_max = (2 * t) // MOE_TILE + N_EXPERTS
    n_valid = tile_end[-1:].astype(jnp.int32)
    tile_ids = jnp.minimum(jnp.arange(n_tiles_max), n_valid[0] - 1)
    tile_expert = jnp.sum((tile_ids[:, None] >= tile_end[None, :]).astype(jnp.int32), axis=1).astype(jnp.int32)

    xs = moe_dispatch(h2.reshape(t, d), pos, n_tiles_max * MOE_TILE)
    ys = moe_experts(xs, tile_expert, n_valid, wg, wu, wd)
    return moe_combine(x, info, mod, fg, ys, pos, n_tiles)


def _rope_tables(n_lat, n_ctx):
    t = jnp.arange(n_lat)
    pos = jnp.stack([t // GRID_W, t % GRID_W], axis=-1).astype(F32)
    half = HEAD_DIM // 2
    inv = 1.0 / (ROPE_BASE ** (jnp.arange(0, half, 2, dtype=F32) / half))
    ang = pos[:, :, None] * inv
    ang = jnp.concatenate([ang, ang], axis=-1).reshape(n_lat, HEAD_DIM)
    cos = jnp.concatenate([jnp.cos(ang), jnp.ones((n_ctx, HEAD_DIM), F32)], axis=0)
    sin = jnp.concatenate([jnp.sin(ang), jnp.zeros((n_ctx, HEAD_DIM), F32)], axis=0)
    return jnp.tile(cos, (1, 2)), jnp.tile(sin, (1, 2))


def _slot_offsets(kind):
    if kind == "gqa":
        return [(h // GQA_GROUP) * HEAD_DIM for h in range(GQA_HEADS)]
    return [(h % 2) * HEAD_DIM for h in range(NA_HEADS)]


def _pad_cols(w, kind):
    d = w.shape[0]
    out = jnp.zeros((d, GQA_HEADS, LANES), w.dtype)
    for h, off in enumerate(_slot_offsets(kind)):
        out = out.at[:, h, off:off + HEAD_DIM].set(w[:, h * HEAD_DIM:(h + 1) * HEAD_DIM])
    return out.reshape(d, GQA_HEADS * LANES)


def _pad_rows(w, kind):
    return _pad_cols(w.T, kind).T


def kernel(x, c, ctx, c_ctx, w_mod, b_mod, norm1_g, w_in, ssm_a_re, ssm_a_im, ssm_b_re, ssm_b_im, ssm_c_re,
           ssm_c_im, ssm_log_dt, ssm_d, glu_w, glu_b, q_norm_g, k_norm_g, na_rpb, w_branch_ssm, w_branch_gqa,
           w_branch_na, w_out, norm2_g, ffn_w_gate, ffn_w_up, ffn_w_down, router_w, moe_w_gate, moe_w_up,
           moe_w_down, final_norm_g):
    b, n_lat, d = x.shape
    n_ctx = ctx.shape[1]
    s = n_lat + n_ctx
    depth = w_mod.shape[0]
    assert d == D_MODEL and n_lat % (NA_QROWS * GRID_W) == 0 and n_ctx == TOKEN_TILE and n_lat % 512 == 0
    assert depth == 2 and n_lat // GRID_W >= NA_UNION
    n_lat_tiles = n_lat // TOKEN_TILE
    n_all_tiles = s // TOKEN_TILE

    n_rows = -(-(b + 1) // 8) * 8
    cv = jnp.zeros((n_rows, d), F32).at[:b].set(c).at[b].set(c_ctx)
    mods = adaln(cv, w_mod, b_mod)[:, :b + 1].reshape(depth, b + 1, 6, d)

    cos, sin = _rope_tables(n_lat, n_ctx)
    table_shape_rows = n_lat // GRID_W
    xall = jnp.concatenate([x, ctx], axis=1)

    out = None
    for i in range(depth):
        last = i == depth - 1
        wi = w_in[i]
        c0 = SSM_WIDTH
        wu = wi[:, :c0].astype(BF)
        wkv = wi[:, c0:KV_COLS].astype(BF)
        wqg = _pad_cols(wi[:, KV_COLS:KV_COLS + GQA_Q_W], "gqa").astype(BF)
        wqn = _pad_cols(wi[:, KV_COLS + GQA_Q_W:KV_COLS + GQA_Q_W + NA_W], "na").astype(BF)
        wgt = wi[:, KV_COLS + GQA_Q_W + NA_W:].astype(BF)
        qgain = _pad_cols(jnp.tile(q_norm_g[i].astype(F32), GQA_HEADS)[None, :], "gqa")
        kgain = jnp.tile(k_norm_g[i].astype(F32), GQA_KV_HEADS)[None, :]

        uq, kg, vg, kn, vn, qg, qn, gt = inproj(xall, mods[i], norm1_g[i][None, :], cos, sin, wu, wkv, wqg, wqn,
                                                wgt, qgain, kgain, n_lat_tiles)

        mats = s5_matrices(ssm_a_re[i], ssm_a_im[i], ssm_b_re[i], ssm_b_im[i], ssm_c_re[i], ssm_c_im[i],
                           ssm_log_dt[i], ssm_d[i])
        y4 = s5_branch(uq, mats, n_lat // SSM_CHUNK)

        n_q_tiles = n_lat_tiles if last else n_all_tiles
        og = gqa_attention(qg, kg, vg, n_lat, n_q_tiles)
        on = na_attention(qn, kn, vn, na_table(na_rpb[i], table_shape_rows, n_ctx), n_lat, n_q_tiles)

        xall = merge(xall, y4, og, on, gt, mods[i], glu_w[i].astype(BF), glu_b[i][None, :].astype(F32),
                     w_branch_ssm[i].astype(BF), _pad_rows(w_branch_gqa[i], "gqa").astype(BF),
                     _pad_rows(w_branch_na[i], "na").astype(BF), w_out[i].astype(BF), n_q_tiles, n_lat_tiles)

        j = i // 2
        if i % 2 == 0:
            xall = ffn(xall, mods[i], norm2_g[i][None, :], ffn_w_gate[j].astype(BF), ffn_w_up[j].astype(BF),
                       ffn_w_down[j].astype(BF), n_lat_tiles)
        else:
            out = moe_layer(xall, mods[i], norm2_g[i][None, :], router_w[j], moe_w_gate[j].astype(BF),
                            moe_w_up[j].astype(BF), moe_w_down[j].astype(BF), final_norm_g[None, :], n_lat)
    return out
```

```python
import functools
import math

import jax
import jax.numpy as jnp
from jax import lax
from jax.experimental import pallas as pl
from jax.experimental.pallas import tpu as pltpu

D_MODEL = 1024
GRID_W = 64
SSM_WIDTH = 512
SSM_GROUP = 16
SSM_GROUPS = SSM_WIDTH // SSM_GROUP
SSM_STATE = 64
HEAD_DIM = 64
GQA_HEADS = 8
GQA_KV_HEADS = 2
GQA_GROUP = GQA_HEADS // GQA_KV_HEADS
ROPE_BASE = 10000.0
NA_HEADS = 8
NA_WIN_R = 8
NA_WIN_C = 16
ATTN_SCALE = HEAD_DIM ** -0.5
GQA_Q_W = GQA_HEADS * HEAD_DIM
GQA_KV_W = GQA_KV_HEADS * HEAD_DIM
NA_W = NA_HEADS * HEAD_DIM
KV_COLS = SSM_WIDTH + 2 * GQA_KV_W + 2 * NA_W
N_EXPERTS = 8
NORM_EPS = 1e-6

LANES = 128
TOKEN_TILE = 256
SSM_CHUNK = 16
SSM_QT = SSM_WIDTH // LANES
SSM_GPT = LANES // SSM_GROUP
SSM_ST = SSM_GPT * SSM_STATE
NA_QROWS = 4
NA_UNION = 12
MOE_TILE = 512
MOE_FCHUNK = 896
NEG = -1e30

BF = jnp.bfloat16
F32 = jnp.float32


def _cp(sem, vmem_mb=48):
    return pltpu.CompilerParams(dimension_semantics=sem, vmem_limit_bytes=vmem_mb * 1024 * 1024)


def _const_spec(shape):
    nd = len(shape)
    return pl.BlockSpec(shape, lambda *_: (0,) * nd, pipeline_mode=pl.Buffered(1))


def _dot(a, b):
    return jnp.dot(a, b, preferred_element_type=F32)


def _dot_nt(a, b):
    return lax.dot_general(a, b, (((1,), (1,)), ((), ())), preferred_element_type=F32)


def _rms(x, g):
    return x * lax.rsqrt(jnp.mean(x * x, axis=-1, keepdims=True) + NORM_EPS) * g


def _silu(x):
    return x * jax.nn.sigmoid(x)


def _gelu_tanh(x):
    return 0.5 * x * (1.0 + jnp.tanh(math.sqrt(2.0 / math.pi) * (x + 0.044715 * x * x * x)))


def _adaln_kernel(cv_ref, w_ref, b_ref, o_ref):
    a = _silu(cv_ref[...])
    w = w_ref[0]
    a_hi = a.astype(BF)
    a_lo = (a - a_hi.astype(F32)).astype(BF)
    w_hi = w.astype(BF)
    w_lo = (w - w_hi.astype(F32)).astype(BF)
    o_ref[0] = _dot(a_hi, w_hi) + _dot(a_hi, w_lo) + _dot(a_lo, w_hi) + b_ref[0]


def adaln(cv, w_mod, b_mod):
    depth, d, n = w_mod.shape
    r = cv.shape[0]
    tn = 512
    return pl.pallas_call(
        _adaln_kernel,
        grid=(depth, n // tn),
        in_specs=[pl.BlockSpec((r, d), lambda i, j: (0, 0)),
                  pl.BlockSpec((1, d, tn), lambda i, j: (i, 0, j)),
                  pl.BlockSpec((1, 1, tn), lambda i, j: (i, 0, j))],
        out_specs=pl.BlockSpec((1, r, tn), lambda i, j: (i, 0, j)),
        out_shape=jax.ShapeDtypeStruct((depth, r, n), F32),
        compiler_params=_cp(("arbitrary", "arbitrary")),
        name="adaln",
    )(cv, w_mod, b_mod.reshape(depth, 1, n))


def _rope(xn, cos, sin):
    lane = lax.broadcasted_iota(jnp.int32, xn.shape, 1)
    first = (lane % 32) < 16
    rot = jnp.where(first, -pltpu.roll(xn, LANES - 16, 1), pltpu.roll(xn, 16, 1))
    return xn * cos + rot * sin


def _inproj_kernel(x_ref, mod_ref, g_ref, cos_ref, sin_ref, wu_ref, wkv_ref, wqg_ref, wqn_ref, wgt_ref,
                   qgain_ref, kgain_ref,
                   u_ref, kg_ref, vg_ref, kn_ref, vn_ref, qg_ref, qn_ref, gt_ref):
    x = x_ref[0]
    mod = mod_ref[0]
    h = _rms(x, g_ref[...]) * (1.0 + mod[1:2]) + mod[0:1]
    hb = h.astype(BF)
    cos = cos_ref[...]
    sin = sin_ref[...]

    u = _dot(hb, wu_ref[...])
    for q in range(SSM_QT):
        u_ref[q, 0] = u[:, q * LANES:(q + 1) * LANES]

    kv = _dot(hb, wkv_ref[...])
    kg = kv[:, :LANES]
    lane = lax.broadcasted_iota(jnp.int32, kg.shape, 1)
    lo = lane < HEAD_DIM
    sq = kg * kg
    ms_lo = jnp.sum(jnp.where(lo, sq, 0.0), axis=-1, keepdims=True)
    ms_hi = jnp.sum(jnp.where(lo, 0.0, sq), axis=-1, keepdims=True)
    ms = jnp.where(lo, ms_lo, ms_hi) * (1.0 / HEAD_DIM)
    kgn = kg * lax.rsqrt(ms + NORM_EPS) * kgain_ref[...]
    kg_ref[0] = _rope(kgn, cos, sin).astype(BF)
    vg_ref[0] = kv[:, LANES:2 * LANES].astype(BF)
    kn_ref[0] = kv[:, 2 * LANES:2 * LANES + NA_W].astype(BF)
    vn_ref[0] = kv[:, 2 * LANES + NA_W:].astype(BF)

    qg = _dot(hb, wqg_ref[...])
    for s in range(GQA_HEADS):
        qs = qg[:, s * LANES:(s + 1) * LANES]
        ms = jnp.sum(qs * qs, axis=-1, keepdims=True) * (1.0 / HEAD_DIM)
        qsn = qs * lax.rsqrt(ms + NORM_EPS) * qgain_ref[:, s * LANES:(s + 1) * LANES]
        qg_ref[0, :, s * LANES:(s + 1) * LANES] = (_rope(qsn, cos, sin) * ATTN_SCALE).astype(BF)

    qn_ref[0] = (_dot(hb, wqn_ref[...]) * ATTN_SCALE).astype(BF)
    gt_ref[0] = jax.nn.sigmoid(_dot(hb, wgt_ref[...])).astype(BF)


def inproj(xall, mod, g, cos, sin, wu, wkv, wqg, wqn, wgt, qgain, kgain, n_lat_tiles):
    b, s, d = xall.shape
    tm = TOKEN_TILE
    nt = s // tm
    nb = mod.shape[0] - 1

    def row(bi, j):
        return (bi, j, 0)

    def modrow(bi, j):
        return (jnp.where(j < n_lat_tiles, bi, nb), 0, 0)

    outs = [
        jax.ShapeDtypeStruct((SSM_QT, b, s, LANES), F32),
        jax.ShapeDtypeStruct((b, s, LANES), BF),
        jax.ShapeDtypeStruct((b, s, LANES), BF),
        jax.ShapeDtypeStruct((b, s, NA_W), BF),
        jax.ShapeDtypeStruct((b, s, NA_W), BF),
        jax.ShapeDtypeStruct((b, s, GQA_HEADS * LANES), BF),
        jax.ShapeDtypeStruct((b, s, NA_HEADS * LANES), BF),
        jax.ShapeDtypeStruct((b, s, 3 * d), BF),
    ]
    out_specs = [
        pl.BlockSpec((SSM_QT, 1, tm, LANES), lambda bi, j: (0, bi, j, 0)),
        pl.BlockSpec((1, tm, LANES), row),
        pl.BlockSpec((1, tm, LANES), row),
        pl.BlockSpec((1, tm, NA_W), row),
        pl.BlockSpec((1, tm, NA_W), row),
        pl.BlockSpec((1, tm, GQA_HEADS * LANES), row),
        pl.BlockSpec((1, tm, NA_HEADS * LANES), row),
        pl.BlockSpec((1, tm, 3 * d), row),
    ]
    in_specs = [
        pl.BlockSpec((1, tm, d), row),
        pl.BlockSpec((1, 6, d), modrow),
        _const_spec(g.shape),
        pl.BlockSpec((tm, LANES), lambda bi, j: (j, 0)),
        pl.BlockSpec((tm, LANES), lambda bi, j: (j, 0)),
        _const_spec(wu.shape), _const_spec(wkv.shape), _const_spec(wqg.shape),
        _const_spec(wqn.shape), _const_spec(wgt.shape), _const_spec(qgain.shape), _const_spec(kgain.shape),
    ]
    return pl.pallas_call(
        _inproj_kernel, grid=(b, nt), in_specs=in_specs, out_specs=out_specs, out_shape=outs,
        compiler_params=_cp(("arbitrary", "arbitrary"), 56), name="inproj",
    )(xall, mod, g, cos, sin, wu, wkv, wqg, wqn, wgt, qgain, kgain)


def _chunk_rows(u_ref):
    nk = u_ref.shape[2] // SSM_CHUNK
    cols = [u_ref[0, 0, pl.ds(j, nk, stride=SSM_CHUNK), :] for j in range(SSM_CHUNK)]
    return jnp.concatenate(cols, axis=1).astype(BF)


def _ssm_in_kernel(u_ref, m_ref, zf_ref, zr_ref):
    z = _dot(_chunk_rows(u_ref), m_ref[0])
    half = z.shape[1] // 2
    zf_ref[0] = z[:, :half]
    zr_ref[0] = z[:, half:]


def ssm_in(uq, m1):
    qt, b, s, _ = uq.shape
    nk = s // SSM_CHUNK
    k = SSM_CHUNK * LANES
    w = 2 * SSM_ST
    zspec = pl.BlockSpec((1, nk, w), lambda q, bi: (bi, 0, q))
    return pl.pallas_call(
        _ssm_in_kernel, grid=(qt, b),
        in_specs=[pl.BlockSpec((1, 1, s, LANES), lambda q, bi: (q, bi, 0, 0)),
                  pl.BlockSpec((1, k, 2 * w), lambda q, bi: (q, 0, 0))],
        out_specs=[zspec, zspec],
        out_shape=[jax.ShapeDtypeStruct((b, nk, qt * w), F32)] * 2,
        compiler_params=_cp(("arbitrary", "arbitrary")), name="ssm_in",
    )(uq, m1)


def _ssm_scan_kernel(zf_ref, zr_ref, af_ref, ar_ref, sf_ref, sr_ref, *, n_lat_chunks):
    nk = zf_ref.shape[1]
    st = SSM_ST
    af_re, af_im = af_ref[:, :st], af_ref[:, st:]
    ar_re, ar_im = ar_ref[:, :st], ar_ref[:, st:]

    def step(i, carry):
        fre, fim, rre, rim = carry
        kf = lax.rem(i + n_lat_chunks, nk)
        kr = nk - 1 - i
        sf_ref[0, pl.ds(kf, 1), :] = jnp.concatenate([fre, fim], axis=1)
        sr_ref[0, pl.ds(kr, 1), :] = jnp.concatenate([rre, rim], axis=1)
        zf = zf_ref[0, pl.ds(kf, 1), :]
        zr = zr_ref[0, pl.ds(kr, 1), :]
        nfre = af_re * fre - af_im * fim + zf[:, :st]
        nfim = af_re * fim + af_im * fre + zf[:, st:]
        nrre = ar_re * rre - ar_im * rim + zr[:, :st]
        nrim = ar_re * rim + ar_im * rre + zr[:, st:]
        return nfre, nfim, nrre, nrim

    z0 = jnp.zeros((1, st), F32)
    lax.fori_loop(0, nk, step, (z0, z0, z0, z0))


def ssm_scan(zf, zr, af, ar, n_lat_chunks):
    b, nk, n = zf.shape
    w = 2 * SSM_ST
    spec = pl.BlockSpec((1, nk, w), lambda bi, q: (bi, 0, q))
    aspec = pl.BlockSpec((1, w), lambda bi, q: (0, q))
    return pl.pallas_call(
        functools.partial(_ssm_scan_kernel, n_lat_chunks=n_lat_chunks),
        grid=(b, n // w), in_specs=[spec, spec, aspec, aspec], out_specs=[spec, spec],
        out_shape=[jax.ShapeDtypeStruct((b, nk, n), F32)] * 2,
        compiler_params=_cp(("arbitrary", "arbitrary")), name="ssm_scan",
    )(zf, zr, af, ar)


def _ssm_out_kernel(u_ref, sf_ref, sr_ref, mu_ref, mf_ref, mr_ref, y_ref):
    nk = sf_ref.shape[1]
    y = _dot(_chunk_rows(u_ref), mu_ref[0])
    y += _dot(sf_ref[0].astype(BF), mf_ref[0])
    y += _dot(sr_ref[0].astype(BF), mr_ref[0])
    for j in range(SSM_CHUNK):
        y_ref[0, 0, pl.ds(j, nk, stride=SSM_CHUNK), :] = y[:, j * LANES:(j + 1) * LANES]


def ssm_out(uq, sf, sr, m3u, m3f, m3r):
    qt, b, s, _ = uq.shape
    nk = s // SSM_CHUNK
    k = SSM_CHUNK * LANES
    w = 2 * SSM_ST
    one = pl.Buffered(1)
    uspec = pl.BlockSpec((1, 1, s, LANES), lambda q, bi: (q, bi, 0, 0))
    sspec = pl.BlockSpec((1, nk, w), lambda q, bi: (bi, 0, q))
    return pl.pallas_call(
        _ssm_out_kernel, grid=(qt, b),
        in_specs=[uspec, sspec, sspec,
                  pl.BlockSpec((1, k, k), lambda q, bi: (q, 0, 0), pipeline_mode=one),
                  pl.BlockSpec((1, w, k), lambda q, bi: (q, 0, 0), pipeline_mode=one),
                  pl.BlockSpec((1, w, k), lambda q, bi: (q, 0, 0), pipeline_mode=one)],
        out_specs=uspec,
        out_shape=jax.ShapeDtypeStruct((qt, b, s, LANES), F32),
        compiler_params=_cp(("arbitrary", "arbitrary")), name="ssm_out",
    )(uq, sf, sr, m3u, m3f, m3r)


def s5_matrices(a_re, a_im, b_re, b_im, c_re, c_im, log_dt, d_skip):
    tc, g, p, hh = SSM_CHUNK, SSM_GROUPS, SSM_STATE, SSM_GROUP
    lam = lax.complex(a_re.astype(F32), a_im.astype(F32))
    dt = jnp.exp(log_dt.astype(F32))[..., None]
    ldt = lam * dt
    a_bar = jnp.exp(ldt)
    b_bar = ((a_bar - 1.0) / lam)[..., None] * lax.complex(b_re.astype(F32), b_im.astype(F32))
    cm = lax.complex(c_re.astype(F32), c_im.astype(F32))
    pw = jnp.exp(ldt[None] * jnp.arange(tc + 1, dtype=F32)[:, None, None, None])
    eye = jnp.eye(SSM_GPT, dtype=F32)

    kk = jnp.real(jnp.einsum('dghp,ldgp,dgpk->dlghk', cm, pw[:tc], b_bar))
    jj = jnp.arange(tc)[:, None]
    ii = jnp.arange(tc)[None, :]
    kf = jnp.where((ii >= jj)[..., None, None, None], kk[0][jnp.clip(ii - jj, 0, tc - 1)], 0.0)
    kr = jnp.where((jj >= ii)[..., None, None, None], kk[1][jnp.clip(jj - ii, 0, tc - 1)], 0.0)
    dd = (jnp.eye(tc, dtype=F32)[:, :, None, None, None] * jnp.eye(hh, dtype=F32)[None, None, None]
          * d_skip.astype(F32).reshape(g, hh)[None, None, :, :, None])
    tt = (kf + kr + dd).reshape(tc, tc, SSM_QT, SSM_GPT, hh, hh)
    m3u = jnp.einsum('jiqghk,gm->qjgkimh', tt, eye).reshape(SSM_QT, tc * LANES, tc * LANES)

    inf = pw[tc - 1 - jnp.arange(tc), 0][..., None] * b_bar[0][None]
    inr = pw[jnp.arange(tc), 1][..., None] * b_bar[1][None]
    m1 = jnp.stack([jnp.real(inf), jnp.imag(inf), jnp.real(inr), jnp.imag(inr)], axis=0)
    m1 = m1.reshape(2, 2, tc, SSM_QT, SSM_GPT, p, hh)
    m1 = jnp.einsum('drjqgpk,gm->qjgkdrmp', m1, eye).reshape(SSM_QT, tc * LANES, 4 * SSM_ST)

    def out_mat(d, powers):
        ca = cm[d][None] * pw[powers, d][:, :, None, :]
        mm = jnp.stack([jnp.real(ca), -jnp.imag(ca)], axis=0)
        mm = mm.reshape(2, tc, SSM_QT, SSM_GPT, hh, p)
        return jnp.einsum('riqghp,gm->qrgpimh', mm, eye).reshape(SSM_QT, 2 * SSM_ST, tc * LANES)

    m3f = out_mat(0, jnp.arange(tc) + 1)
    m3r = out_mat(1, tc - jnp.arange(tc))

    def decay(d):
        a16 = pw[tc, d].reshape(SSM_QT, SSM_ST)
        return jnp.concatenate([jnp.real(a16), jnp.imag(a16)], axis=1).reshape(1, SSM_QT * 2 * SSM_ST)

    return m1.astype(BF), m3u.astype(BF), m3f.astype(BF), m3r.astype(BF), decay(0), decay(1)


def s5_branch(uq4, mats, n_lat_chunks):
    m1, m3u, m3f, m3r, a16f, a16r = mats
    zf, zr = ssm_in(uq4, m1)
    sf, sr = ssm_scan(zf, zr, a16f, a16r, n_lat_chunks)
    return ssm_out(uq4, sf, sr, m3u, m3f, m3r)


def _gqa_kernel(q_ref, k_ref, vt_ref, vtc_ref, o_ref, q4_ref, m_ref, l_ref, acc_ref, *, n_lat, n_ctx, tk,
                n_lat_tiles):
    tq = q_ref.shape[1]
    qi = pl.program_id(2)
    for s in range(GQA_GROUP):
        q4_ref[s * tq:(s + 1) * tq, :] = q_ref[0, :, s * LANES:(s + 1) * LANES]
    m_ref[...] = jnp.full(m_ref.shape, NEG, F32)
    l_ref[...] = jnp.zeros(l_ref.shape, F32)
    acc_ref[...] = jnp.zeros(acc_ref.shape, F32)

    def update(kc, vtc):
        st = _dot_nt(kc, q4_ref[...])
        m_old = m_ref[...]
        m_new = jnp.maximum(m_old, jnp.max(st, axis=0, keepdims=True))
        alpha = jnp.exp(m_old - m_new)
        p = jnp.exp(st - m_new)
        l_ref[...] = alpha * l_ref[...] + jnp.sum(p, axis=0, keepdims=True)
        acc_ref[...] = alpha * acc_ref[...] + _dot(vtc, p.astype(BF))
        m_ref[...] = m_new

    def chunk(c, carry):
        off = pl.multiple_of(c * tk, tk)
        update(k_ref[0, pl.ds(off, tk), :], vt_ref[0, c])
        return carry

    lax.fori_loop(0, jnp.where(qi < n_lat_tiles, n_lat // tk, 0), chunk, 0)
    update(k_ref[0, n_lat:n_lat + n_ctx, :], vtc_ref[0])

    o = (acc_ref[...] / l_ref[...]).T
    for s in range(GQA_GROUP):
        o_ref[0, :, s * LANES:(s + 1) * LANES] = o[s * tq:(s + 1) * tq].astype(o_ref.dtype)


def gqa_attention(qg, kg, vg, n_lat, n_q_tiles):
    b, s, _ = qg.shape
    tq = TOKEN_TILE
    tk = 512
    gw = GQA_GROUP * LANES
    n_ctx = s - n_lat
    nc = n_lat // tk
    vt = jnp.swapaxes(vg[:, :n_lat].reshape(b, nc, tk, LANES), 2, 3)
    vtc = jnp.swapaxes(vg[:, n_lat:], 1, 2)
    kern = functools.partial(_gqa_kernel, n_lat=n_lat, n_ctx=n_ctx, tk=tk, n_lat_tiles=n_lat // tq)
    return pl.pallas_call(
        kern, grid=(b, GQA_KV_HEADS, n_q_tiles),
        in_specs=[pl.BlockSpec((1, tq, gw), lambda bi, g, i: (bi, i, g)),
                  pl.BlockSpec((1, s, LANES), lambda bi, g, i: (bi, 0, 0)),
                  pl.BlockSpec((1, nc, LANES, tk), lambda bi, g, i: (bi, 0, 0, 0)),
                  pl.BlockSpec((1, LANES, n_ctx), lambda bi, g, i: (bi, 0, 0))],
        out_specs=pl.BlockSpec((1, tq, gw), lambda bi, g, i: (bi, i, g)),
        out_shape=jax.ShapeDtypeStruct((b, n_q_tiles * tq, GQA_HEADS * LANES), BF),
        scratch_shapes=[pltpu.VMEM((GQA_GROUP * tq, LANES), BF),
                        pltpu.VMEM((1, GQA_GROUP * tq), F32),
                        pltpu.VMEM((1, GQA_GROUP * tq), F32),
                        pltpu.VMEM((LANES, GQA_GROUP * tq), F32)],
        compiler_params=_cp(("arbitrary", "arbitrary", "arbitrary")), name="gqa",
    )(qg, kg, vt, vtc)


def _na_kernel(q_ref, k_ref, v_ref, tab_ref, o_ref, *, n_lat, n_ctx, rows):
    rb = pl.program_id(1)
    ws = jnp.clip(rb * NA_QROWS - NA_WIN_R // 2, 0, rows - NA_UNION)
    off = pl.multiple_of(ws * GRID_W, GRID_W)
    nwin = NA_UNION * GRID_W
    for pair in range(NA_HEADS // 2):
        ls = slice(pair * LANES, (pair + 1) * LANES)
        kcat = jnp.concatenate([k_ref[0, pl.ds(off, nwin), ls], k_ref[0, n_lat:n_lat + n_ctx, ls]], axis=0)
        vcat = jnp.concatenate([v_ref[0, pl.ds(off, nwin), ls], v_ref[0, n_lat:n_lat + n_ctx, ls]], axis=0)
        for half in range(2):
            h = pair * 2 + half
            hs = slice(h * LANES, (h + 1) * LANES)
            s = _dot_nt(q_ref[0, :, hs], kcat) + tab_ref[0, h].astype(F32)
            m = jnp.max(s, axis=-1, keepdims=True)
            p = jnp.exp(s - m)
            l = jnp.sum(p, axis=-1, keepdims=True)
            o_ref[0, :, hs] = (_dot(p.astype(BF), vcat) / l).astype(o_ref.dtype)


def na_attention(qn, kn, vn, table, n_lat, n_q_tiles):
    b, s, _ = qn.shape
    tq = NA_QROWS * GRID_W
    rows = n_lat // GRID_W
    n_lat_tiles = n_lat // tq
    tw = table.shape[-1]

    def tab_idx(bi, rb):
        case = jnp.where(rb == 0, 0, jnp.where(rb < n_lat_tiles - 1, 1, jnp.where(rb == n_lat_tiles - 1, 2, 3)))
        return (case, 0, 0, 0)

    kern = functools.partial(_na_kernel, n_lat=n_lat, n_ctx=s - n_lat, rows=rows)
    return pl.pallas_call(
        kern, grid=(b, n_q_tiles),
        in_specs=[pl.BlockSpec((1, tq, NA_HEADS * LANES), lambda bi, rb: (bi, rb, 0)),
                  pl.BlockSpec((1, s, NA_W), lambda bi, rb: (bi, 0, 0)),
                  pl.BlockSpec((1, s, NA_W), lambda bi, rb: (bi, 0, 0)),
                  pl.BlockSpec((1, NA_HEADS, tq, tw), tab_idx)],
        out_specs=pl.BlockSpec((1, tq, NA_HEADS * LANES), lambda bi, rb: (bi, rb, 0)),
        out_shape=jax.ShapeDtypeStruct((b, n_q_tiles * tq, NA_HEADS * LANES), BF),
        compiler_params=_cp(("arbitrary", "arbitrary"), 56), name="na",
    )(qn, kn, vn, table)


def na_table(rpb, rows, n_ctx):
    tq = NA_QROWS * GRID_W
    qr = jnp.arange(NA_QROWS)[:, None, None, None]
    qc = jnp.arange(GRID_W)[None, :, None, None]
    kr = jnp.arange(NA_UNION)[None, None, :, None]
    kc = jnp.arange(GRID_W)[None, None, None, :]
    cs = jnp.clip(qc - NA_WIN_C // 2, 0, GRID_W - NA_WIN_C)
    col_ok = (kc >= cs) & (kc < cs + NA_WIN_C)
    col_idx = (kc - qc + NA_WIN_C - 1)[:, :, 0, :]
    col_hot = (col_idx[0, :, :, None] == jnp.arange(2 * NA_WIN_C - 1)).astype(F32)
    n_blocks = rows // NA_QROWS
    hi = lax.Precision.HIGHEST
    tabs = []
    for blk in (0, 1, n_blocks - 1):
        r = blk * NA_QROWS + qr
        ws = min(max(blk * NA_QROWS - NA_WIN_R // 2, 0), rows - NA_UNION)
        rs = jnp.clip(r - NA_WIN_R // 2, 0, rows - NA_WIN_R)
        ka = ws + kr
        ok = (ka >= rs) & (ka < rs + NA_WIN_R) & col_ok
        row_idx = (ka - r + NA_WIN_R - 1)[:, 0, :, 0]
        row_hot = (row_idx[:, :, None] == jnp.arange(2 * NA_WIN_R - 1)).astype(F32)
        by_row = jnp.einsum('hab,rka->hrkb', rpb.astype(F32), row_hot, precision=hi)
        bias = jnp.einsum('hrkb,qcb->hrqkc', by_row, col_hot, precision=hi)
        tabs.append(jnp.where(ok[None], bias, NEG).reshape(NA_HEADS, tq, NA_UNION * GRID_W))
    tabs.append(jnp.full((NA_HEADS, tq, NA_UNION * GRID_W), NEG, F32))
    win = jnp.stack(tabs, axis=0)
    return jnp.concatenate([win, jnp.zeros(win.shape[:3] + (n_ctx,), F32)], axis=-1).astype(BF)


def _merge_kernel(x_ref, y_ref, og_ref, on_ref, gt_ref, mod_ref, gw_ref, gb_ref, ws_ref, wg_ref, wn_ref, wo_ref,
                  o_ref):
    d = x_ref.shape[-1]
    y = jnp.concatenate([y_ref[q, 0].astype(F32) for q in range(SSM_QT)], axis=1)
    gy = _gelu_tanh(y)
    ys = gy * jax.nn.sigmoid(_dot(gy.astype(BF), gw_ref[...]) + gb_ref[...])
    gt = gt_ref[0]
    m = gt[:, :d].astype(F32) * _dot(ys.astype(BF), ws_ref[...])
    m += gt[:, d:2 * d].astype(F32) * _dot(og_ref[0], wg_ref[...])
    m += gt[:, 2 * d:].astype(F32) * _dot(on_ref[0], wn_ref[...])
    o_ref[0] = x_ref[0] + mod_ref[0][2:3] * _dot(m.astype(BF), wo_ref[...])


def merge(xall, y4, og, on, gt, mod, gw, gb, ws, wg, wn, wo, n_tiles, n_lat_tiles):
    b, s, d = xall.shape
    tm = TOKEN_TILE
    nb = mod.shape[0] - 1

    def row(bi, j):
        return (bi, j, 0)

    return pl.pallas_call(
        _merge_kernel, grid=(b, n_tiles),
        in_specs=[pl.BlockSpec((1, tm, d), row),
                  pl.BlockSpec((SSM_QT, 1, tm, LANES), lambda bi, j: (0, bi, j, 0)),
                  pl.BlockSpec((1, tm, GQA_HEADS * LANES), row),
                  pl.BlockSpec((1, tm, NA_HEADS * LANES), row),
                  pl.BlockSpec((1, tm, 3 * d), row),
                  pl.BlockSpec((1, 6, d), lambda bi, j: (jnp.where(j < n_lat_tiles, bi, nb), 0, 0)),
                  _const_spec(gw.shape), _const_spec(gb.shape), _const_spec(ws.shape),
                  _const_spec(wg.shape), _const_spec(wn.shape), _const_spec(wo.shape)],
        out_specs=pl.BlockSpec((1, tm, d), row),
        out_shape=jax.ShapeDtypeStruct((b, n_tiles * tm, d), F32),
        compiler_params=_cp(("arbitrary", "arbitrary")), name="merge",
    )(xall, y4, og, on, gt, mod, gw, gb, ws, wg, wn, wo)


def _ffn_kernel(x_ref, mod_ref, g_ref, wg_ref, wu_ref, wd_ref, o_ref):
    x = x_ref[0]
    mod = mod_ref[0]
    hb = (_rms(x, g_ref[...]) * (1.0 + mod[4:5]) + mod[3:4]).astype(BF)
    a = _silu(_dot(hb, wg_ref[...])) * _dot(hb, wu_ref[...])
    o_ref[0] = x + mod[5:6] * _dot(a.astype(BF), wd_ref[...])


def ffn(xall, mod, g, wg, wu, wd, n_lat_tiles):
    b, s, d = xall.shape
    tm = TOKEN_TILE
    nb = mod.shape[0] - 1

    def row(bi, j):
        return (bi, j, 0)

    return pl.pallas_call(
        _ffn_kernel, grid=(b, s // tm),
        in_specs=[pl.BlockSpec((1, tm, d), row),
                  pl.BlockSpec((1, 6, d), lambda bi, j: (jnp.where(j < n_lat_tiles, bi, nb), 0, 0)),
                  _const_spec(g.shape), _const_spec(wg.shape), _const_spec(wu.shape), _const_spec(wd.shape)],
        out_specs=pl.BlockSpec((1, tm, d), row),
        out_shape=jax.ShapeDtypeStruct((b, s, d), F32),
        compiler_params=_cp(("arbitrary", "arbitrary"), 56), name="ffn",
    )(xall, mod, g, wg, wu, wd)


def _route_kernel(x_ref, mod_ref, g_ref, rw_ref, h_ref, info_ref):
    x = x_ref[0]
    mod = mod_ref[0]
    h = _rms(x, g_ref[...]) * (1.0 + mod[4:5]) + mod[3:4]
    h_ref[0] = h
    logits = jnp.dot(h, rw_ref[...], preferred_element_type=F32, precision=lax.Precision.HIGHEST)
    lane = lax.broadcasted_iota(jnp.int32, logits.shape, 1)
    lanef = lane.astype(F32)
    logits = jnp.where(lane < N_EXPERTS, logits, -jnp.inf)
    m1 = jnp.max(logits, axis=-1, keepdims=True)
    i1 = jnp.min(jnp.where(logits == m1, lanef, float(LANES)), axis=-1, keepdims=True)
    rest = jnp.where(lanef == i1, -jnp.inf, logits)
    m2 = jnp.max(rest, axis=-1, keepdims=True)
    i2 = jnp.min(jnp.where(rest == m2, lanef, float(LANES)), axis=-1, keepdims=True)
    e2 = jnp.exp(m2 - m1)
    w1 = 1.0 / (1.0 + e2)
    w2 = e2 / (1.0 + e2)
    info_ref[0] = jnp.where(lane == 0, i1, jnp.where(lane == 1, i2, jnp.where(lane == 2, w1,
                            jnp.where(lane == 3, w2, 0.0))))


def moe_route(xall, mod, g, rw, n_tiles):
    b, s, d = xall.shape
    tm = TOKEN_TILE

    def row(bi, j):
        return (bi, j, 0)

    return pl.pallas_call(
        _route_kernel, grid=(b, n_tiles),
        in_specs=[pl.BlockSpec((1, tm, d), row), pl.BlockSpec((1, 6, d), lambda bi, j: (bi, 0, 0)),
                  _const_spec(g.shape), _const_spec(rw.shape)],
        out_specs=[pl.BlockSpec((1, tm, d), row), pl.BlockSpec((1, tm, LANES), row)],
        out_shape=[jax.ShapeDtypeStruct((b, n_tiles * tm, d), F32),
                   jax.ShapeDtypeStruct((b, n_tiles * tm, LANES), F32)],
        compiler_params=_cp(("arbitrary", "arbitrary")), name="moe_route",
    )(xall, mod, g, rw)


def _dispatch_kernel(pos_ref, h_ref, xs_in_ref, xs_ref, sem):
    del xs_in_ref
    tm = h_ref.shape[0]

    def row_copy(r, k):
        return pltpu.make_async_copy(h_ref.at[pl.ds(r, 1)], xs_ref.at[pl.ds(pos_ref[0, 0, 2 * r + k], 1)], sem)

    def issue(r, c):
        row_copy(r, 0).start()
        row_copy(r, 1).start()
        return c

    def drain(r, c):
        row_copy(r, 0).wait()
        row_copy(r, 1).wait()
        return c

    lax.fori_loop(0, tm, issue, 0)
    lax.fori_loop(0, tm, drain, 0)


def moe_dispatch(h2, pos, n_slots):
    t, d = h2.shape
    tm = TOKEN_TILE
    zeros = jnp.zeros((n_slots, d), F32)
    return pl.pallas_call(
        _dispatch_kernel, grid=(t // tm,),
        in_specs=[pl.BlockSpec((1, 1, 2 * tm), lambda i: (i, 0, 0), memory_space=pltpu.SMEM),
                  pl.BlockSpec((tm, d), lambda i: (i, 0)),
                  pl.BlockSpec(memory_space=pl.ANY)],
        out_specs=pl.BlockSpec(memory_space=pl.ANY),
        out_shape=jax.ShapeDtypeStruct((n_slots, d), F32),
        scratch_shapes=[pltpu.SemaphoreType.DMA(())],
        input_output_aliases={2: 0},
        compiler_params=_cp(("arbitrary",)), name="moe_dispatch",
    )(pos, h2, zeros)


def _experts_kernel(te_ref, nv_ref, x_ref, wg_ref, wu_ref, wd_ref, o_ref, acc_ref):
    i = pl.program_id(0)
    j = pl.program_id(1)

    @pl.when(i < nv_ref[0])
    def _():
        xb = x_ref[...].astype(BF)
        a = _silu(_dot(xb, wg_ref[0])) * _dot(xb, wu_ref[0])
        part = _dot(a.astype(BF), wd_ref[0])

        @pl.when(j == 0)
        def _():
            acc_ref[...] = part

        @pl.when(j > 0)
        def _():
            acc_ref[...] += part

        @pl.when(j == pl.num_programs(1) - 1)
        def _():
            o_ref[...] = acc_ref[...]

    @pl.when((i >= nv_ref[0]) & (j == pl.num_programs(1) - 1))
    def _():
        o_ref[...] = jnp.zeros(o_ref.shape, o_ref.dtype)


def moe_experts(xs, tile_expert, n_valid, wg, wu, wd):
    n_slots, d = xs.shape
    tm = MOE_TILE
    tf = MOE_FCHUNK
    f = wg.shape[-1]
    nf = f // tf

    def xrow(i, j, te, nv):
        return (jnp.minimum(i, nv[0] - 1), 0)

    def fcol(i, j, nv):
        return jnp.where(i < nv[0], j, nf - 1)

    grid_spec = pltpu.PrefetchScalarGridSpec(
        num_scalar_prefetch=2, grid=(n_slots // tm, nf),
        in_specs=[pl.BlockSpec((tm, d), xrow),
                  pl.BlockSpec((1, d, tf), lambda i, j, te, nv: (te[i], 0, fcol(i, j, nv))),
                  pl.BlockSpec((1, d, tf), lambda i, j, te, nv: (te[i], 0, fcol(i, j, nv))),
                  pl.BlockSpec((1, tf, d), lambda i, j, te, nv: (te[i], fcol(i, j, nv), 0))],
        out_specs=pl.BlockSpec((tm, d), lambda i, j, te, nv: (i, 0)),
        scratch_shapes=[pltpu.VMEM((tm, d), F32)])
    return pl.pallas_call(
        _experts_kernel, grid_spec=grid_spec,
        out_shape=jax.ShapeDtypeStruct((n_slots, d), F32),
        compiler_params=_cp(("arbitrary", "arbitrary"), 56), name="moe_experts",
    )(tile_expert, n_valid, xs, wg, wu, wd)


def _combine_kernel(pos_ref, x_ref, info_ref, mod_ref, fg_ref, ys_ref, o_ref, y1_ref, y2_ref, sem):
    tm = x_ref.shape[1]

    def row_copy(r, k, dst):
        return pltpu.make_async_copy(ys_ref.at[pl.ds(pos_ref[0, 0, 2 * r + k], 1)], dst.at[pl.ds(r, 1)], sem)

    def issue(r, c):
        row_copy(r, 0, y1_ref).start()
        row_copy(r, 1, y2_ref).start()
        return c

    def drain(r, c):
        row_copy(r, 0, y1_ref).wait()
        row_copy(r, 1, y2_ref).wait()
        return c

    lax.fori_loop(0, tm, issue, 0)
    lax.fori_loop(0, tm, drain, 0)
    info = info_ref[0]
    y = info[:, 2:3] * y1_ref[...] + info[:, 3:4] * y2_ref[...]
    xn = x_ref[0] + mod_ref[0][5:6] * y
    o_ref[0] = _rms(xn, fg_ref[...])


def moe_combine(x, info, mod, fg, ys, pos, n_tiles):
    b, s, d = x.shape
    tm = TOKEN_TILE

    def row(bi, j):
        return (bi, j, 0)

    return pl.pallas_call(
        _combine_kernel, grid=(b, n_tiles),
        in_specs=[pl.BlockSpec((1, 1, 2 * tm), lambda bi, j: (bi * n_tiles + j, 0, 0), memory_space=pltpu.SMEM),
                  pl.BlockSpec((1, tm, d), row), pl.BlockSpec((1, tm, LANES), row),
                  pl.BlockSpec((1, 6, d), lambda bi, j: (bi, 0, 0)), _const_spec(fg.shape),
                  pl.BlockSpec(memory_space=pl.ANY)],
        out_specs=pl.BlockSpec((1, tm, d), row),
        out_shape=jax.ShapeDtypeStruct((b, n_tiles * tm, d), F32),
        scratch_shapes=[pltpu.VMEM((tm, d), F32), pltpu.VMEM((tm, d), F32), pltpu.SemaphoreType.DMA(())],
        compiler_params=_cp(("arbitrary", "arbitrary")), name="moe_combine",
    )(pos, x, info, mod, fg, ys)


def moe_layer(x, mod, g2, rw, wg, wu, wd, fg, n_lat):
    b, s, d = x.shape
    n_tiles = n_lat // TOKEN_TILE
    t = b * n_lat
    rw_pad = jnp.zeros((d, LANES), F32).at[:, :N_EXPERTS].set(rw.astype(F32))
    h2, info = moe_route(x, mod, g2, rw_pad, n_tiles)
    info2 = info.reshape(t, LANES)

    e_pair = info2[:, :2].astype(jnp.int32).reshape(2 * t)
    onehot = (e_pair[:, None] == jnp.arange(N_EXPERTS)[None, :]).astype(jnp.int32)
    csum = jnp.cumsum(onehot, axis=0)
    rank = jnp.sum((csum - onehot) * onehot, axis=1)
    counts = csum[-1]
    tiles_e = (counts + MOE_TILE - 1) // MOE_TILE
    tile_end = jnp.cumsum(tiles_e)
    slot_off = (tile_end - tiles_e) * MOE_TILE
    pos = (slot_off[e_pair] + rank).astype(jnp.int32).reshape(t // TOKEN_TILE, 1, 2 * TOKEN_TILE)
    n_tiles_max = (2 * t) // MOE_TILE + N_EXPERTS
    n_valid = tile_end[-1:].astype(jnp.int32)
    tile_ids = jnp.minimum(jnp.arange(n_tiles_max), n_valid[0] - 1)
    tile_expert = jnp.sum((tile_ids[:, None] >= tile_end[None, :]).astype(jnp.int32), axis=1).astype(jnp.int32)

    xs = moe_dispatch(h2.reshape(t, d), pos, n_tiles_max * MOE_TILE)
    ys = moe_experts(xs, tile_expert, n_valid, wg, wu, wd)
    return moe_combine(x, info, mod, fg, ys, pos, n_tiles)


def _rope_tables(n_lat, n_ctx):
    t = jnp.arange(n_lat)
    pos = jnp.stack([t // GRID_W, t % GRID_W], axis=-1).astype(F32)
    half = HEAD_DIM // 2
    inv = 1.0 / (ROPE_BASE ** (jnp.arange(0, half, 2, dtype=F32) / half))
    ang = pos[:, :, None] * inv
    ang = jnp.concatenate([ang, ang], axis=-1).reshape(n_lat, HEAD_DIM)
    cos = jnp.concatenate([jnp.cos(ang), jnp.ones((n_ctx, HEAD_DIM), F32)], axis=0)
    sin = jnp.concatenate([jnp.sin(ang), jnp.zeros((n_ctx, HEAD_DIM), F32)], axis=0)
    return jnp.tile(cos, (1, 2)), jnp.tile(sin, (1, 2))


def _slot_offsets(kind):
    if kind == "gqa":
        return [(h // GQA_GROUP) * HEAD_DIM for h in range(GQA_HEADS)]
    return [(h % 2) * HEAD_DIM for h in range(NA_HEADS)]


def _pad_cols(w, kind):
    d = w.shape[0]
    out = jnp.zeros((d, GQA_HEADS, LANES), w.dtype)
    for h, off in enumerate(_slot_offsets(kind)):
        out = out.at[:, h, off:off + HEAD_DIM].set(w[:, h * HEAD_DIM:(h + 1) * HEAD_DIM])
    return out.reshape(d, GQA_HEADS * LANES)


def _pad_rows(w, kind):
    return _pad_cols(w.T, kind).T


def kernel(x, c, ctx, c_ctx, w_mod, b_mod, norm1_g, w_in, ssm_a_re, ssm_a_im, ssm_b_re, ssm_b_im, ssm_c_re,
           ssm_c_im, ssm_log_dt, ssm_d, glu_w, glu_b, q_norm_g, k_norm_g, na_rpb, w_branch_ssm, w_branch_gqa,
           w_branch_na, w_out, norm2_g, ffn_w_gate, ffn_w_up, ffn_w_down, router_w, moe_w_gate, moe_w_up,
           moe_w_down, final_norm_g):
    b, n_lat, d = x.shape
    n_ctx = ctx.shape[1]
    s = n_lat + n_ctx
    depth = w_mod.shape[0]
    assert d == D_MODEL and n_lat % (NA_QROWS * GRID_W) == 0 and n_ctx == TOKEN_TILE and n_lat % 512 == 0
    assert depth == 2 and n_lat // GRID_W >= NA_UNION
    n_lat_tiles = n_lat // TOKEN_TILE
    n_all_tiles = s // TOKEN_TILE

    n_rows = -(-(b + 1) // 8) * 8
    cv = jnp.zeros((n_rows, d), F32).at[:b].set(c).at[b].set(c_ctx)
    mods = adaln(cv, w_mod, b_mod)[:, :b + 1].reshape(depth, b + 1, 6, d)

    cos, sin = _rope_tables(n_lat, n_ctx)
    table_shape_rows = n_lat // GRID_W
    xall = jnp.concatenate([x, ctx], axis=1)

    out = None
    for i in range(depth):
        last = i == depth - 1
        wi = w_in[i]
        c0 = SSM_WIDTH
        wu = wi[:, :c0].astype(BF)
        wkv = wi[:, c0:KV_COLS].astype(BF)
        wqg = _pad_cols(wi[:, KV_COLS:KV_COLS + GQA_Q_W], "gqa").astype(BF)
        wqn = _pad_cols(wi[:, KV_COLS + GQA_Q_W:KV_COLS + GQA_Q_W + NA_W], "na").astype(BF)
        wgt = wi[:, KV_COLS + GQA_Q_W + NA_W:].astype(BF)
        qgain = _pad_cols(jnp.tile(q_norm_g[i].astype(F32), GQA_HEADS)[None, :], "gqa")
        kgain = jnp.tile(k_norm_g[i].astype(F32), GQA_KV_HEADS)[None, :]

        uq, kg, vg, kn, vn, qg, qn, gt = inproj(xall, mods[i], norm1_g[i][None, :], cos, sin, wu, wkv, wqg, wqn,
                                                wgt, qgain, kgain, n_lat_tiles)

        mats = s5_matrices(ssm_a_re[i], ssm_a_im[i], ssm_b_re[i], ssm_b_im[i], ssm_c_re[i], ssm_c_im[i],
                           ssm_log_dt[i], ssm_d[i])
        y4 = s5_branch(uq, mats, n_lat // SSM_CHUNK)

        n_q_tiles = n_lat_tiles if last else n_all_tiles
        og = gqa_attention(qg, kg, vg, n_lat, n_q_tiles)
        on = na_attention(qn, kn, vn, na_table(na_rpb[i], table_shape_rows, n_ctx), n_lat, n_q_tiles)

        xall = merge(xall, y4, og, on, gt, mods[i], glu_w[i].astype(BF), glu_b[i][None, :].astype(F32),
                     w_branch_ssm[i].astype(BF), _pad_rows(w_branch_gqa[i], "gqa").astype(BF),
                     _pad_rows(w_branch_na[i], "na").astype(BF), w_out[i].astype(BF), n_q_tiles, n_lat_tiles)

        j = i // 2
        if i % 2 == 0:
            xall = ffn(xall, mods[i], norm2_g[i][None, :], ffn_w_gate[j].astype(BF), ffn_w_up[j].astype(BF),
                       ffn_w_down[j].astype(BF), n_lat_tiles)
        else:
            out = moe_layer(xall, mods[i], norm2_g[i][None, :], router_w[j], moe_w_gate[j].astype(BF),
                            moe_w_up[j].astype(BF), moe_w_down[j].astype(BF), final_norm_g[None, :], n_lat)
    return out
```

```python
import functools
import math

import jax
import jax.numpy as jnp
from jax import lax
from jax.experimental import pallas as pl
from jax.experimental.pallas import tpu as pltpu

D_MODEL = 1024
GRID_W = 64
SSM_WIDTH = 512
SSM_GROUP = 16
SSM_GROUPS = SSM_WIDTH // SSM_GROUP
SSM_STATE = 64
HEAD_DIM = 64
GQA_HEADS = 8
GQA_KV_HEADS = 2
GQA_GROUP = GQA_HEADS // GQA_KV_HEADS
ROPE_BASE = 10000.0
NA_HEADS = 8
NA_WIN_R = 8
NA_WIN_C = 16
ATTN_SCALE = HEAD_DIM ** -0.5
LOG2E = math.log2(math.e)
GQA_Q_W = GQA_HEADS * HEAD_DIM
GQA_KV_W = GQA_KV_HEADS * HEAD_DIM
NA_W = NA_HEADS * HEAD_DIM
KV_COLS = SSM_WIDTH + 2 * GQA_KV_W + 2 * NA_W
N_EXPERTS = 8
NORM_EPS = 1e-6

LANES = 128
TOKEN_TILE = 256
SSM_CHUNK = 16
SSM_QT = SSM_WIDTH // LANES
SSM_GPT = LANES // SSM_GROUP
SSM_ST = SSM_GPT * SSM_STATE
NA_QROWS = 4
NA_UNION = 12
MOE_TILE = 512
MOE_FCHUNK = 1792
NEG = -1e30

BF = jnp.bfloat16
F32 = jnp.float32


def _cp(sem, vmem_mb=48):
    return pltpu.CompilerParams(dimension_semantics=sem, vmem_limit_bytes=vmem_mb * 1024 * 1024)


def _const_spec(shape):
    nd = len(shape)
    return pl.BlockSpec(shape, lambda *_: (0,) * nd, pipeline_mode=pl.Buffered(1))


def _dot(a, b):
    return jnp.dot(a, b, preferred_element_type=F32)


def _dot_nt(a, b):
    return lax.dot_general(a, b, (((1,), (1,)), ((), ())), preferred_element_type=F32)


def _rms(x, g):
    return x * lax.rsqrt(jnp.mean(x * x, axis=-1, keepdims=True) + NORM_EPS) * g


def _silu(x):
    return x * jax.nn.sigmoid(x)


def _gelu_tanh(x):
    return 0.5 * x * (1.0 + jnp.tanh(math.sqrt(2.0 / math.pi) * (x + 0.044715 * x * x * x)))


def _adaln_kernel(cv_ref, w_ref, b_ref, o_ref):
    a = _silu(cv_ref[...])
    w = w_ref[0]
    a_hi = a.astype(BF)
    a_lo = (a - a_hi.astype(F32)).astype(BF)
    w_hi = w.astype(BF)
    w_lo = (w - w_hi.astype(F32)).astype(BF)
    o_ref[0] = _dot(a_hi, w_hi) + _dot(a_hi, w_lo) + _dot(a_lo, w_hi) + b_ref[0]


def adaln(cv, w_mod, b_mod):
    depth, d, n = w_mod.shape
    r = cv.shape[0]
    tn = 512
    return pl.pallas_call(
        _adaln_kernel,
        grid=(depth, n // tn),
        in_specs=[pl.BlockSpec((r, d), lambda i, j: (0, 0)),
                  pl.BlockSpec((1, d, tn), lambda i, j: (i, 0, j)),
                  pl.BlockSpec((1, 1, tn), lambda i, j: (i, 0, j))],
        out_specs=pl.BlockSpec((1, r, tn), lambda i, j: (i, 0, j)),
        out_shape=jax.ShapeDtypeStruct((depth, r, n), F32),
        compiler_params=_cp(("arbitrary", "arbitrary")),
        name="adaln",
    )(cv, w_mod, b_mod.reshape(depth, 1, n))


def _rope(xn, cos, sin):
    lane = lax.broadcasted_iota(jnp.int32, xn.shape, 1)
    first = (lane % 32) < 16
    rot = jnp.where(first, -pltpu.roll(xn, LANES - 16, 1), pltpu.roll(xn, 16, 1))
    return xn * cos + rot * sin


def _inproj_kernel(x_ref, mod_ref, g_ref, cos_ref, sin_ref, wu_ref, wkv_ref, wqg_ref, wqn_ref, wgt_ref,
                   qgain_ref, kgain_ref,
                   u_ref, kg_ref, vg_ref, kn_ref, vn_ref, qg_ref, qn_ref, gt_ref):
    x = x_ref[0]
    mod = mod_ref[0]
    h = _rms(x, g_ref[...]) * (1.0 + mod[1:2]) + mod[0:1]
    hb = h.astype(BF)
    cos = cos_ref[...]
    sin = sin_ref[...]

    u = _dot(hb, wu_ref[...])
    for q in range(SSM_QT):
        u_ref[q, 0] = u[:, q * LANES:(q + 1) * LANES]

    kv = _dot(hb, wkv_ref[...])
    kg = kv[:, :LANES]
    lane = lax.broadcasted_iota(jnp.int32, kg.shape, 1)
    lo = lane < HEAD_DIM
    sq = kg * kg
    ms_lo = jnp.sum(jnp.where(lo, sq, 0.0), axis=-1, keepdims=True)
    ms_hi = jnp.sum(jnp.where(lo, 0.0, sq), axis=-1, keepdims=True)
    ms = jnp.where(lo, ms_lo, ms_hi) * (1.0 / HEAD_DIM)
    kgn = kg * lax.rsqrt(ms + NORM_EPS) * kgain_ref[...]
    kg_ref[0] = _rope(kgn, cos, sin).astype(BF)
    vg_ref[0] = kv[:, LANES:2 * LANES].astype(BF)
    kn_ref[0] = kv[:, 2 * LANES:2 * LANES + NA_W].astype(BF)
    vn_ref[0] = kv[:, 2 * LANES + NA_W:].astype(BF)

    qg = _dot(hb, wqg_ref[...])
    for s in range(GQA_HEADS):
        qs = qg[:, s * LANES:(s + 1) * LANES]
        ms = jnp.sum(qs * qs, axis=-1, keepdims=True) * (1.0 / HEAD_DIM)
        qsn = qs * lax.rsqrt(ms + NORM_EPS) * qgain_ref[:, s * LANES:(s + 1) * LANES]
        qg_ref[0, :, s * LANES:(s + 1) * LANES] = (_rope(qsn, cos, sin) * (ATTN_SCALE * LOG2E)).astype(BF)

    qn_ref[0] = (_dot(hb, wqn_ref[...]) * ATTN_SCALE).astype(BF)
    gt_ref[0] = jax.nn.sigmoid(_dot(hb, wgt_ref[...])).astype(BF)


def inproj(xall, mod, g, cos, sin, wu, wkv, wqg, wqn, wgt, qgain, kgain, n_lat_tiles):
    b, s, d = xall.shape
    tm = TOKEN_TILE
    nt = s // tm
    nb = mod.shape[0] - 1

    def row(bi, j):
        return (bi, j, 0)

    def modrow(bi, j):
        return (jnp.where(j < n_lat_tiles, bi, nb), 0, 0)

    outs = [
        jax.ShapeDtypeStruct((SSM_QT, b, s, LANES), F32),
        jax.ShapeDtypeStruct((b, s, LANES), BF),
        jax.ShapeDtypeStruct((b, s, LANES), BF),
        jax.ShapeDtypeStruct((b, s, NA_W), BF),
        jax.ShapeDtypeStruct((b, s, NA_W), BF),
        jax.ShapeDtypeStruct((b, s, GQA_HEADS * LANES), BF),
        jax.ShapeDtypeStruct((b, s, NA_HEADS * LANES), BF),
        jax.ShapeDtypeStruct((b, s, 3 * d), BF),
    ]
    out_specs = [
        pl.BlockSpec((SSM_QT, 1, tm, LANES), lambda bi, j: (0, bi, j, 0)),
        pl.BlockSpec((1, tm, LANES), row),
        pl.BlockSpec((1, tm, LANES), row),
        pl.BlockSpec((1, tm, NA_W), row),
        pl.BlockSpec((1, tm, NA_W), row),
        pl.BlockSpec((1, tm, GQA_HEADS * LANES), row),
        pl.BlockSpec((1, tm, NA_HEADS * LANES), row),
        pl.BlockSpec((1, tm, 3 * d), row),
    ]
    in_specs = [
        pl.BlockSpec((1, tm, d), row),
        pl.BlockSpec((1, 6, d), modrow),
        _const_spec(g.shape),
        pl.BlockSpec((tm, LANES), lambda bi, j: (j, 0)),
        pl.BlockSpec((tm, LANES), lambda bi, j: (j, 0)),
        _const_spec(wu.shape), _const_spec(wkv.shape), _const_spec(wqg.shape),
        _const_spec(wqn.shape), _const_spec(wgt.shape), _const_spec(qgain.shape), _const_spec(kgain.shape),
    ]
    return pl.pallas_call(
        _inproj_kernel, grid=(b, nt), in_specs=in_specs, out_specs=out_specs, out_shape=outs,
        compiler_params=_cp(("arbitrary", "arbitrary"), 56), name="inproj",
    )(xall, mod, g, cos, sin, wu, wkv, wqg, wqn, wgt, qgain, kgain)


def _chunk_rows(u_ref):
    nk = u_ref.shape[2] // SSM_CHUNK
    cols = [u_ref[0, 0, pl.ds(j, nk, stride=SSM_CHUNK), :] for j in range(SSM_CHUNK)]
    return jnp.concatenate(cols, axis=1).astype(BF)


def _ssm_in_kernel(u_ref, m_ref, zf_ref, zr_ref):
    z = _dot(_chunk_rows(u_ref), m_ref[0])
    half = z.shape[1] // 2
    zf_ref[0] = z[:, :half]
    zr_ref[0] = z[:, half:]


def ssm_in(uq, m1):
    qt, b, s, _ = uq.shape
    nk = s // SSM_CHUNK
    k = SSM_CHUNK * LANES
    w = 2 * SSM_ST
    zspec = pl.BlockSpec((1, nk, w), lambda q, bi: (bi, 0, q))
    return pl.pallas_call(
        _ssm_in_kernel, grid=(qt, b),
        in_specs=[pl.BlockSpec((1, 1, s, LANES), lambda q, bi: (q, bi, 0, 0)),
                  pl.BlockSpec((1, k, 2 * w), lambda q, bi: (q, 0, 0))],
        out_specs=[zspec, zspec],
        out_shape=[jax.ShapeDtypeStruct((b, nk, qt * w), F32)] * 2,
        compiler_params=_cp(("arbitrary", "arbitrary")), name="ssm_in",
    )(uq, m1)


def _ssm_scan_kernel(zf_ref, zr_ref, af_ref, ar_ref, sf_ref, sr_ref, *, n_lat_chunks):
    nk = zf_ref.shape[1]
    st = SSM_ST
    af_re, af_im = af_ref[:, :st], af_ref[:, st:]
    ar_re, ar_im = ar_ref[:, :st], ar_ref[:, st:]

    def step(i, carry):
        fre, fim, rre, rim = carry
        kf = lax.rem(i + n_lat_chunks, nk)
        kr = nk - 1 - i
        sf_ref[0, pl.ds(kf, 1), :] = jnp.concatenate([fre, fim], axis=1)
        sr_ref[0, pl.ds(kr, 1), :] = jnp.concatenate([rre, rim], axis=1)
        zf = zf_ref[0, pl.ds(kf, 1), :]
        zr = zr_ref[0, pl.ds(kr, 1), :]
        nfre = af_re * fre - af_im * fim + zf[:, :st]
        nfim = af_re * fim + af_im * fre + zf[:, st:]
        nrre = ar_re * rre - ar_im * rim + zr[:, :st]
        nrim = ar_re * rim + ar_im * rre + zr[:, st:]
        return nfre, nfim, nrre, nrim

    z0 = jnp.zeros((1, st), F32)
    lax.fori_loop(0, nk, step, (z0, z0, z0, z0))


def ssm_scan(zf, zr, af, ar, n_lat_chunks):
    b, nk, n = zf.shape
    w = 2 * SSM_ST
    spec = pl.BlockSpec((1, nk, w), lambda bi, q: (bi, 0, q))
    aspec = pl.BlockSpec((1, w), lambda bi, q: (0, q))
    return pl.pallas_call(
        functools.partial(_ssm_scan_kernel, n_lat_chunks=n_lat_chunks),
        grid=(b, n // w), in_specs=[spec, spec, aspec, aspec], out_specs=[spec, spec],
        out_shape=[jax.ShapeDtypeStruct((b, nk, n), F32)] * 2,
        compiler_params=_cp(("arbitrary", "arbitrary")), name="ssm_scan",
    )(zf, zr, af, ar)


def _ssm_out_kernel(u_ref, sf_ref, sr_ref, mu_ref, mf_ref, mr_ref, y_ref):
    nk = sf_ref.shape[1]
    y = _dot(_chunk_rows(u_ref), mu_ref[0])
    y += _dot(sf_ref[0].astype(BF), mf_ref[0])
    y += _dot(sr_ref[0].astype(BF), mr_ref[0])
    for j in range(SSM_CHUNK):
        y_ref[0, 0, pl.ds(j, nk, stride=SSM_CHUNK), :] = y[:, j * LANES:(j + 1) * LANES]


def ssm_out(uq, sf, sr, m3u, m3f, m3r):
    qt, b, s, _ = uq.shape
    nk = s // SSM_CHUNK
    k = SSM_CHUNK * LANES
    w = 2 * SSM_ST
    one = pl.Buffered(1)
    uspec = pl.BlockSpec((1, 1, s, LANES), lambda q, bi: (q, bi, 0, 0))
    sspec = pl.BlockSpec((1, nk, w), lambda q, bi: (bi, 0, q))
    return pl.pallas_call(
        _ssm_out_kernel, grid=(qt, b),
        in_specs=[uspec, sspec, sspec,
                  pl.BlockSpec((1, k, k), lambda q, bi: (q, 0, 0), pipeline_mode=one),
                  pl.BlockSpec((1, w, k), lambda q, bi: (q, 0, 0), pipeline_mode=one),
                  pl.BlockSpec((1, w, k), lambda q, bi: (q, 0, 0), pipeline_mode=one)],
        out_specs=uspec,
        out_shape=jax.ShapeDtypeStruct((qt, b, s, LANES), F32),
        compiler_params=_cp(("arbitrary", "arbitrary")), name="ssm_out",
    )(uq, sf, sr, m3u, m3f, m3r)


def s5_matrices(a_re, a_im, b_re, b_im, c_re, c_im, log_dt, d_skip):
    tc, g, p, hh = SSM_CHUNK, SSM_GROUPS, SSM_STATE, SSM_GROUP
    lam = lax.complex(a_re.astype(F32), a_im.astype(F32))
    dt = jnp.exp(log_dt.astype(F32))[..., None]
    ldt = lam * dt
    a_bar = jnp.exp(ldt)
    b_bar = ((a_bar - 1.0) / lam)[..., None] * lax.complex(b_re.astype(F32), b_im.astype(F32))
    cm = lax.complex(c_re.astype(F32), c_im.astype(F32))
    steps = jnp.arange(tc + 1, dtype=F32)[:, None, None, None]
    pw = jnp.exp(ldt[None] * steps)
    pw_desc = jnp.exp(ldt[None] * (tc - steps))
    hi = lax.Precision.HIGHEST
    gpt = SSM_GPT

    def widen(compact, col_expand, row_div, col_div):
        rows, cols = compact.shape[1], col_expand.shape[1]
        wide = jnp.einsum('qrc,cn->qrn', compact.astype(BF), col_expand, preferred_element_type=F32)
        rg = (jnp.arange(rows) // row_div) % gpt
        cg = (jnp.arange(cols) // col_div) % gpt
        return jnp.where((rg[:, None] == cg[None, :])[None], wide, 0.0).astype(BF)

    x_tok = jnp.einsum('ab,m,cd->acbmd', jnp.eye(tc, dtype=F32), jnp.ones((gpt,), F32),
                       jnp.eye(hh, dtype=F32)).reshape(tc * hh, tc * LANES).astype(BF)
    x_st = jnp.einsum('ab,m,cd->acbmd', jnp.eye(4, dtype=F32), jnp.ones((gpt,), F32),
                      jnp.eye(p, dtype=F32)).reshape(4 * p, 4 * SSM_ST).astype(BF)

    kk = jnp.real(jnp.einsum('dghp,ldgp,dgpk->dlghk', cm, pw[:tc], b_bar))
    jj = jnp.arange(tc)[None, :, None]
    ii = jnp.arange(tc)[None, None, :]
    lag = jnp.arange(tc)[:, None, None]
    shift_f = (ii - jj == lag).astype(F32)
    shift_r = (jj - ii == lag).astype(F32)
    dd = (jnp.eye(tc, dtype=F32)[:, :, None, None, None] * jnp.eye(hh, dtype=F32)[None, None, None]
          * d_skip.astype(F32).reshape(g, hh)[None, None, :, :, None])
    t5 = (jnp.einsum('lji,lghk->jighk', shift_f, kk[0], precision=hi)
          + jnp.einsum('lji,lghk->jighk', shift_r, kk[1], precision=hi) + dd)
    tcomp = t5.reshape(tc, tc, SSM_QT, gpt, hh, hh).transpose(2, 0, 3, 5, 1, 4)
    m3u = widen(tcomp.reshape(SSM_QT, tc * LANES, tc * hh), x_tok, hh, hh)

    inf = pw_desc[1:, 0][..., None] * b_bar[0][None]
    inr = pw[:tc, 1][..., None] * b_bar[1][None]
    m1 = jnp.stack([jnp.real(inf), jnp.imag(inf), jnp.real(inr), jnp.imag(inr)], axis=0)
    m1 = m1.reshape(2, 2, tc, SSM_QT, gpt, p, hh).transpose(3, 2, 4, 6, 0, 1, 5)
    m1 = widen(m1.reshape(SSM_QT, tc * LANES, 4 * p), x_st, hh, p)

    def out_mat(d, powers):
        ca = cm[d][None] * powers[:, :, None, :]
        mm = jnp.stack([jnp.real(ca), -jnp.imag(ca)], axis=0)
        mm = mm.reshape(2, tc, SSM_QT, gpt, hh, p).transpose(2, 0, 3, 5, 1, 4)
        return widen(mm.reshape(SSM_QT, 2 * SSM_ST, tc * hh), x_tok, p, hh)

    m3f = out_mat(0, pw[1:, 0])
    m3r = out_mat(1, pw_desc[:tc, 1])

    def decay(d):
        a16 = pw[tc, d].reshape(SSM_QT, SSM_ST)
        return jnp.concatenate([jnp.real(a16), jnp.imag(a16)], axis=1).reshape(1, SSM_QT * 2 * SSM_ST)

    return m1.astype(BF), m3u.astype(BF), m3f.astype(BF), m3r.astype(BF), decay(0), decay(1)


def s5_branch(uq4, mats, n_lat_chunks):
    m1, m3u, m3f, m3r, a16f, a16r = mats
    zf, zr = ssm_in(uq4, m1)
    sf, sr = ssm_scan(zf, zr, a16f, a16r, n_lat_chunks)
    return ssm_out(uq4, sf, sr, m3u, m3f, m3r)


def _gqa_kernel(q_ref, k_ref, vt_ref, vtc_ref, o_ref, q4_ref, m_ref, l_ref, acc_ref, *, n_lat, n_ctx, tk,
                n_lat_tiles):
    tq = q_ref.shape[1]
    qi = pl.program_id(2)
    for s in range(GQA_GROUP):
        q4_ref[s * tq:(s + 1) * tq, :] = q_ref[0, :, s * LANES:(s + 1) * LANES]
    m_ref[...] = jnp.full(m_ref.shape, NEG, F32)
    l_ref[...] = jnp.zeros(l_ref.shape, F32)
    acc_ref[...] = jnp.zeros(acc_ref.shape, F32)

    def update(kc, vtc):
        def scores(s):
            return _dot_nt(kc, q4_ref[s * tq:(s + 1) * tq, :])

        st_next = scores(0)
        for s in range(GQA_GROUP):
            cs = slice(s * tq, (s + 1) * tq)
            st = st_next
            if s + 1 < GQA_GROUP:
                st_next = scores(s + 1)
            m_old = m_ref[:, cs]
            m_new = jnp.maximum(m_old, jnp.max(st, axis=0, keepdims=True))
            alpha = jnp.exp2(m_old - m_new)
            p = jnp.exp2(st - m_new)
            l_ref[:, cs] = alpha * l_ref[:, cs] + jnp.sum(p, axis=0, keepdims=True)
            acc_ref[:, cs] = alpha * acc_ref[:, cs] + _dot(vtc, p.astype(BF))
            m_ref[:, cs] = m_new

    def chunk(c, carry):
        off = pl.multiple_of(c * tk, tk)
        update(k_ref[0, pl.ds(off, tk), :], vt_ref[0, c])
        return carry

    lax.fori_loop(0, jnp.where(qi < n_lat_tiles, n_lat // tk, 0), chunk, 0)
    update(k_ref[0, n_lat:n_lat + n_ctx, :], vtc_ref[0])

    o = (acc_ref[...] / l_ref[...]).T
    for s in range(GQA_GROUP):
        o_ref[0, :, s * LANES:(s + 1) * LANES] = o[s * tq:(s + 1) * tq].astype(o_ref.dtype)


def gqa_attention(qg, kg, vg, n_lat, n_q_tiles):
    b, s, _ = qg.shape
    tq = TOKEN_TILE
    tk = 512
    gw = GQA_GROUP * LANES
    n_ctx = s - n_lat
    nc = n_lat // tk
    vt = jnp.swapaxes(vg[:, :n_lat].reshape(b, nc, tk, LANES), 2, 3)
    vtc = jnp.swapaxes(vg[:, n_lat:], 1, 2)
    kern = functools.partial(_gqa_kernel, n_lat=n_lat, n_ctx=n_ctx, tk=tk, n_lat_tiles=n_lat // tq)
    return pl.pallas_call(
        kern, grid=(b, GQA_KV_HEADS, n_q_tiles),
        in_specs=[pl.BlockSpec((1, tq, gw), lambda bi, g, i: (bi, i, g)),
                  pl.BlockSpec((1, s, LANES), lambda bi, g, i: (bi, 0, 0)),
                  pl.BlockSpec((1, nc, LANES, tk), lambda bi, g, i: (bi, 0, 0, 0)),
                  pl.BlockSpec((1, LANES, n_ctx), lambda bi, g, i: (bi, 0, 0))],
        out_specs=pl.BlockSpec((1, tq, gw), lambda bi, g, i: (bi, i, g)),
        out_shape=jax.ShapeDtypeStruct((b, n_q_tiles * tq, GQA_HEADS * LANES), BF),
        scratch_shapes=[pltpu.VMEM((GQA_GROUP * tq, LANES), BF),
                        pltpu.VMEM((1, GQA_GROUP * tq), F32),
                        pltpu.VMEM((1, GQA_GROUP * tq), F32),
                        pltpu.VMEM((LANES, GQA_GROUP * tq), F32)],
        compiler_params=_cp(("arbitrary", "arbitrary", "arbitrary")), name="gqa",
    )(qg, kg, vt, vtc)


def _na_kernel(q_ref, k_ref, v_ref, tab_ref, o_ref, *, n_lat, n_ctx, rows):
    rb = pl.program_id(1)
    ws = jnp.clip(rb * NA_QROWS - NA_WIN_R // 2, 0, rows - NA_UNION)
    off = pl.multiple_of(ws * GRID_W, GRID_W)
    nwin = NA_UNION * GRID_W
    for pair in range(NA_HEADS // 2):
        ls = slice(pair * LANES, (pair + 1) * LANES)
        kcat = jnp.concatenate([k_ref[0, pl.ds(off, nwin), ls], k_ref[0, n_lat:n_lat + n_ctx, ls]], axis=0)
        vcat = jnp.concatenate([v_ref[0, pl.ds(off, nwin), ls], v_ref[0, n_lat:n_lat + n_ctx, ls]], axis=0)
        for half in range(2):
            h = pair * 2 + half
            hs = slice(h * LANES, (h + 1) * LANES)
            s = _dot_nt(q_ref[0, :, hs], kcat) + tab_ref[0, h].astype(F32)
            m = jnp.max(s, axis=-1, keepdims=True)
            p = jnp.exp(s - m)
            l = jnp.sum(p, axis=-1, keepdims=True)
            o_ref[0, :, hs] = (_dot(p.astype(BF), vcat) / l).astype(o_ref.dtype)


def na_attention(qn, kn, vn, table, n_lat, n_q_tiles):
    b, s, _ = qn.shape
    tq = NA_QROWS * GRID_W
    rows = n_lat // GRID_W
    n_lat_tiles = n_lat // tq
    tw = table.shape[-1]

    def tab_idx(bi, rb):
        case = jnp.where(rb == 0, 0, jnp.where(rb < n_lat_tiles - 1, 1, jnp.where(rb == n_lat_tiles - 1, 2, 3)))
        return (case, 0, 0, 0)

    kern = functools.partial(_na_kernel, n_lat=n_lat, n_ctx=s - n_lat, rows=rows)
    return pl.pallas_call(
        kern, grid=(b, n_q_tiles),
        in_specs=[pl.BlockSpec((1, tq, NA_HEADS * LANES), lambda bi, rb: (bi, rb, 0)),
                  pl.BlockSpec((1, s, NA_W), lambda bi, rb: (bi, 0, 0)),
                  pl.BlockSpec((1, s, NA_W), lambda bi, rb: (bi, 0, 0)),
                  pl.BlockSpec((1, NA_HEADS, tq, tw), tab_idx)],
        out_specs=pl.BlockSpec((1, tq, NA_HEADS * LANES), lambda bi, rb: (bi, rb, 0)),
        out_shape=jax.ShapeDtypeStruct((b, n_q_tiles * tq, NA_HEADS * LANES), BF),
        compiler_params=_cp(("arbitrary", "arbitrary"), 56), name="na",
    )(qn, kn, vn, table)


def na_table(rpb, rows, n_ctx):
    tq = NA_QROWS * GRID_W
    qr = jnp.arange(NA_QROWS)[:, None, None, None]
    qc = jnp.arange(GRID_W)[None, :, None, None]
    kr = jnp.arange(NA_UNION)[None, None, :, None]
    kc = jnp.arange(GRID_W)[None, None, None, :]
    cs = jnp.clip(qc - NA_WIN_C // 2, 0, GRID_W - NA_WIN_C)
    col_ok = (kc >= cs) & (kc < cs + NA_WIN_C)
    col_idx = (kc - qc + NA_WIN_C - 1)[:, :, 0, :]
    col_hot = (col_idx[0, :, :, None] == jnp.arange(2 * NA_WIN_C - 1)).astype(F32)
    n_blocks = rows // NA_QROWS
    hi = lax.Precision.HIGHEST
    tabs = []
    for blk in (0, 1, n_blocks - 1):
        r = blk * NA_QROWS + qr
        ws = min(max(blk * NA_QROWS - NA_WIN_R // 2, 0), rows - NA_UNION)
        rs = jnp.clip(r - NA_WIN_R // 2, 0, rows - NA_WIN_R)
        ka = ws + kr
        ok = (ka >= rs) & (ka < rs + NA_WIN_R) & col_ok
        row_idx = (ka - r + NA_WIN_R - 1)[:, 0, :, 0]
        row_hot = (row_idx[:, :, None] == jnp.arange(2 * NA_WIN_R - 1)).astype(F32)
        by_row = jnp.einsum('hab,rka->hrkb', rpb.astype(F32), row_hot, precision=hi)
        bias = jnp.einsum('hrkb,qcb->hrqkc', by_row, col_hot, precision=hi)
        tabs.append(jnp.where(ok[None], bias, NEG).reshape(NA_HEADS, tq, NA_UNION * GRID_W))
    tabs.append(jnp.full((NA_HEADS, tq, NA_UNION * GRID_W), NEG, F32))
    win = jnp.stack(tabs, axis=0)
    return jnp.concatenate([win, jnp.zeros(win.shape[:3] + (n_ctx,), F32)], axis=-1).astype(BF)


def _merge_kernel(x_ref, y_ref, og_ref, on_ref, gt_ref, mod_ref, gw_ref, gb_ref, ws_ref, wg_ref, wn_ref, wo_ref,
                  o_ref):
    d = x_ref.shape[-1]
    y = jnp.concatenate([y_ref[q, 0].astype(F32) for q in range(SSM_QT)], axis=1)
    gy = _gelu_tanh(y)
    ys = gy * jax.nn.sigmoid(_dot(gy.astype(BF), gw_ref[...]) + gb_ref[...])
    gt = gt_ref[0]
    m = gt[:, :d].astype(F32) * _dot(ys.astype(BF), ws_ref[...])
    m += gt[:, d:2 * d].astype(F32) * _dot(og_ref[0], wg_ref[...])
    m += gt[:, 2 * d:].astype(F32) * _dot(on_ref[0], wn_ref[...])
    o_ref[0] = x_ref[0] + mod_ref[0][2:3] * _dot(m.astype(BF), wo_ref[...])


def merge(xall, y4, og, on, gt, mod, gw, gb, ws, wg, wn, wo, n_tiles, n_lat_tiles):
    b, s, d = xall.shape
    tm = TOKEN_TILE
    nb = mod.shape[0] - 1

    def row(bi, j):
        return (bi, j, 0)

    return pl.pallas_call(
        _merge_kernel, grid=(b, n_tiles),
        in_specs=[pl.BlockSpec((1, tm, d), row),
                  pl.BlockSpec((SSM_QT, 1, tm, LANES), lambda bi, j: (0, bi, j, 0)),
                  pl.BlockSpec((1, tm, GQA_HEADS * LANES), row),
                  pl.BlockSpec((1, tm, NA_HEADS * LANES), row),
                  pl.BlockSpec((1, tm, 3 * d), row),
                  pl.BlockSpec((1, 6, d), lambda bi, j: (jnp.where(j < n_lat_tiles, bi, nb), 0, 0)),
                  _const_spec(gw.shape), _const_spec(gb.shape), _const_spec(ws.shape),
                  _const_spec(wg.shape), _const_spec(wn.shape), _const_spec(wo.shape)],
        out_specs=pl.BlockSpec((1, tm, d), row),
        out_shape=jax.ShapeDtypeStruct((b, n_tiles * tm, d), F32),
        compiler_params=_cp(("arbitrary", "arbitrary")), name="merge",
    )(xall, y4, og, on, gt, mod, gw, gb, ws, wg, wn, wo)


def _ffn_kernel(x_ref, mod_ref, g_ref, wg_ref, wu_ref, wd_ref, o_ref):
    x = x_ref[0]
    mod = mod_ref[0]
    hb = (_rms(x, g_ref[...]) * (1.0 + mod[4:5]) + mod[3:4]).astype(BF)
    a = _silu(_dot(hb, wg_ref[...])) * _dot(hb, wu_ref[...])
    o_ref[0] = x + mod[5:6] * _dot(a.astype(BF), wd_ref[...])


def ffn(xall, mod, g, wg, wu, wd, n_lat_tiles):
    b, s, d = xall.shape
    tm = TOKEN_TILE
    nb = mod.shape[0] - 1

    def row(bi, j):
        return (bi, j, 0)

    return pl.pallas_call(
        _ffn_kernel, grid=(b, s // tm),
        in_specs=[pl.BlockSpec((1, tm, d), row),
                  pl.BlockSpec((1, 6, d), lambda bi, j: (jnp.where(j < n_lat_tiles, bi, nb), 0, 0)),
                  _const_spec(g.shape), _const_spec(wg.shape), _const_spec(wu.shape), _const_spec(wd.shape)],
        out_specs=pl.BlockSpec((1, tm, d), row),
        out_shape=jax.ShapeDtypeStruct((b, s, d), F32),
        compiler_params=_cp(("arbitrary", "arbitrary"), 56), name="ffn",
    )(xall, mod, g, wg, wu, wd)


def _route_kernel(x_ref, mod_ref, g_ref, rw_ref, h_ref, info_ref):
    x = x_ref[0]
    mod = mod_ref[0]
    h = _rms(x, g_ref[...]) * (1.0 + mod[4:5]) + mod[3:4]
    h_ref[0] = h
    logits = jnp.dot(h, rw_ref[...], preferred_element_type=F32, precision=lax.Precision.HIGHEST)
    lane = lax.broadcasted_iota(jnp.int32, logits.shape, 1)
    lanef = lane.astype(F32)
    logits = jnp.where(lane < N_EXPERTS, logits, -jnp.inf)
    m1 = jnp.max(logits, axis=-1, keepdims=True)
    i1 = jnp.min(jnp.where(logits == m1, lanef, float(LANES)), axis=-1, keepdims=True)
    rest = jnp.where(lanef == i1, -jnp.inf, logits)
    m2 = jnp.max(rest, axis=-1, keepdims=True)
    i2 = jnp.min(jnp.where(rest == m2, lanef, float(LANES)), axis=-1, keepdims=True)
    e2 = jnp.exp(m2 - m1)
    w1 = 1.0 / (1.0 + e2)
    w2 = e2 / (1.0 + e2)
    info_ref[0] = jnp.where(lane == 0, i1, jnp.where(lane == 1, i2, jnp.where(lane == 2, w1,
                            jnp.where(lane == 3, w2, 0.0))))


def moe_route(xall, mod, g, rw, n_tiles):
    b, s, d = xall.shape
    tm = TOKEN_TILE

    def row(bi, j):
        return (bi, j, 0)

    return pl.pallas_call(
        _route_kernel, grid=(b, n_tiles),
        in_specs=[pl.BlockSpec((1, tm, d), row), pl.BlockSpec((1, 6, d), lambda bi, j: (bi, 0, 0)),
                  _const_spec(g.shape), _const_spec(rw.shape)],
        out_specs=[pl.BlockSpec((1, tm, d), row), pl.BlockSpec((1, tm, LANES), row)],
        out_shape=[jax.ShapeDtypeStruct((b, n_tiles * tm, d), F32),
                   jax.ShapeDtypeStruct((b, n_tiles * tm, LANES), F32)],
        compiler_params=_cp(("arbitrary", "arbitrary")), name="moe_route",
    )(xall, mod, g, rw)


def _dispatch_kernel(pos_ref, h_ref, xs_in_ref, xs_ref, sem):
    del xs_in_ref
    tm = h_ref.shape[0]

    def row_copy(r, k):
        return pltpu.make_async_copy(h_ref.at[pl.ds(r, 1)], xs_ref.at[pl.ds(pos_ref[0, 0, 2 * r + k], 1)], sem)

    def issue(r, c):
        row_copy(r, 0).start()
        row_copy(r, 1).start()
        return c

    def drain(r, c):
        row_copy(r, 0).wait()
        row_copy(r, 1).wait()
        return c

    lax.fori_loop(0, tm, issue, 0, unroll=4)
    lax.fori_loop(0, tm, drain, 0)


def moe_dispatch(h2, pos, n_slots):
    t, d = h2.shape
    tm = TOKEN_TILE
    zeros = jnp.zeros((n_slots, d), F32)
    return pl.pallas_call(
        _dispatch_kernel, grid=(t // tm,),
        in_specs=[pl.BlockSpec((1, 1, 2 * tm), lambda i: (i, 0, 0), memory_space=pltpu.SMEM),
                  pl.BlockSpec((tm, d), lambda i: (i, 0)),
                  pl.BlockSpec(memory_space=pl.ANY)],
        out_specs=pl.BlockSpec(memory_space=pl.ANY),
        out_shape=jax.ShapeDtypeStruct((n_slots, d), F32),
        scratch_shapes=[pltpu.SemaphoreType.DMA(())],
        input_output_aliases={2: 0},
        compiler_params=_cp(("arbitrary",)), name="moe_dispatch",
    )(pos, h2, zeros)


def _experts_kernel(te_ref, nv_ref, x_ref, wg_ref, wu_ref, wd_ref, o_ref, acc_ref):
    i = pl.program_id(0)
    j = pl.program_id(1)

    @pl.when(i < nv_ref[0])
    def _():
        xb = x_ref[...].astype(BF)
        a = _silu(_dot(xb, wg_ref[0])) * _dot(xb, wu_ref[0])
        part = _dot(a.astype(BF), wd_ref[0])

        @pl.when(j == 0)
        def _():
            acc_ref[...] = part

        @pl.when(j > 0)
        def _():
            acc_ref[...] += part

        @pl.when(j == pl.num_programs(1) - 1)
        def _():
            o_ref[...] = acc_ref[...]

    @pl.when((i >= nv_ref[0]) & (j == pl.num_programs(1) - 1))
    def _():
        o_ref[...] = jnp.zeros(o_ref.shape, o_ref.dtype)


def moe_experts(xs, tile_expert, n_valid, wg, wu, wd):
    n_slots, d = xs.shape
    tm = MOE_TILE
    tf = MOE_FCHUNK
    f = wg.shape[-1]
    nf = f // tf

    def xrow(i, j, te, nv):
        return (jnp.minimum(i, nv[0] - 1), 0)

    def fcol(i, j, nv):
        return jnp.where(i < nv[0], j, nf - 1)

    grid_spec = pltpu.PrefetchScalarGridSpec(
        num_scalar_prefetch=2, grid=(n_slots // tm, nf),
        in_specs=[pl.BlockSpec((tm, d), xrow),
                  pl.BlockSpec((1, d, tf), lambda i, j, te, nv: (te[i], 0, fcol(i, j, nv))),
                  pl.BlockSpec((1, d, tf), lambda i, j, te, nv: (te[i], 0, fcol(i, j, nv))),
                  pl.BlockSpec((1, tf, d), lambda i, j, te, nv: (te[i], fcol(i, j, nv), 0))],
        out_specs=pl.BlockSpec((tm, d), lambda i, j, te, nv: (i, 0)),
        scratch_shapes=[pltpu.VMEM((tm, d), F32)])
    return pl.pallas_call(
        _experts_kernel, grid_spec=grid_spec,
        out_shape=jax.ShapeDtypeStruct((n_slots, d), F32),
        compiler_params=_cp(("arbitrary", "arbitrary"), 56), name="moe_experts",
    )(tile_expert, n_valid, xs, wg, wu, wd)


def _combine_kernel(pos_ref, x_ref, info_ref, mod_ref, fg_ref, ys_ref, o_ref, y1_ref, y2_ref, sem):
    tm = x_ref.shape[1]

    def row_copy(r, k, dst):
        return pltpu.make_async_copy(ys_ref.at[pl.ds(pos_ref[0, 0, 2 * r + k], 1)], dst.at[pl.ds(r, 1)], sem)

    def issue(r, c):
        row_copy(r, 0, y1_ref).start()
        row_copy(r, 1, y2_ref).start()
        return c

    def drain(r, c):
        row_copy(r, 0, y1_ref).wait()
        row_copy(r, 1, y2_ref).wait()
        return c

    lax.fori_loop(0, tm, issue, 0, unroll=4)
    lax.fori_loop(0, tm, drain, 0)
    info = info_ref[0]
    y = info[:, 2:3] * y1_ref[...] + info[:, 3:4] * y2_ref[...]
    xn = x_ref[0] + mod_ref[0][5:6] * y
    o_ref[0] = _rms(xn, fg_ref[...])


def moe_combine(x, info, mod, fg, ys, pos, n_tiles):
    b, s, d = x.shape
    tm = TOKEN_TILE

    def row(bi, j):
        return (bi, j, 0)

    return pl.pallas_call(
        _combine_kernel, grid=(b, n_tiles),
        in_specs=[pl.BlockSpec((1, 1, 2 * tm), lambda bi, j: (bi * n_tiles + j, 0, 0), memory_space=pltpu.SMEM),
                  pl.BlockSpec((1, tm, d), row), pl.BlockSpec((1, tm, LANES), row),
                  pl.BlockSpec((1, 6, d), lambda bi, j: (bi, 0, 0)), _const_spec(fg.shape),
                  pl.BlockSpec(memory_space=pl.ANY)],
        out_specs=pl.BlockSpec((1, tm, d), row),
        out_shape=jax.ShapeDtypeStruct((b, n_tiles * tm, d), F32),
        scratch_shapes=[pltpu.VMEM((tm, d), F32), pltpu.VMEM((tm, d), F32), pltpu.SemaphoreType.DMA(())],
        compiler_params=_cp(("arbitrary", "arbitrary")), name="moe_combine",
    )(pos, x, info, mod, fg, ys)


def moe_layer(x, mod, g2, rw, wg, wu, wd, fg, n_lat):
    b, s, d = x.shape
    n_tiles = n_lat // TOKEN_TILE
    t = b * n_lat
    rw_pad = jnp.zeros((d, LANES), F32).at[:, :N_EXPERTS].set(rw.astype(F32))
    h2, info = moe_route(x, mod, g2, rw_pad, n_tiles)
    info2 = info.reshape(t, LANES)

    e_pair = info2[:, :2].astype(jnp.int32).reshape(2 * t)
    onehot = (e_pair[:, None] == jnp.arange(N_EXPERTS)[None, :]).astype(jnp.int32)
    csum = jnp.cumsum(onehot, axis=0)
    rank = jnp.sum((csum - onehot) * onehot, axis=1)
    counts = csum[-1]
    tiles_e = (counts + MOE_TILE - 1) // MOE_TILE
    tile_end = jnp.cumsum(tiles_e)
    slot_off = (tile_end - tiles_e) * MOE_TILE
    pos = (slot_off[e_pair] + rank).astype(jnp.int32).reshape(t // TOKEN_TILE, 1, 2 * TOKEN_TILE)
    n_tiles_max = (2 * t) // MOE_TILE + N_EXPERTS
    n_valid = tile_end[-1:].astype(jnp.int32)
    tile_ids = jnp.minimum(jnp.arange(n_tiles_max), n_valid[0] - 1)
    tile_expert = jnp.sum((tile_ids[:, None] >= tile_end[None, :]).astype(jnp.int32), axis=1).astype(jnp.int32)

    xs = moe_dispatch(h2.reshape(t, d), pos, n_tiles_max * MOE_TILE)
    ys = moe_experts(xs, tile_expert, n_valid, wg, wu, wd)
    return moe_combine(x, info, mod, fg, ys, pos, n_tiles)


def _rope_tables(n_lat, n_ctx):
    t = jnp.arange(n_lat)
    pos = jnp.stack([t // GRID_W, t % GRID_W], axis=-1).astype(F32)
    half = HEAD_DIM // 2
    inv = 1.0 / (ROPE_BASE ** (jnp.arange(0, half, 2, dtype=F32) / half))
    ang = pos[:, :, None] * inv
    ang = jnp.concatenate([ang, ang], axis=-1).reshape(n_lat, HEAD_DIM)
    cos = jnp.concatenate([jnp.cos(ang), jnp.ones((n_ctx, HEAD_DIM), F32)], axis=0)
    sin = jnp.concatenate([jnp.sin(ang), jnp.zeros((n_ctx, HEAD_DIM), F32)], axis=0)
    return jnp.tile(cos, (1, 2)), jnp.tile(sin, (1, 2))


def _slot_offsets(kind):
    if kind == "gqa":
        return [(h // GQA_GROUP) * HEAD_DIM for h in range(GQA_HEADS)]
    return [(h % 2) * HEAD_DIM for h in range(NA_HEADS)]


def _pad_cols(w, kind):
    d = w.shape[0]
    out = jnp.zeros((d, GQA_HEADS, LANES), w.dtype)
    for h, off in enumerate(_slot_offsets(kind)):
        out = out.at[:, h, off:off + HEAD_DIM].set(w[:, h * HEAD_DIM:(h + 1) * HEAD_DIM])
    return out.reshape(d, GQA_HEADS * LANES)


def _pad_rows(w, kind):
    return _pad_cols(w.T, kind).T


def kernel(x, c, ctx, c_ctx, w_mod, b_mod, norm1_g, w_in, ssm_a_re, ssm_a_im, ssm_b_re, ssm_b_im, ssm_c_re,
           ssm_c_im, ssm_log_dt, ssm_d, glu_w, glu_b, q_norm_g, k_norm_g, na_rpb, w_branch_ssm, w_branch_gqa,
           w_branch_na, w_out, norm2_g, ffn_w_gate, ffn_w_up, ffn_w_down, router_w, moe_w_gate, moe_w_up,
           moe_w_down, final_norm_g):
    b, n_lat, d = x.shape
    n_ctx = ctx.shape[1]
    s = n_lat + n_ctx
    depth = w_mod.shape[0]
    assert d == D_MODEL and n_lat % (NA_QROWS * GRID_W) == 0 and n_ctx == TOKEN_TILE and n_lat % 512 == 0
    assert depth == 2 and n_lat // GRID_W >= NA_UNION
    n_lat_tiles = n_lat // TOKEN_TILE
    n_all_tiles = s // TOKEN_TILE

    n_rows = -(-(b + 1) // 8) * 8
    cv = jnp.zeros((n_rows, d), F32).at[:b].set(c).at[b].set(c_ctx)
    mods = adaln(cv, w_mod, b_mod)[:, :b + 1].reshape(depth, b + 1, 6, d)

    cos, sin = _rope_tables(n_lat, n_ctx)
    table_shape_rows = n_lat // GRID_W
    xall = jnp.concatenate([x, ctx], axis=1)

    out = None
    for i in range(depth):
        last = i == depth - 1
        wi = w_in[i]
        c0 = SSM_WIDTH
        wu = wi[:, :c0].astype(BF)
        wkv = wi[:, c0:KV_COLS].astype(BF)
        wqg = _pad_cols(wi[:, KV_COLS:KV_COLS + GQA_Q_W], "gqa").astype(BF)
        wqn = _pad_cols(wi[:, KV_COLS + GQA_Q_W:KV_COLS + GQA_Q_W + NA_W], "na").astype(BF)
        wgt = wi[:, KV_COLS + GQA_Q_W + NA_W:].astype(BF)
        qgain = _pad_cols(jnp.tile(q_norm_g[i].astype(F32), GQA_HEADS)[None, :], "gqa")
        kgain = jnp.tile(k_norm_g[i].astype(F32), GQA_KV_HEADS)[None, :]

        uq, kg, vg, kn, vn, qg, qn, gt = inproj(xall, mods[i], norm1_g[i][None, :], cos, sin, wu, wkv, wqg, wqn,
                                                wgt, qgain, kgain, n_lat_tiles)

        mats = s5_matrices(ssm_a_re[i], ssm_a_im[i], ssm_b_re[i], ssm_b_im[i], ssm_c_re[i], ssm_c_im[i],
                           ssm_log_dt[i], ssm_d[i])
        y4 = s5_branch(uq, mats, n_lat // SSM_CHUNK)

        n_q_tiles = n_lat_tiles if last else n_all_tiles
        og = gqa_attention(qg, kg, vg, n_lat, n_q_tiles)
        on = na_attention(qn, kn, vn, na_table(na_rpb[i], table_shape_rows, n_ctx), n_lat, n_q_tiles)

        xall = merge(xall, y4, og, on, gt, mods[i], glu_w[i].astype(BF), glu_b[i][None, :].astype(F32),
                     w_branch_ssm[i].astype(BF), _pad_rows(w_branch_gqa[i], "gqa").astype(BF),
                     _pad_rows(w_branch_na[i], "na").astype(BF), w_out[i].astype(BF), n_q_tiles, n_lat_tiles)

        j = i // 2
        if i % 2 == 0:
            xall = ffn(xall, mods[i], norm2_g[i][None, :], ffn_w_gate[j].astype(BF), ffn_w_up[j].astype(BF),
                       ffn_w_down[j].astype(BF), n_lat_tiles)
        else:
            out = moe_layer(xall, mods[i], norm2_g[i][None, :], router_w[j], moe_w_gate[j].astype(BF),
                            moe_w_up[j].astype(BF), moe_w_down[j].astype(BF), final_norm_g[None, :], n_lat)
    return out
```

```python
import functools
import math

import jax
import jax.numpy as jnp
from jax import lax
from jax.experimental import pallas as pl
from jax.experimental.pallas import tpu as pltpu

D_MODEL = 1024
GRID_W = 64
SSM_WIDTH = 512
SSM_GROUP = 16
SSM_GROUPS = SSM_WIDTH // SSM_GROUP
SSM_STATE = 64
HEAD_DIM = 64
GQA_HEADS = 8
GQA_KV_HEADS = 2
GQA_GROUP = GQA_HEADS // GQA_KV_HEADS
ROPE_BASE = 10000.0
NA_HEADS = 8
NA_WIN_R = 8
NA_WIN_C = 16
ATTN_SCALE = HEAD_DIM ** -0.5
LOG2E = math.log2(math.e)
GQA_Q_W = GQA_HEADS * HEAD_DIM
GQA_KV_W = GQA_KV_HEADS * HEAD_DIM
NA_W = NA_HEADS * HEAD_DIM
KV_COLS = SSM_WIDTH + 2 * GQA_KV_W + 2 * NA_W
N_EXPERTS = 8
NORM_EPS = 1e-6

LANES = 128
TOKEN_TILE = 256
SSM_CHUNK = 16
SSM_QT = SSM_WIDTH // LANES
SSM_GPT = LANES // SSM_GROUP
SSM_ST = SSM_GPT * SSM_STATE
NA_QROWS = 4
NA_UNION = 12
MOE_TILE = 512
MOE_FCHUNK = 1792
NEG = -1e30

BF = jnp.bfloat16
F32 = jnp.float32


def _cp(sem, vmem_mb=48):
    return pltpu.CompilerParams(dimension_semantics=sem, vmem_limit_bytes=vmem_mb * 1024 * 1024)


def _const_spec(shape):
    nd = len(shape)
    return pl.BlockSpec(shape, lambda *_: (0,) * nd, pipeline_mode=pl.Buffered(1))


def _dot(a, b):
    return jnp.dot(a, b, preferred_element_type=F32)


def _dot_nt(a, b):
    return lax.dot_general(a, b, (((1,), (1,)), ((), ())), preferred_element_type=F32)


def _rms(x, g):
    return x * lax.rsqrt(jnp.mean(x * x, axis=-1, keepdims=True) + NORM_EPS) * g


def _silu(x):
    return x * jax.nn.sigmoid(x)


def _gelu_tanh(x):
    return 0.5 * x * (1.0 + jnp.tanh(math.sqrt(2.0 / math.pi) * (x + 0.044715 * x * x * x)))


def _adaln_kernel(cv_ref, w_ref, b_ref, o_ref):
    a = _silu(cv_ref[...])
    w = w_ref[0]
    a_hi = a.astype(BF)
    a_lo = (a - a_hi.astype(F32)).astype(BF)
    w_hi = w.astype(BF)
    w_lo = (w - w_hi.astype(F32)).astype(BF)
    o_ref[0] = _dot(a_hi, w_hi) + _dot(a_hi, w_lo) + _dot(a_lo, w_hi) + b_ref[0]


def adaln(cv, w_mod, b_mod):
    depth, d, n = w_mod.shape
    r = cv.shape[0]
    tn = 512
    return pl.pallas_call(
        _adaln_kernel,
        grid=(depth, n // tn),
        in_specs=[pl.BlockSpec((r, d), lambda i, j: (0, 0)),
                  pl.BlockSpec((1, d, tn), lambda i, j: (i, 0, j)),
                  pl.BlockSpec((1, 1, tn), lambda i, j: (i, 0, j))],
        out_specs=pl.BlockSpec((1, r, tn), lambda i, j: (i, 0, j)),
        out_shape=jax.ShapeDtypeStruct((depth, r, n), F32),
        compiler_params=_cp(("arbitrary", "arbitrary")),
        name="adaln",
    )(cv, w_mod, b_mod.reshape(depth, 1, n))


def _rope(xn, cos, sin):
    lane = lax.broadcasted_iota(jnp.int32, xn.shape, 1)
    first = (lane % 32) < 16
    rot = jnp.where(first, -pltpu.roll(xn, LANES - 16, 1), pltpu.roll(xn, 16, 1))
    return xn * cos + rot * sin


def _inproj_kernel(x_ref, mod_ref, g_ref, cos_ref, sin_ref, wu_ref, wkv_ref, wqg_ref, wqn_ref, wgt_ref,
                   qgain_ref, kgain_ref,
                   u_ref, kg_ref, vg_ref, kn_ref, vn_ref, qg_ref, qn_ref, gt_ref):
    x = x_ref[0]
    mod = mod_ref[0]
    h = _rms(x, g_ref[...]) * (1.0 + mod[1:2]) + mod[0:1]
    hb = h.astype(BF)
    cos = cos_ref[...]
    sin = sin_ref[...]

    u = _dot(hb, wu_ref[...])
    for q in range(SSM_QT):
        u_ref[q, 0] = u[:, q * LANES:(q + 1) * LANES]

    kv = _dot(hb, wkv_ref[...])
    kg = kv[:, :LANES]
    lane = lax.broadcasted_iota(jnp.int32, kg.shape, 1)
    lo = lane < HEAD_DIM
    sq = kg * kg
    ms_lo = jnp.sum(jnp.where(lo, sq, 0.0), axis=-1, keepdims=True)
    ms_hi = jnp.sum(jnp.where(lo, 0.0, sq), axis=-1, keepdims=True)
    ms = jnp.where(lo, ms_lo, ms_hi) * (1.0 / HEAD_DIM)
    kgn = kg * lax.rsqrt(ms + NORM_EPS) * kgain_ref[...]
    kg_ref[0] = _rope(kgn, cos, sin).astype(BF)
    vg_ref[0] = kv[:, LANES:2 * LANES].astype(BF)
    kn_ref[0] = kv[:, 2 * LANES:2 * LANES + NA_W].astype(BF)
    vn_ref[0] = kv[:, 2 * LANES + NA_W:].astype(BF)

    qg = _dot(hb, wqg_ref[...])
    for s in range(GQA_HEADS):
        qs = qg[:, s * LANES:(s + 1) * LANES]
        ms = jnp.sum(qs * qs, axis=-1, keepdims=True) * (1.0 / HEAD_DIM)
        qsn = qs * lax.rsqrt(ms + NORM_EPS) * qgain_ref[:, s * LANES:(s + 1) * LANES]
        qg_ref[0, :, s * LANES:(s + 1) * LANES] = (_rope(qsn, cos, sin) * (ATTN_SCALE * LOG2E)).astype(BF)

    qn_ref[0] = (_dot(hb, wqn_ref[...]) * (ATTN_SCALE * LOG2E)).astype(BF)
    gt_ref[0] = jax.nn.sigmoid(_dot(hb, wgt_ref[...])).astype(BF)


def inproj(xall, mod, g, cos, sin, wu, wkv, wqg, wqn, wgt, qgain, kgain, n_lat_tiles):
    b, s, d = xall.shape
    tm = TOKEN_TILE
    nt = s // tm
    nb = mod.shape[0] - 1

    def row(bi, j):
        return (bi, j, 0)

    def modrow(bi, j):
        return (jnp.where(j < n_lat_tiles, bi, nb), 0, 0)

    outs = [
        jax.ShapeDtypeStruct((SSM_QT, b, s, LANES), F32),
        jax.ShapeDtypeStruct((b, s, LANES), BF),
        jax.ShapeDtypeStruct((b, s, LANES), BF),
        jax.ShapeDtypeStruct((b, s, NA_W), BF),
        jax.ShapeDtypeStruct((b, s, NA_W), BF),
        jax.ShapeDtypeStruct((b, s, GQA_HEADS * LANES), BF),
        jax.ShapeDtypeStruct((b, s, NA_HEADS * LANES), BF),
        jax.ShapeDtypeStruct((b, s, 3 * d), BF),
    ]
    out_specs = [
        pl.BlockSpec((SSM_QT, 1, tm, LANES), lambda bi, j: (0, bi, j, 0)),
        pl.BlockSpec((1, tm, LANES), row),
        pl.BlockSpec((1, tm, LANES), row),
        pl.BlockSpec((1, tm, NA_W), row),
        pl.BlockSpec((1, tm, NA_W), row),
        pl.BlockSpec((1, tm, GQA_HEADS * LANES), row),
        pl.BlockSpec((1, tm, NA_HEADS * LANES), row),
        pl.BlockSpec((1, tm, 3 * d), row),
    ]
    in_specs = [
        pl.BlockSpec((1, tm, d), row),
        pl.BlockSpec((1, 6, d), modrow),
        _const_spec(g.shape),
        pl.BlockSpec((tm, LANES), lambda bi, j: (j, 0)),
        pl.BlockSpec((tm, LANES), lambda bi, j: (j, 0)),
        _const_spec(wu.shape), _const_spec(wkv.shape), _const_spec(wqg.shape),
        _const_spec(wqn.shape), _const_spec(wgt.shape), _const_spec(qgain.shape), _const_spec(kgain.shape),
    ]
    return pl.pallas_call(
        _inproj_kernel, grid=(b, nt), in_specs=in_specs, out_specs=out_specs, out_shape=outs,
        compiler_params=_cp(("arbitrary", "arbitrary"), 56), name="inproj",
    )(xall, mod, g, cos, sin, wu, wkv, wqg, wqn, wgt, qgain, kgain)


def _chunk_rows(u_ref):
    nk = u_ref.shape[2] // SSM_CHUNK
    cols = [u_ref[0, 0, pl.ds(j, nk, stride=SSM_CHUNK), :] for j in range(SSM_CHUNK)]
    return jnp.concatenate(cols, axis=1).astype(BF)


def _ssm_in_kernel(u_ref, m_ref, zf_ref, zr_ref):
    z = _dot(_chunk_rows(u_ref), m_ref[0])
    half = z.shape[1] // 2
    zf_ref[0] = z[:, :half]
    zr_ref[0] = z[:, half:]


def ssm_in(uq, m1):
    qt, b, s, _ = uq.shape
    nk = s // SSM_CHUNK
    k = SSM_CHUNK * LANES
    w = 2 * SSM_ST
    zspec = pl.BlockSpec((1, nk, w), lambda q, bi: (bi, 0, q))
    return pl.pallas_call(
        _ssm_in_kernel, grid=(qt, b),
        in_specs=[pl.BlockSpec((1, 1, s, LANES), lambda q, bi: (q, bi, 0, 0)),
                  pl.BlockSpec((1, k, 2 * w), lambda q, bi: (q, 0, 0))],
        out_specs=[zspec, zspec],
        out_shape=[jax.ShapeDtypeStruct((b, nk, qt * w), F32)] * 2,
        compiler_params=_cp(("arbitrary", "arbitrary")), name="ssm_in",
    )(uq, m1)


def _ssm_scan_kernel(zf_ref, zr_ref, af_ref, ar_ref, sf_ref, sr_ref, *, n_lat_chunks):
    nk = zf_ref.shape[1]
    st = SSM_ST
    af_re, af_im = af_ref[:, :st], af_ref[:, st:]
    ar_re, ar_im = ar_ref[:, :st], ar_ref[:, st:]

    def step(i, carry):
        fre, fim, rre, rim = carry
        kf = lax.rem(i + n_lat_chunks, nk)
        kr = nk - 1 - i
        sf_ref[0, pl.ds(kf, 1), :] = jnp.concatenate([fre, fim], axis=1)
        sr_ref[0, pl.ds(kr, 1), :] = jnp.concatenate([rre, rim], axis=1)
        zf = zf_ref[0, pl.ds(kf, 1), :]
        zr = zr_ref[0, pl.ds(kr, 1), :]
        nfre = af_re * fre - af_im * fim + zf[:, :st]
        nfim = af_re * fim + af_im * fre + zf[:, st:]
        nrre = ar_re * rre - ar_im * rim + zr[:, :st]
        nrim = ar_re * rim + ar_im * rre + zr[:, st:]
        return nfre, nfim, nrre, nrim

    z0 = jnp.zeros((1, st), F32)
    lax.fori_loop(0, nk, step, (z0, z0, z0, z0))


def ssm_scan(zf, zr, af, ar, n_lat_chunks):
    b, nk, n = zf.shape
    w = 2 * SSM_ST
    spec = pl.BlockSpec((1, nk, w), lambda bi, q: (bi, 0, q))
    aspec = pl.BlockSpec((1, w), lambda bi, q: (0, q))
    return pl.pallas_call(
        functools.partial(_ssm_scan_kernel, n_lat_chunks=n_lat_chunks),
        grid=(b, n // w), in_specs=[spec, spec, aspec, aspec], out_specs=[spec, spec],
        out_shape=[jax.ShapeDtypeStruct((b, nk, n), F32)] * 2,
        compiler_params=_cp(("arbitrary", "arbitrary")), name="ssm_scan",
    )(zf, zr, af, ar)


def _ssm_out_kernel(u_ref, sf_ref, sr_ref, mu_ref, mf_ref, mr_ref, y_ref):
    nk = sf_ref.shape[1]
    y = _dot(_chunk_rows(u_ref), mu_ref[0])
    y += _dot(sf_ref[0].astype(BF), mf_ref[0])
    y += _dot(sr_ref[0].astype(BF), mr_ref[0])
    for j in range(SSM_CHUNK):
        y_ref[0, 0, pl.ds(j, nk, stride=SSM_CHUNK), :] = y[:, j * LANES:(j + 1) * LANES]


def ssm_out(uq, sf, sr, m3u, m3f, m3r):
    qt, b, s, _ = uq.shape
    nk = s // SSM_CHUNK
    k = SSM_CHUNK * LANES
    w = 2 * SSM_ST
    one = pl.Buffered(1)
    uspec = pl.BlockSpec((1, 1, s, LANES), lambda q, bi: (q, bi, 0, 0))
    sspec = pl.BlockSpec((1, nk, w), lambda q, bi: (bi, 0, q))
    return pl.pallas_call(
        _ssm_out_kernel, grid=(qt, b),
        in_specs=[uspec, sspec, sspec,
                  pl.BlockSpec((1, k, k), lambda q, bi: (q, 0, 0), pipeline_mode=one),
                  pl.BlockSpec((1, w, k), lambda q, bi: (q, 0, 0), pipeline_mode=one),
                  pl.BlockSpec((1, w, k), lambda q, bi: (q, 0, 0), pipeline_mode=one)],
        out_specs=uspec,
        out_shape=jax.ShapeDtypeStruct((qt, b, s, LANES), F32),
        compiler_params=_cp(("arbitrary", "arbitrary")), name="ssm_out",
    )(uq, sf, sr, m3u, m3f, m3r)


def s5_matrices(a_re, a_im, b_re, b_im, c_re, c_im, log_dt, d_skip):
    tc, g, p, hh = SSM_CHUNK, SSM_GROUPS, SSM_STATE, SSM_GROUP
    lam = lax.complex(a_re.astype(F32), a_im.astype(F32))
    dt = jnp.exp(log_dt.astype(F32))[..., None]
    ldt = lam * dt
    a_bar = jnp.exp(ldt)
    b_bar = ((a_bar - 1.0) / lam)[..., None] * lax.complex(b_re.astype(F32), b_im.astype(F32))
    cm = lax.complex(c_re.astype(F32), c_im.astype(F32))
    steps = jnp.arange(tc + 1, dtype=F32)[:, None, None, None]
    pw = jnp.exp(ldt[None] * steps)
    pw_desc = jnp.exp(ldt[None] * (tc - steps))
    hi = lax.Precision.HIGHEST
    gpt = SSM_GPT

    def widen(compact, col_expand, row_div, col_div):
        rows, cols = compact.shape[1], col_expand.shape[1]
        wide = jnp.einsum('qrc,cn->qrn', compact.astype(BF), col_expand, preferred_element_type=F32)
        rg = (jnp.arange(rows) // row_div) % gpt
        cg = (jnp.arange(cols) // col_div) % gpt
        return jnp.where((rg[:, None] == cg[None, :])[None], wide, 0.0).astype(BF)

    x_tok = jnp.einsum('ab,m,cd->acbmd', jnp.eye(tc, dtype=F32), jnp.ones((gpt,), F32),
                       jnp.eye(hh, dtype=F32)).reshape(tc * hh, tc * LANES).astype(BF)
    x_st = jnp.einsum('ab,m,cd->acbmd', jnp.eye(4, dtype=F32), jnp.ones((gpt,), F32),
                      jnp.eye(p, dtype=F32)).reshape(4 * p, 4 * SSM_ST).astype(BF)

    kk = jnp.real(jnp.einsum('dghp,ldgp,dgpk->dlghk', cm, pw[:tc], b_bar))
    jj = jnp.arange(tc)[None, :, None]
    ii = jnp.arange(tc)[None, None, :]
    lag = jnp.arange(tc)[:, None, None]
    shift_f = (ii - jj == lag).astype(F32)
    shift_r = (jj - ii == lag).astype(F32)
    dd = (jnp.eye(tc, dtype=F32)[:, :, None, None, None] * jnp.eye(hh, dtype=F32)[None, None, None]
          * d_skip.astype(F32).reshape(g, hh)[None, None, :, :, None])
    t5 = (jnp.einsum('lji,lghk->jighk', shift_f, kk[0], precision=hi)
          + jnp.einsum('lji,lghk->jighk', shift_r, kk[1], precision=hi) + dd)
    tcomp = t5.reshape(tc, tc, SSM_QT, gpt, hh, hh).transpose(2, 0, 3, 5, 1, 4)
    m3u = widen(tcomp.reshape(SSM_QT, tc * LANES, tc * hh), x_tok, hh, hh)

    inf = pw_desc[1:, 0][..., None] * b_bar[0][None]
    inr = pw[:tc, 1][..., None] * b_bar[1][None]
    m1 = jnp.stack([jnp.real(inf), jnp.imag(inf), jnp.real(inr), jnp.imag(inr)], axis=0)
    m1 = m1.reshape(2, 2, tc, SSM_QT, gpt, p, hh).transpose(3, 2, 4, 6, 0, 1, 5)
    m1 = widen(m1.reshape(SSM_QT, tc * LANES, 4 * p), x_st, hh, p)

    def out_mat(d, powers):
        ca = cm[d][None] * powers[:, :, None, :]
        mm = jnp.stack([jnp.real(ca), -jnp.imag(ca)], axis=0)
        mm = mm.reshape(2, tc, SSM_QT, gpt, hh, p).transpose(2, 0, 3, 5, 1, 4)
        return widen(mm.reshape(SSM_QT, 2 * SSM_ST, tc * hh), x_tok, p, hh)

    m3f = out_mat(0, pw[1:, 0])
    m3r = out_mat(1, pw_desc[:tc, 1])

    def decay(d):
        a16 = pw[tc, d].reshape(SSM_QT, SSM_ST)
        return jnp.concatenate([jnp.real(a16), jnp.imag(a16)], axis=1).reshape(1, SSM_QT * 2 * SSM_ST)

    return m1.astype(BF), m3u.astype(BF), m3f.astype(BF), m3r.astype(BF), decay(0), decay(1)


def s5_branch(uq4, mats, n_lat_chunks):
    m1, m3u, m3f, m3r, a16f, a16r = mats
    zf, zr = ssm_in(uq4, m1)
    sf, sr = ssm_scan(zf, zr, a16f, a16r, n_lat_chunks)
    return ssm_out(uq4, sf, sr, m3u, m3f, m3r)


def _gqa_kernel(q_ref, k_ref, vt_ref, vtc_ref, o_ref, q4_ref, m_ref, l_ref, acc_ref, sa_ref, sb_ref, sctx_ref, *,
                n_lat, n_ctx, tk, n_lat_tiles):
    tq = q_ref.shape[1]
    qi = pl.program_id(2)
    for s in range(GQA_GROUP):
        q4_ref[s * tq:(s + 1) * tq, :] = q_ref[0, :, s * LANES:(s + 1) * LANES]
    m_ref[...] = jnp.full(m_ref.shape, NEG, F32)
    l_ref[...] = jnp.zeros(l_ref.shape, F32)
    acc_ref[...] = jnp.zeros(acc_ref.shape, F32)

    n_chunks = n_lat // tk

    def scores(kc, s):
        return _dot_nt(kc, q4_ref[s * tq:(s + 1) * tq, :])

    def lat_keys(c):
        return k_ref[0, pl.ds(pl.multiple_of(c * tk, tk), tk), :]

    def update(st, s, vtc):
        cs = slice(s * tq, (s + 1) * tq)
        m_old = m_ref[:, cs]
        m_new = jnp.maximum(m_old, jnp.max(st, axis=0, keepdims=True))
        alpha = jnp.exp2(m_old - m_new)
        p = jnp.exp2(st - m_new)
        l_ref[:, cs] = alpha * l_ref[:, cs] + jnp.sum(p, axis=0, keepdims=True)
        acc_ref[:, cs] = alpha * acc_ref[:, cs] + _dot(vtc, p.astype(BF))
        m_ref[:, cs] = m_new

    @pl.when(qi < n_lat_tiles)
    def _():
        bufs = (sa_ref, sb_ref)
        kc0 = lat_keys(0)
        for s in range(GQA_GROUP):
            sa_ref[s] = scores(kc0, s)

        def step(c, cur, nxt):
            kn = lat_keys(c + 1)
            for s in range(GQA_GROUP):
                nxt[s] = scores(kn, s)
                update(cur[s], s, vt_ref[0, c])

        def pair(i, carry):
            step(2 * i, sa_ref, sb_ref)
            step(2 * i + 1, sb_ref, sa_ref)
            return carry

        n_steps = n_chunks - 1
        lax.fori_loop(0, n_steps // 2, pair, 0)
        if n_steps % 2:
            step(n_steps - 1, sa_ref, sb_ref)
        last = bufs[n_steps % 2]
        kcc = k_ref[0, n_lat:n_lat + n_ctx, :]
        for s in range(GQA_GROUP):
            sctx_ref[s] = scores(kcc, s)
            update(last[s], s, vt_ref[0, n_chunks - 1])

    @pl.when(qi >= n_lat_tiles)
    def _():
        kcc = k_ref[0, n_lat:n_lat + n_ctx, :]
        for s in range(GQA_GROUP):
            sctx_ref[s] = scores(kcc, s)

    for s in range(GQA_GROUP):
        update(sctx_ref[s], s, vtc_ref[0])

    o = (acc_ref[...] / l_ref[...]).T
    for s in range(GQA_GROUP):
        o_ref[0, :, s * LANES:(s + 1) * LANES] = o[s * tq:(s + 1) * tq].astype(o_ref.dtype)


def gqa_attention(qg, kg, vg, n_lat, n_q_tiles):
    b, s, _ = qg.shape
    tq = TOKEN_TILE
    tk = 512
    gw = GQA_GROUP * LANES
    n_ctx = s - n_lat
    nc = n_lat // tk
    vt = jnp.swapaxes(vg[:, :n_lat].reshape(b, nc, tk, LANES), 2, 3)
    vtc = jnp.swapaxes(vg[:, n_lat:], 1, 2)
    kern = functools.partial(_gqa_kernel, n_lat=n_lat, n_ctx=n_ctx, tk=tk, n_lat_tiles=n_lat // tq)
    return pl.pallas_call(
        kern, grid=(b, GQA_KV_HEADS, n_q_tiles),
        in_specs=[pl.BlockSpec((1, tq, gw), lambda bi, g, i: (bi, i, g)),
                  pl.BlockSpec((1, s, LANES), lambda bi, g, i: (bi, 0, 0)),
                  pl.BlockSpec((1, nc, LANES, tk), lambda bi, g, i: (bi, 0, 0, 0)),
                  pl.BlockSpec((1, LANES, n_ctx), lambda bi, g, i: (bi, 0, 0))],
        out_specs=pl.BlockSpec((1, tq, gw), lambda bi, g, i: (bi, i, g)),
        out_shape=jax.ShapeDtypeStruct((b, n_q_tiles * tq, GQA_HEADS * LANES), BF),
        scratch_shapes=[pltpu.VMEM((GQA_GROUP * tq, LANES), BF),
                        pltpu.VMEM((1, GQA_GROUP * tq), F32),
                        pltpu.VMEM((1, GQA_GROUP * tq), F32),
                        pltpu.VMEM((LANES, GQA_GROUP * tq), F32),
                        pltpu.VMEM((GQA_GROUP, tk, tq), F32),
                        pltpu.VMEM((GQA_GROUP, tk, tq), F32),
                        pltpu.VMEM((GQA_GROUP, n_ctx, tq), F32)],
        compiler_params=_cp(("arbitrary", "arbitrary", "arbitrary")), name="gqa",
    )(qg, kg, vt, vtc)


def _na_kernel(q_ref, k_ref, v_ref, tab_ref, o_ref, *, n_lat, n_ctx, rows):
    rb = pl.program_id(1)
    ws = jnp.clip(rb * NA_QROWS - NA_WIN_R // 2, 0, rows - NA_UNION)
    off = pl.multiple_of(ws * GRID_W, GRID_W)
    nwin = NA_UNION * GRID_W
    def window_and_context(ref, pair):
        ls = slice(pair * LANES, (pair + 1) * LANES)
        return jnp.concatenate([ref[0, pl.ds(off, nwin), ls], ref[0, n_lat:n_lat + n_ctx, ls]], axis=0)

    def scores(h):
        kcat = window_and_context(k_ref, h // 2)
        return _dot_nt(q_ref[0, :, h * LANES:(h + 1) * LANES], kcat) + tab_ref[0, h]

    s_next = scores(0)
    for h in range(NA_HEADS):
        s = s_next
        if h + 1 < NA_HEADS:
            s_next = scores(h + 1)
        m = jnp.max(s, axis=-1, keepdims=True)
        p = jnp.exp2(s - m)
        l = jnp.sum(p, axis=-1, keepdims=True)
        o = _dot(p.astype(BF), window_and_context(v_ref, h // 2)) / l
        o_ref[0, :, h * LANES:(h + 1) * LANES] = o.astype(o_ref.dtype)


def na_attention(qn, kn, vn, table, n_lat, n_q_tiles):
    b, s, _ = qn.shape
    tq = NA_QROWS * GRID_W
    rows = n_lat // GRID_W
    n_lat_tiles = n_lat // tq
    tw = table.shape[-1]

    def tab_idx(bi, rb):
        case = jnp.where(rb == 0, 0, jnp.where(rb < n_lat_tiles - 1, 1, jnp.where(rb == n_lat_tiles - 1, 2, 3)))
        return (case, 0, 0, 0)

    kern = functools.partial(_na_kernel, n_lat=n_lat, n_ctx=s - n_lat, rows=rows)
    return pl.pallas_call(
        kern, grid=(b, n_q_tiles),
        in_specs=[pl.BlockSpec((1, tq, NA_HEADS * LANES), lambda bi, rb: (bi, rb, 0)),
                  pl.BlockSpec((1, s, NA_W), lambda bi, rb: (bi, 0, 0)),
                  pl.BlockSpec((1, s, NA_W), lambda bi, rb: (bi, 0, 0)),
                  pl.BlockSpec((1, NA_HEADS, tq, tw), tab_idx)],
        out_specs=pl.BlockSpec((1, tq, NA_HEADS * LANES), lambda bi, rb: (bi, rb, 0)),
        out_shape=jax.ShapeDtypeStruct((b, n_q_tiles * tq, NA_HEADS * LANES), BF),
        compiler_params=_cp(("arbitrary", "arbitrary"), 56), name="na",
    )(qn, kn, vn, table)


def na_table(rpb, rows, n_ctx):
    tq = NA_QROWS * GRID_W
    qr = jnp.arange(NA_QROWS)[:, None, None, None]
    qc = jnp.arange(GRID_W)[None, :, None, None]
    kr = jnp.arange(NA_UNION)[None, None, :, None]
    kc = jnp.arange(GRID_W)[None, None, None, :]
    cs = jnp.clip(qc - NA_WIN_C // 2, 0, GRID_W - NA_WIN_C)
    col_ok = (kc >= cs) & (kc < cs + NA_WIN_C)
    col_idx = (kc - qc + NA_WIN_C - 1)[:, :, 0, :]
    col_hot = (col_idx[0, :, :, None] == jnp.arange(2 * NA_WIN_C - 1)).astype(F32)
    n_blocks = rows // NA_QROWS
    hi = lax.Precision.HIGHEST
    tabs = []
    for blk in (0, 1, n_blocks - 1):
        r = blk * NA_QROWS + qr
        ws = min(max(blk * NA_QROWS - NA_WIN_R // 2, 0), rows - NA_UNION)
        rs = jnp.clip(r - NA_WIN_R // 2, 0, rows - NA_WIN_R)
        ka = ws + kr
        ok = (ka >= rs) & (ka < rs + NA_WIN_R) & col_ok
        row_idx = (ka - r + NA_WIN_R - 1)[:, 0, :, 0]
        row_hot = (row_idx[:, :, None] == jnp.arange(2 * NA_WIN_R - 1)).astype(F32)
        by_row = jnp.einsum('hab,rka->hrkb', rpb.astype(F32), row_hot, precision=hi)
        bias = jnp.einsum('hrkb,qcb->hrqkc', by_row, col_hot, precision=hi)
        tabs.append(jnp.where(ok[None], bias, NEG).reshape(NA_HEADS, tq, NA_UNION * GRID_W))
    tabs.append(jnp.full((NA_HEADS, tq, NA_UNION * GRID_W), NEG, F32))
    win = jnp.stack(tabs, axis=0)
    return jnp.concatenate([win * LOG2E, jnp.zeros(win.shape[:3] + (n_ctx,), F32)], axis=-1)


def _merge_kernel(x_ref, y_ref, og_ref, on_ref, gt_ref, mod_ref, gw_ref, gb_ref, ws_ref, wg_ref, wn_ref, wo_ref,
                  o_ref):
    d = x_ref.shape[-1]
    y = jnp.concatenate([y_ref[q, 0].astype(F32) for q in range(SSM_QT)], axis=1)
    gy = _gelu_tanh(y)
    ys = gy * jax.nn.sigmoid(_dot(gy.astype(BF), gw_ref[...]) + gb_ref[...])
    gt = gt_ref[0]
    m = gt[:, :d].astype(F32) * _dot(ys.astype(BF), ws_ref[...])
    m += gt[:, d:2 * d].astype(F32) * _dot(og_ref[0], wg_ref[...])
    m += gt[:, 2 * d:].astype(F32) * _dot(on_ref[0], wn_ref[...])
    o_ref[0] = x_ref[0] + mod_ref[0][2:3] * _dot(m.astype(BF), wo_ref[...])


def merge(xall, y4, og, on, gt, mod, gw, gb, ws, wg, wn, wo, n_tiles, n_lat_tiles):
    b, s, d = xall.shape
    tm = TOKEN_TILE
    nb = mod.shape[0] - 1

    def row(bi, j):
        return (bi, j, 0)

    return pl.pallas_call(
        _merge_kernel, grid=(b, n_tiles),
        in_specs=[pl.BlockSpec((1, tm, d), row),
                  pl.BlockSpec((SSM_QT, 1, tm, LANES), lambda bi, j: (0, bi, j, 0)),
                  pl.BlockSpec((1, tm, GQA_HEADS * LANES), row),
                  pl.BlockSpec((1, tm, NA_HEADS * LANES), row),
                  pl.BlockSpec((1, tm, 3 * d), row),
                  pl.BlockSpec((1, 6, d), lambda bi, j: (jnp.where(j < n_lat_tiles, bi, nb), 0, 0)),
                  _const_spec(gw.shape), _const_spec(gb.shape), _const_spec(ws.shape),
                  _const_spec(wg.shape), _const_spec(wn.shape), _const_spec(wo.shape)],
        out_specs=pl.BlockSpec((1, tm, d), row),
        out_shape=jax.ShapeDtypeStruct((b, n_tiles * tm, d), F32),
        compiler_params=_cp(("arbitrary", "arbitrary")), name="merge",
    )(xall, y4, og, on, gt, mod, gw, gb, ws, wg, wn, wo)


def _ffn_kernel(x_ref, mod_ref, g_ref, wg_ref, wu_ref, wd_ref, o_ref):
    x = x_ref[0]
    mod = mod_ref[0]
    hb = (_rms(x, g_ref[...]) * (1.0 + mod[4:5]) + mod[3:4]).astype(BF)
    a = _silu(_dot(hb, wg_ref[...])) * _dot(hb, wu_ref[...])
    o_ref[0] = x + mod[5:6] * _dot(a.astype(BF), wd_ref[...])


def ffn(xall, mod, g, wg, wu, wd, n_lat_tiles):
    b, s, d = xall.shape
    tm = TOKEN_TILE
    nb = mod.shape[0] - 1

    def row(bi, j):
        return (bi, j, 0)

    return pl.pallas_call(
        _ffn_kernel, grid=(b, s // tm),
        in_specs=[pl.BlockSpec((1, tm, d), row),
                  pl.BlockSpec((1, 6, d), lambda bi, j: (jnp.where(j < n_lat_tiles, bi, nb), 0, 0)),
                  _const_spec(g.shape), _const_spec(wg.shape), _const_spec(wu.shape), _const_spec(wd.shape)],
        out_specs=pl.BlockSpec((1, tm, d), row),
        out_shape=jax.ShapeDtypeStruct((b, s, d), F32),
        compiler_params=_cp(("arbitrary", "arbitrary"), 56), name="ffn",
    )(xall, mod, g, wg, wu, wd)


def _route_kernel(x_ref, mod_ref, g_ref, rw_ref, h_ref, info_ref):
    x = x_ref[0]
    mod = mod_ref[0]
    h = _rms(x, g_ref[...]) * (1.0 + mod[4:5]) + mod[3:4]
    h_ref[0] = h
    logits = jnp.dot(h, rw_ref[...], preferred_element_type=F32, precision=lax.Precision.HIGHEST)
    lane = lax.broadcasted_iota(jnp.int32, logits.shape, 1)
    lanef = lane.astype(F32)
    logits = jnp.where(lane < N_EXPERTS, logits, -jnp.inf)
    m1 = jnp.max(logits, axis=-1, keepdims=True)
    i1 = jnp.min(jnp.where(logits == m1, lanef, float(LANES)), axis=-1, keepdims=True)
    rest = jnp.where(lanef == i1, -jnp.inf, logits)
    m2 = jnp.max(rest, axis=-1, keepdims=True)
    i2 = jnp.min(jnp.where(rest == m2, lanef, float(LANES)), axis=-1, keepdims=True)
    e2 = jnp.exp(m2 - m1)
    w1 = 1.0 / (1.0 + e2)
    w2 = e2 / (1.0 + e2)
    info_ref[0] = jnp.where(lane == 0, i1, jnp.where(lane == 1, i2, jnp.where(lane == 2, w1,
                            jnp.where(lane == 3, w2, 0.0))))


def moe_route(xall, mod, g, rw, n_tiles):
    b, s, d = xall.shape
    tm = TOKEN_TILE

    def row(bi, j):
        return (bi, j, 0)

    return pl.pallas_call(
        _route_kernel, grid=(b, n_tiles),
        in_specs=[pl.BlockSpec((1, tm, d), row), pl.BlockSpec((1, 6, d), lambda bi, j: (bi, 0, 0)),
                  _const_spec(g.shape), _const_spec(rw.shape)],
        out_specs=[pl.BlockSpec((1, tm, d), row), pl.BlockSpec((1, tm, LANES), row)],
        out_shape=[jax.ShapeDtypeStruct((b, n_tiles * tm, d), F32),
                   jax.ShapeDtypeStruct((b, n_tiles * tm, LANES), F32)],
        compiler_params=_cp(("arbitrary", "arbitrary")), name="moe_route",
    )(xall, mod, g, rw)


def _dispatch_kernel(pos_ref, h_ref, xs_in_ref, xs_ref, sem):
    del xs_in_ref
    tm = h_ref.shape[0]

    def row_copy(r, k):
        return pltpu.make_async_copy(h_ref.at[pl.ds(r, 1)], xs_ref.at[pl.ds(pos_ref[0, 0, 2 * r + k], 1)], sem)

    def issue(r, c):
        row_copy(r, 0).start()
        row_copy(r, 1).start()
        return c

    def drain(r, c):
        row_copy(r, 0).wait()
        row_copy(r, 1).wait()
        return c

    lax.fori_loop(0, tm, issue, 0, unroll=4)
    lax.fori_loop(0, tm, drain, 0)


def moe_dispatch(h2, pos, n_slots):
    t, d = h2.shape
    tm = TOKEN_TILE
    zeros = jnp.zeros((n_slots, d), F32)
    return pl.pallas_call(
        _dispatch_kernel, grid=(t // tm,),
        in_specs=[pl.BlockSpec((1, 1, 2 * tm), lambda i: (i, 0, 0), memory_space=pltpu.SMEM),
                  pl.BlockSpec((tm, d), lambda i: (i, 0)),
                  pl.BlockSpec(memory_space=pl.ANY)],
        out_specs=pl.BlockSpec(memory_space=pl.ANY),
        out_shape=jax.ShapeDtypeStruct((n_slots, d), F32),
        scratch_shapes=[pltpu.SemaphoreType.DMA(())],
        input_output_aliases={2: 0},
        compiler_params=_cp(("arbitrary",)), name="moe_dispatch",
    )(pos, h2, zeros)


def _experts_kernel(te_ref, nv_ref, x_ref, wg_ref, wu_ref, wd_ref, o_ref, acc_ref):
    i = pl.program_id(0)
    j = pl.program_id(1)

    @pl.when(i < nv_ref[0])
    def _():
        xb = x_ref[...].astype(BF)
        a = _silu(_dot(xb, wg_ref[0])) * _dot(xb, wu_ref[0])
        part = _dot(a.astype(BF), wd_ref[0])

        @pl.when(j == 0)
        def _():
            acc_ref[...] = part

        @pl.when(j > 0)
        def _():
            acc_ref[...] += part

        @pl.when(j == pl.num_programs(1) - 1)
        def _():
            o_ref[...] = acc_ref[...]

    @pl.when((i >= nv_ref[0]) & (j == pl.num_programs(1) - 1))
    def _():
        o_ref[...] = jnp.zeros(o_ref.shape, o_ref.dtype)


def moe_experts(xs, tile_expert, n_valid, wg, wu, wd):
    n_slots, d = xs.shape
    tm = MOE_TILE
    tf = MOE_FCHUNK
    f = wg.shape[-1]
    nf = f // tf

    def xrow(i, j, te, nv):
        return (jnp.minimum(i, nv[0] - 1), 0)

    def fcol(i, j, nv):
        return jnp.where(i < nv[0], j, nf - 1)

    grid_spec = pltpu.PrefetchScalarGridSpec(
        num_scalar_prefetch=2, grid=(n_slots // tm, nf),
        in_specs=[pl.BlockSpec((tm, d), xrow),
                  pl.BlockSpec((1, d, tf), lambda i, j, te, nv: (te[i], 0, fcol(i, j, nv))),
                  pl.BlockSpec((1, d, tf), lambda i, j, te, nv: (te[i], 0, fcol(i, j, nv))),
                  pl.BlockSpec((1, tf, d), lambda i, j, te, nv: (te[i], fcol(i, j, nv), 0))],
        out_specs=pl.BlockSpec((tm, d), lambda i, j, te, nv: (i, 0)),
        scratch_shapes=[pltpu.VMEM((tm, d), F32)])
    return pl.pallas_call(
        _experts_kernel, grid_spec=grid_spec,
        out_shape=jax.ShapeDtypeStruct((n_slots, d), F32),
        compiler_params=_cp(("arbitrary", "arbitrary"), 56), name="moe_experts",
    )(tile_expert, n_valid, xs, wg, wu, wd)


def _combine_kernel(pos_ref, x_ref, info_ref, mod_ref, fg_ref, ys_ref, o_ref, y1_ref, y2_ref, sem):
    tm = x_ref.shape[1]

    def row_copy(r, k, dst):
        return pltpu.make_async_copy(ys_ref.at[pl.ds(pos_ref[0, 0, 2 * r + k], 1)], dst.at[pl.ds(r, 1)], sem)

    def issue(r, c):
        row_copy(r, 0, y1_ref).start()
        row_copy(r, 1, y2_ref).start()
        return c

    def drain(r, c):
        row_copy(r, 0, y1_ref).wait()
        row_copy(r, 1, y2_ref).wait()
        return c

    lax.fori_loop(0, tm, issue, 0, unroll=4)
    lax.fori_loop(0, tm, drain, 0)
    info = info_ref[0]
    y = info[:, 2:3] * y1_ref[...] + info[:, 3:4] * y2_ref[...]
    xn = x_ref[0] + mod_ref[0][5:6] * y
    o_ref[0] = _rms(xn, fg_ref[...])


def moe_combine(x, info, mod, fg, ys, pos, n_tiles):
    b, s, d = x.shape
    tm = TOKEN_TILE

    def row(bi, j):
        return (bi, j, 0)

    return pl.pallas_call(
        _combine_kernel, grid=(b, n_tiles),
        in_specs=[pl.BlockSpec((1, 1, 2 * tm), lambda bi, j: (bi * n_tiles + j, 0, 0), memory_space=pltpu.SMEM),
                  pl.BlockSpec((1, tm, d), row), pl.BlockSpec((1, tm, LANES), row),
                  pl.BlockSpec((1, 6, d), lambda bi, j: (bi, 0, 0)), _const_spec(fg.shape),
                  pl.BlockSpec(memory_space=pl.ANY)],
        out_specs=pl.BlockSpec((1, tm, d), row),
        out_shape=jax.ShapeDtypeStruct((b, n_tiles * tm, d), F32),
        scratch_shapes=[pltpu.VMEM((tm, d), F32), pltpu.VMEM((tm, d), F32), pltpu.SemaphoreType.DMA(())],
        compiler_params=_cp(("arbitrary", "arbitrary")), name="moe_combine",
    )(pos, x, info, mod, fg, ys)


def moe_layer(x, mod, g2, rw, wg, wu, wd, fg, n_lat):
    b, s, d = x.shape
    n_tiles = n_lat // TOKEN_TILE
    t = b * n_lat
    rw_pad = jnp.zeros((d, LANES), F32).at[:, :N_EXPERTS].set(rw.astype(F32))
    h2, info = moe_route(x, mod, g2, rw_pad, n_tiles)
    info2 = info.reshape(t, LANES)

    e_pair = info2[:, :2].astype(jnp.int32).reshape(2 * t)
    onehot = (e_pair[:, None] == jnp.arange(N_EXPERTS)[None, :]).astype(jnp.int32)
    csum = jnp.cumsum(onehot, axis=0)
    rank = jnp.sum((csum - onehot) * onehot, axis=1)
    counts = csum[-1]
    tiles_e = (counts + MOE_TILE - 1) // MOE_TILE
    tile_end = jnp.cumsum(tiles_e)
    slot_off = (tile_end - tiles_e) * MOE_TILE
    pos = (slot_off[e_pair] + rank).astype(jnp.int32).reshape(t // TOKEN_TILE, 1, 2 * TOKEN_TILE)
    n_tiles_max = (2 * t) // MOE_TILE + N_EXPERTS
    n_valid = tile_end[-1:].astype(jnp.int32)
    tile_ids = jnp.minimum(jnp.arange(n_tiles_max), n_valid[0] - 1)
    tile_expert = jnp.sum((tile_ids[:, None] >= tile_end[None, :]).astype(jnp.int32), axis=1).astype(jnp.int32)

    xs = moe_dispatch(h2.reshape(t, d), pos, n_tiles_max * MOE_TILE)
    ys = moe_experts(xs, tile_expert, n_valid, wg, wu, wd)
    return moe_combine(x, info, mod, fg, ys, pos, n_tiles)


def _rope_tables(n_lat, n_ctx):
    t = jnp.arange(n_lat)
    pos = jnp.stack([t // GRID_W, t % GRID_W], axis=-1).astype(F32)
    half = HEAD_DIM // 2
    inv = 1.0 / (ROPE_BASE ** (jnp.arange(0, half, 2, dtype=F32) / half))
    ang = pos[:, :, None] * inv
    ang = jnp.concatenate([ang, ang], axis=-1).reshape(n_lat, HEAD_DIM)
    cos = jnp.concatenate([jnp.cos(ang), jnp.ones((n_ctx, HEAD_DIM), F32)], axis=0)
    sin = jnp.concatenate([jnp.sin(ang), jnp.zeros((n_ctx, HEAD_DIM), F32)], axis=0)
    return jnp.tile(cos, (1, 2)), jnp.tile(sin, (1, 2))


def _slot_offsets(kind):
    if kind == "gqa":
        return [(h // GQA_GROUP) * HEAD_DIM for h in range(GQA_HEADS)]
    return [(h % 2) * HEAD_DIM for h in range(NA_HEADS)]


def _pad_cols(w, kind):
    d = w.shape[0]
    w3 = w.reshape(d, GQA_HEADS, HEAD_DIM)
    both = jnp.concatenate([w3, w3], axis=-1)
    half = jnp.arange(LANES)[None, :] // HEAD_DIM
    keep = half == (jnp.asarray(_slot_offsets(kind)) // HEAD_DIM)[:, None]
    return jnp.where(keep[None], both, jnp.zeros((), w.dtype)).reshape(d, GQA_HEADS * LANES)


def _pad_rows(w, kind):
    return _pad_cols(w.T, kind).T


def kernel(x, c, ctx, c_ctx, w_mod, b_mod, norm1_g, w_in, ssm_a_re, ssm_a_im, ssm_b_re, ssm_b_im, ssm_c_re,
           ssm_c_im, ssm_log_dt, ssm_d, glu_w, glu_b, q_norm_g, k_norm_g, na_rpb, w_branch_ssm, w_branch_gqa,
           w_branch_na, w_out, norm2_g, ffn_w_gate, ffn_w_up, ffn_w_down, router_w, moe_w_gate, moe_w_up,
           moe_w_down, final_norm_g):
    b, n_lat, d = x.shape
    n_ctx = ctx.shape[1]
    s = n_lat + n_ctx
    depth = w_mod.shape[0]
    assert d == D_MODEL and n_lat % (NA_QROWS * GRID_W) == 0 and n_ctx == TOKEN_TILE and n_lat % 512 == 0
    assert depth == 2 and n_lat // GRID_W >= NA_UNION
    n_lat_tiles = n_lat // TOKEN_TILE
    n_all_tiles = s // TOKEN_TILE

    n_rows = -(-(b + 1) // 8) * 8
    cv = jnp.zeros((n_rows, d), F32).at[:b].set(c).at[b].set(c_ctx)
    mods = adaln(cv, w_mod, b_mod)[:, :b + 1].reshape(depth, b + 1, 6, d)

    cos, sin = _rope_tables(n_lat, n_ctx)
    table_shape_rows = n_lat // GRID_W
    xall = jnp.concatenate([x, ctx], axis=1)

    out = None
    for i in range(depth):
        last = i == depth - 1
        wi = w_in[i]
        c0 = SSM_WIDTH
        wu = wi[:, :c0].astype(BF)
        wkv = wi[:, c0:KV_COLS].astype(BF)
        wqg = _pad_cols(wi[:, KV_COLS:KV_COLS + GQA_Q_W], "gqa").astype(BF)
        wqn = _pad_cols(wi[:, KV_COLS + GQA_Q_W:KV_COLS + GQA_Q_W + NA_W], "na").astype(BF)
        wgt = wi[:, KV_COLS + GQA_Q_W + NA_W:].astype(BF)
        qgain = _pad_cols(jnp.tile(q_norm_g[i].astype(F32), GQA_HEADS)[None, :], "gqa")
        kgain = jnp.tile(k_norm_g[i].astype(F32), GQA_KV_HEADS)[None, :]

        uq, kg, vg, kn, vn, qg, qn, gt = inproj(xall, mods[i], norm1_g[i][None, :], cos, sin, wu, wkv, wqg, wqn,
                                                wgt, qgain, kgain, n_lat_tiles)

        mats = s5_matrices(ssm_a_re[i], ssm_a_im[i], ssm_b_re[i], ssm_b_im[i], ssm_c_re[i], ssm_c_im[i],
                           ssm_log_dt[i], ssm_d[i])
        y4 = s5_branch(uq, mats, n_lat // SSM_CHUNK)

        n_q_tiles = n_lat_tiles if last else n_all_tiles
        og = gqa_attention(qg, kg, vg, n_lat, n_q_tiles)
        on = na_attention(qn, kn, vn, na_table(na_rpb[i], table_shape_rows, n_ctx), n_lat, n_q_tiles)

        xall = merge(xall, y4, og, on, gt, mods[i], glu_w[i].astype(BF), glu_b[i][None, :].astype(F32),
                     w_branch_ssm[i].astype(BF), _pad_rows(w_branch_gqa[i], "gqa").astype(BF),
                     _pad_rows(w_branch_na[i], "na").astype(BF), w_out[i].astype(BF), n_q_tiles, n_lat_tiles)

        j = i // 2
        if i % 2 == 0:
            xall = ffn(xall, mods[i], norm2_g[i][None, :], ffn_w_gate[j].astype(BF), ffn_w_up[j].astype(BF),
                       ffn_w_down[j].astype(BF), n_lat_tiles)
        else:
            out = moe_layer(xall, mods[i], norm2_g[i][None, :], router_w[j], moe_w_gate[j].astype(BF),
                            moe_w_up[j].astype(BF), moe_w_down[j].astype(BF), final_norm_g[None, :], n_lat)
    return out
```

```python
import functools
import math

import jax
import jax.numpy as jnp
from jax import lax
from jax.experimental import pallas as pl
from jax.experimental.pallas import tpu as pltpu

D_MODEL = 1024
GRID_W = 64
SSM_WIDTH = 512
SSM_GROUP = 16
SSM_GROUPS = SSM_WIDTH // SSM_GROUP
SSM_STATE = 64
HEAD_DIM = 64
GQA_HEADS = 8
GQA_KV_HEADS = 2
GQA_GROUP = GQA_HEADS // GQA_KV_HEADS
ROPE_BASE = 10000.0
NA_HEADS = 8
NA_WIN_R = 8
NA_WIN_C = 16
ATTN_SCALE = HEAD_DIM ** -0.5
LOG2E = math.log2(math.e)
GQA_Q_W = GQA_HEADS * HEAD_DIM
GQA_KV_W = GQA_KV_HEADS * HEAD_DIM
NA_W = NA_HEADS * HEAD_DIM
KV_COLS = SSM_WIDTH + 2 * GQA_KV_W + 2 * NA_W
N_EXPERTS = 8
NORM_EPS = 1e-6

LANES = 128
TOKEN_TILE = 256
SSM_CHUNK = 16
SSM_QT = SSM_WIDTH // LANES
SSM_GPT = LANES // SSM_GROUP
SSM_ST = SSM_GPT * SSM_STATE
NA_QROWS = 4
NA_UNION = 12
MOE_TILE = 512
MOE_FCHUNK = 1792
NEG = -1e30

BF = jnp.bfloat16
F32 = jnp.float32


def _cp(sem, vmem_mb=48):
    return pltpu.CompilerParams(dimension_semantics=sem, vmem_limit_bytes=vmem_mb * 1024 * 1024)


def _const_spec(shape):
    nd = len(shape)
    return pl.BlockSpec(shape, lambda *_: (0,) * nd, pipeline_mode=pl.Buffered(1))


def _dot(a, b):
    return jnp.dot(a, b, preferred_element_type=F32)


def _dot_nt(a, b):
    return lax.dot_general(a, b, (((1,), (1,)), ((), ())), preferred_element_type=F32)


def _rms(x, g):
    return x * lax.rsqrt(jnp.mean(x * x, axis=-1, keepdims=True) + NORM_EPS) * g


def _silu(x):
    return x * jax.nn.sigmoid(x)


def _gelu_tanh(x):
    return 0.5 * x * (1.0 + jnp.tanh(math.sqrt(2.0 / math.pi) * (x + 0.044715 * x * x * x)))


def _adaln_kernel(cv_ref, w_ref, b_ref, o_ref):
    a = _silu(cv_ref[...])
    w = w_ref[0]
    a_hi = a.astype(BF)
    a_lo = (a - a_hi.astype(F32)).astype(BF)
    w_hi = w.astype(BF)
    w_lo = (w - w_hi.astype(F32)).astype(BF)
    o_ref[0] = _dot(a_hi, w_hi) + _dot(a_hi, w_lo) + _dot(a_lo, w_hi) + b_ref[0]


def adaln(cv, w_mod, b_mod):
    depth, d, n = w_mod.shape
    r = cv.shape[0]
    tn = 512
    return pl.pallas_call(
        _adaln_kernel,
        grid=(depth, n // tn),
        in_specs=[pl.BlockSpec((r, d), lambda i, j: (0, 0)),
                  pl.BlockSpec((1, d, tn), lambda i, j: (i, 0, j)),
                  pl.BlockSpec((1, 1, tn), lambda i, j: (i, 0, j))],
        out_specs=pl.BlockSpec((1, r, tn), lambda i, j: (i, 0, j)),
        out_shape=jax.ShapeDtypeStruct((depth, r, n), F32),
        compiler_params=_cp(("arbitrary", "arbitrary")),
        name="adaln",
    )(cv, w_mod, b_mod.reshape(depth, 1, n))


def _rope(xn, cos, sin):
    lane = lax.broadcasted_iota(jnp.int32, xn.shape, 1)
    first = (lane % 32) < 16
    rot = jnp.where(first, -pltpu.roll(xn, LANES - 16, 1), pltpu.roll(xn, 16, 1))
    return xn * cos + rot * sin


def _inproj_kernel(x_ref, mod_ref, g_ref, cos_ref, sin_ref, wu_ref, wkv_ref, wqg_ref, wqn_ref, wgt_ref,
                   qgain_ref, kgain_ref,
                   u_ref, kg_ref, vg_ref, kn_ref, vn_ref, qg_ref, qn_ref, gt_ref):
    x = x_ref[0]
    mod = mod_ref[0]
    h = _rms(x, g_ref[...]) * (1.0 + mod[1:2]) + mod[0:1]
    hb = h.astype(BF)
    cos = cos_ref[...]
    sin = sin_ref[...]

    u = _dot(hb, wu_ref[...])
    for q in range(SSM_QT):
        u_ref[q, 0] = u[:, q * LANES:(q + 1) * LANES]

    def head_pair_norm_rope(t, gain):
        lane = lax.broadcasted_iota(jnp.int32, t.shape, 1)
        lo = lane < HEAD_DIM
        sq = t * t
        ms_lo = jnp.sum(jnp.where(lo, sq, 0.0), axis=-1, keepdims=True)
        ms_hi = jnp.sum(jnp.where(lo, 0.0, sq), axis=-1, keepdims=True)
        ms = jnp.where(lo, ms_lo, ms_hi) * (1.0 / HEAD_DIM)
        return _rope(t * lax.rsqrt(ms + NORM_EPS) * gain, cos, sin)

    kv = _dot(hb, wkv_ref[...])
    kg_ref[0] = head_pair_norm_rope(kv[:, :LANES], kgain_ref[...]).astype(BF)
    vg_ref[0] = kv[:, LANES:2 * LANES].astype(BF)
    kn_ref[0] = kv[:, 2 * LANES:2 * LANES + NA_W].astype(BF)
    vn_ref[0] = kv[:, 2 * LANES + NA_W:].astype(BF)

    qg = _dot(hb, wqg_ref[...])
    for t in range(GQA_Q_W // LANES):
        ts = slice(t * LANES, (t + 1) * LANES)
        qg_ref[0, :, ts] = (head_pair_norm_rope(qg[:, ts], qgain_ref[...]) * (ATTN_SCALE * LOG2E)).astype(BF)

    qn_ref[0] = (_dot(hb, wqn_ref[...]) * (ATTN_SCALE * LOG2E)).astype(BF)
    gt_ref[0] = jax.nn.sigmoid(_dot(hb, wgt_ref[...])).astype(BF)


def inproj(xall, mod, g, cos, sin, wu, wkv, wqg, wqn, wgt, qgain, kgain, n_lat_tiles):
    b, s, d = xall.shape
    tm = TOKEN_TILE
    nt = s // tm
    nb = mod.shape[0] - 1

    def row(bi, j):
        return (bi, j, 0)

    def modrow(bi, j):
        return (jnp.where(j < n_lat_tiles, bi, nb), 0, 0)

    outs = [
        jax.ShapeDtypeStruct((SSM_QT, b, s, LANES), F32),
        jax.ShapeDtypeStruct((b, s, LANES), BF),
        jax.ShapeDtypeStruct((b, s, LANES), BF),
        jax.ShapeDtypeStruct((b, s, NA_W), BF),
        jax.ShapeDtypeStruct((b, s, NA_W), BF),
        jax.ShapeDtypeStruct((b, s, GQA_Q_W), BF),
        jax.ShapeDtypeStruct((b, s, NA_W), BF),
        jax.ShapeDtypeStruct((b, s, 3 * d), BF),
    ]
    out_specs = [
        pl.BlockSpec((SSM_QT, 1, tm, LANES), lambda bi, j: (0, bi, j, 0)),
        pl.BlockSpec((1, tm, LANES), row),
        pl.BlockSpec((1, tm, LANES), row),
        pl.BlockSpec((1, tm, NA_W), row),
        pl.BlockSpec((1, tm, NA_W), row),
        pl.BlockSpec((1, tm, GQA_Q_W), row),
        pl.BlockSpec((1, tm, NA_W), row),
        pl.BlockSpec((1, tm, 3 * d), row),
    ]
    in_specs = [
        pl.BlockSpec((1, tm, d), row),
        pl.BlockSpec((1, 6, d), modrow),
        _const_spec(g.shape),
        pl.BlockSpec((tm, LANES), lambda bi, j: (j, 0)),
        pl.BlockSpec((tm, LANES), lambda bi, j: (j, 0)),
        _const_spec(wu.shape), _const_spec(wkv.shape), _const_spec(wqg.shape),
        _const_spec(wqn.shape), _const_spec(wgt.shape), _const_spec(qgain.shape), _const_spec(kgain.shape),
    ]
    return pl.pallas_call(
        _inproj_kernel, grid=(b, nt), in_specs=in_specs, out_specs=out_specs, out_shape=outs,
        compiler_params=_cp(("arbitrary", "arbitrary"), 56), name="inproj",
    )(xall, mod, g, cos, sin, wu, wkv, wqg, wqn, wgt, qgain, kgain)


def _chunk_rows(u_ref):
    nk = u_ref.shape[2] // SSM_CHUNK
    cols = [u_ref[0, 0, pl.ds(j, nk, stride=SSM_CHUNK), :] for j in range(SSM_CHUNK)]
    return jnp.concatenate(cols, axis=1).astype(BF)


def _ssm_in_kernel(u_ref, m_ref, zf_ref, zr_ref):
    z = _dot(_chunk_rows(u_ref), m_ref[0])
    half = z.shape[1] // 2
    zf_ref[0] = z[:, :half]
    zr_ref[0] = z[:, half:]


def ssm_in(uq, m1):
    qt, b, s, _ = uq.shape
    nk = s // SSM_CHUNK
    k = SSM_CHUNK * LANES
    w = 2 * SSM_ST
    zspec = pl.BlockSpec((1, nk, w), lambda q, bi: (bi, 0, q))
    return pl.pallas_call(
        _ssm_in_kernel, grid=(qt, b),
        in_specs=[pl.BlockSpec((1, 1, s, LANES), lambda q, bi: (q, bi, 0, 0)),
                  pl.BlockSpec((1, k, 2 * w), lambda q, bi: (q, 0, 0))],
        out_specs=[zspec, zspec],
        out_shape=[jax.ShapeDtypeStruct((b, nk, qt * w), F32)] * 2,
        compiler_params=_cp(("arbitrary", "arbitrary")), name="ssm_in",
    )(uq, m1)


def _ssm_scan_kernel(zf_ref, zr_ref, af_ref, ar_ref, sf_ref, sr_ref, *, n_lat_chunks):
    nk = zf_ref.shape[1]
    st = SSM_ST
    af_re, af_im = af_ref[:, :st], af_ref[:, st:]
    ar_re, ar_im = ar_ref[:, :st], ar_ref[:, st:]

    def step(i, carry):
        fre, fim, rre, rim = carry
        kf = lax.rem(i + n_lat_chunks, nk)
        kr = nk - 1 - i
        sf_ref[0, pl.ds(kf, 1), :] = jnp.concatenate([fre, fim], axis=1)
        sr_ref[0, pl.ds(kr, 1), :] = jnp.concatenate([rre, rim], axis=1)
        zf = zf_ref[0, pl.ds(kf, 1), :]
        zr = zr_ref[0, pl.ds(kr, 1), :]
        nfre = af_re * fre - af_im * fim + zf[:, :st]
        nfim = af_re * fim + af_im * fre + zf[:, st:]
        nrre = ar_re * rre - ar_im * rim + zr[:, :st]
        nrim = ar_re * rim + ar_im * rre + zr[:, st:]
        return nfre, nfim, nrre, nrim

    z0 = jnp.zeros((1, st), F32)
    lax.fori_loop(0, nk, step, (z0, z0, z0, z0))


def ssm_scan(zf, zr, af, ar, n_lat_chunks):
    b, nk, n = zf.shape
    w = 2 * SSM_ST
    spec = pl.BlockSpec((1, nk, w), lambda bi, q: (bi, 0, q))
    aspec = pl.BlockSpec((1, w), lambda bi, q: (0, q))
    return pl.pallas_call(
        functools.partial(_ssm_scan_kernel, n_lat_chunks=n_lat_chunks),
        grid=(b, n // w), in_specs=[spec, spec, aspec, aspec], out_specs=[spec, spec],
        out_shape=[jax.ShapeDtypeStruct((b, nk, n), F32)] * 2,
        compiler_params=_cp(("arbitrary", "arbitrary")), name="ssm_scan",
    )(zf, zr, af, ar)


def _ssm_out_kernel(u_ref, sf_ref, sr_ref, mu_ref, mf_ref, mr_ref, y_ref):
    nk = sf_ref.shape[1]
    y = _dot(_chunk_rows(u_ref), mu_ref[0])
    y += _dot(sf_ref[0].astype(BF), mf_ref[0])
    y += _dot(sr_ref[0].astype(BF), mr_ref[0])
    for j in range(SSM_CHUNK):
        y_ref[0, 0, pl.ds(j, nk, stride=SSM_CHUNK), :] = y[:, j * LANES:(j + 1) * LANES]


def ssm_out(uq, sf, sr, m3u, m3f, m3r):
    qt, b, s, _ = uq.shape
    nk = s // SSM_CHUNK
    k = SSM_CHUNK * LANES
    w = 2 * SSM_ST
    one = pl.Buffered(1)
    uspec = pl.BlockSpec((1, 1, s, LANES), lambda q, bi: (q, bi, 0, 0))
    sspec = pl.BlockSpec((1, nk, w), lambda q, bi: (bi, 0, q))
    return pl.pallas_call(
        _ssm_out_kernel, grid=(qt, b),
        in_specs=[uspec, sspec, sspec,
                  pl.BlockSpec((1, k, k), lambda q, bi: (q, 0, 0), pipeline_mode=one),
                  pl.BlockSpec((1, w, k), lambda q, bi: (q, 0, 0), pipeline_mode=one),
                  pl.BlockSpec((1, w, k), lambda q, bi: (q, 0, 0), pipeline_mode=one)],
        out_specs=uspec,
        out_shape=jax.ShapeDtypeStruct((qt, b, s, LANES), F32),
        compiler_params=_cp(("arbitrary", "arbitrary")), name="ssm_out",
    )(uq, sf, sr, m3u, m3f, m3r)


def s5_matrices(a_re, a_im, b_re, b_im, c_re, c_im, log_dt, d_skip):
    tc, g, p, hh = SSM_CHUNK, SSM_GROUPS, SSM_STATE, SSM_GROUP
    lam = lax.complex(a_re.astype(F32), a_im.astype(F32))
    dt = jnp.exp(log_dt.astype(F32))[..., None]
    ldt = lam * dt
    a_bar = jnp.exp(ldt)
    b_bar = ((a_bar - 1.0) / lam)[..., None] * lax.complex(b_re.astype(F32), b_im.astype(F32))
    cm = lax.complex(c_re.astype(F32), c_im.astype(F32))
    steps = jnp.arange(tc + 1, dtype=F32)[:, None, None, None]
    pw = jnp.exp(ldt[None] * steps)
    pw_desc = jnp.exp(ldt[None] * (tc - steps))
    hi = lax.Precision.HIGHEST
    gpt = SSM_GPT

    def widen(compact, col_expand, row_div, col_div):
        rows, cols = compact.shape[1], col_expand.shape[1]
        wide = jnp.einsum('qrc,cn->qrn', compact.astype(BF), col_expand, preferred_element_type=F32)
        rg = (jnp.arange(rows) // row_div) % gpt
        cg = (jnp.arange(cols) // col_div) % gpt
        return jnp.where((rg[:, None] == cg[None, :])[None], wide, 0.0).astype(BF)

    x_tok = jnp.einsum('ab,m,cd->acbmd', jnp.eye(tc, dtype=F32), jnp.ones((gpt,), F32),
                       jnp.eye(hh, dtype=F32)).reshape(tc * hh, tc * LANES).astype(BF)
    x_st = jnp.einsum('ab,m,cd->acbmd', jnp.eye(4, dtype=F32), jnp.ones((gpt,), F32),
                      jnp.eye(p, dtype=F32)).reshape(4 * p, 4 * SSM_ST).astype(BF)

    kk = jnp.real(jnp.einsum('dghp,ldgp,dgpk->dlghk', cm, pw[:tc], b_bar))
    jj = jnp.arange(tc)[None, :, None]
    ii = jnp.arange(tc)[None, None, :]
    lag = jnp.arange(tc)[:, None, None]
    shift_f = (ii - jj == lag).astype(F32)
    shift_r = (jj - ii == lag).astype(F32)
    dd = (jnp.eye(tc, dtype=F32)[:, :, None, None, None] * jnp.eye(hh, dtype=F32)[None, None, None]
          * d_skip.astype(F32).reshape(g, hh)[None, None, :, :, None])
    t5 = (jnp.einsum('lji,lghk->jighk', shift_f, kk[0], precision=hi)
          + jnp.einsum('lji,lghk->jighk', shift_r, kk[1], precision=hi) + dd)
    tcomp = t5.reshape(tc, tc, SSM_QT, gpt, hh, hh).transpose(2, 0, 3, 5, 1, 4)
    m3u = widen(tcomp.reshape(SSM_QT, tc * LANES, tc * hh), x_tok, hh, hh)

    inf = pw_desc[1:, 0][..., None] * b_bar[0][None]
    inr = pw[:tc, 1][..., None] * b_bar[1][None]
    m1 = jnp.stack([jnp.real(inf), jnp.imag(inf), jnp.real(inr), jnp.imag(inr)], axis=0)
    m1 = m1.reshape(2, 2, tc, SSM_QT, gpt, p, hh).transpose(3, 2, 4, 6, 0, 1, 5)
    m1 = widen(m1.reshape(SSM_QT, tc * LANES, 4 * p), x_st, hh, p)

    def out_mat(d, powers):
        ca = cm[d][None] * powers[:, :, None, :]
        mm = jnp.stack([jnp.real(ca), -jnp.imag(ca)], axis=0)
        mm = mm.reshape(2, tc, SSM_QT, gpt, hh, p).transpose(2, 0, 3, 5, 1, 4)
        return widen(mm.reshape(SSM_QT, 2 * SSM_ST, tc * hh), x_tok, p, hh)

    m3f = out_mat(0, pw[1:, 0])
    m3r = out_mat(1, pw_desc[:tc, 1])

    def decay(d):
        a16 = pw[tc, d].reshape(SSM_QT, SSM_ST)
        return jnp.concatenate([jnp.real(a16), jnp.imag(a16)], axis=1).reshape(1, SSM_QT * 2 * SSM_ST)

    return m1.astype(BF), m3u.astype(BF), m3f.astype(BF), m3r.astype(BF), decay(0), decay(1)


def s5_branch(uq4, mats, n_lat_chunks):
    m1, m3u, m3f, m3r, a16f, a16r = mats
    zf, zr = ssm_in(uq4, m1)
    sf, sr = ssm_scan(zf, zr, a16f, a16r, n_lat_chunks)
    return ssm_out(uq4, sf, sr, m3u, m3f, m3r)


def _gqa_kernel(q_ref, k_ref, vt_ref, vtc_ref, o_ref, qs_ref, m_ref, l_ref, acc_ref, sa_ref, sb_ref, sctx_ref, *,
                n_lat, n_ctx, tk, n_lat_tiles):
    tq = q_ref.shape[1]
    qi = pl.program_id(1)
    n_slots = GQA_HEADS
    lane_half = lax.broadcasted_iota(jnp.int32, (tq, LANES), 1) // HEAD_DIM
    for t in range(GQA_Q_W // LANES):
        qt = q_ref[0, :, t * LANES:(t + 1) * LANES]
        for g in range(GQA_KV_HEADS):
            qs_ref[2 * t + g] = jnp.where(lane_half == g, qt, jnp.zeros((), qt.dtype))
    m_ref[...] = jnp.full(m_ref.shape, NEG, F32)
    l_ref[...] = jnp.zeros(l_ref.shape, F32)
    acc_ref[...] = jnp.zeros(acc_ref.shape, F32)

    n_chunks = n_lat // tk

    def scores(kc, s):
        return _dot_nt(kc, qs_ref[s])

    def lat_keys(c):
        return k_ref[0, pl.ds(pl.multiple_of(c * tk, tk), tk), :]

    def update(st, s, vtc):
        cs = slice(s * tq, (s + 1) * tq)
        m_old = m_ref[:, cs]
        m_new = jnp.maximum(m_old, jnp.max(st, axis=0, keepdims=True))
        alpha = jnp.exp2(m_old - m_new)
        p = jnp.exp2(st - m_new)
        l_ref[:, cs] = alpha * l_ref[:, cs] + jnp.sum(p, axis=0, keepdims=True)
        acc_ref[:, cs] = alpha * acc_ref[:, cs] + _dot(vtc, p.astype(BF))
        m_ref[:, cs] = m_new

    kcc = k_ref[0, n_lat:n_lat + n_ctx, :]
    for s in range(n_slots):
        sctx_ref[s] = scores(kcc, s)

    @pl.when(qi < n_lat_tiles)
    def _():
        bufs = (sa_ref, sb_ref)
        kc0 = lat_keys(0)
        for s in range(n_slots):
            sa_ref[s] = scores(kc0, s)
            update(sctx_ref[s], s, vtc_ref[0])

        def step(c, cur, nxt):
            kn = lat_keys(c + 1)
            for s in range(n_slots):
                nxt[s] = scores(kn, s)
                update(cur[s], s, vt_ref[0, c])

        def pair(i, carry):
            step(2 * i, sa_ref, sb_ref)
            step(2 * i + 1, sb_ref, sa_ref)
            return carry

        n_steps = n_chunks - 1
        lax.fori_loop(0, n_steps // 2, pair, 0)
        if n_steps % 2:
            step(n_steps - 1, sa_ref, sb_ref)
        last = bufs[n_steps % 2]
        for s in range(n_slots):
            update(last[s], s, vt_ref[0, n_chunks - 1])

    @pl.when(qi >= n_lat_tiles)
    def _():
        for s in range(n_slots):
            update(sctx_ref[s], s, vtc_ref[0])

    o = acc_ref[...] / l_ref[...]
    row_half = lax.broadcasted_iota(jnp.int32, (LANES, tq), 0) // HEAD_DIM
    for t in range(GQA_Q_W // LANES):
        o0 = o[:, (2 * t) * tq:(2 * t + 1) * tq]
        o1 = o[:, (2 * t + 1) * tq:(2 * t + 2) * tq]
        o_ref[0, :, t * LANES:(t + 1) * LANES] = jnp.where(row_half == 0, o0, o1).T.astype(o_ref.dtype)


def gqa_attention(qg, kg, vg, n_lat, n_q_tiles):
    b, s, _ = qg.shape
    tq = TOKEN_TILE
    tk = 512
    n_ctx = s - n_lat
    nc = n_lat // tk
    vt = jnp.swapaxes(vg[:, :n_lat].reshape(b, nc, tk, LANES), 2, 3)
    vtc = jnp.swapaxes(vg[:, n_lat:], 1, 2)
    kern = functools.partial(_gqa_kernel, n_lat=n_lat, n_ctx=n_ctx, tk=tk, n_lat_tiles=n_lat // tq)
    return pl.pallas_call(
        kern, grid=(b, n_q_tiles),
        in_specs=[pl.BlockSpec((1, tq, GQA_Q_W), lambda bi, i: (bi, i, 0)),
                  pl.BlockSpec((1, s, LANES), lambda bi, i: (bi, 0, 0)),
                  pl.BlockSpec((1, nc, LANES, tk), lambda bi, i: (bi, 0, 0, 0)),
                  pl.BlockSpec((1, LANES, n_ctx), lambda bi, i: (bi, 0, 0))],
        out_specs=pl.BlockSpec((1, tq, GQA_Q_W), lambda bi, i: (bi, i, 0)),
        out_shape=jax.ShapeDtypeStruct((b, n_q_tiles * tq, GQA_Q_W), BF),
        scratch_shapes=[pltpu.VMEM((GQA_HEADS, tq, LANES), BF),
                        pltpu.VMEM((1, GQA_HEADS * tq), F32),
                        pltpu.VMEM((1, GQA_HEADS * tq), F32),
                        pltpu.VMEM((LANES, GQA_HEADS * tq), F32),
                        pltpu.VMEM((GQA_HEADS, tk, tq), F32),
                        pltpu.VMEM((GQA_HEADS, tk, tq), F32),
                        pltpu.VMEM((GQA_HEADS, n_ctx, tq), F32)],
        compiler_params=_cp(("arbitrary", "arbitrary")), name="gqa",
    )(qg, kg, vt, vtc)


def _na_kernel(q_ref, k_ref, v_ref, tab_ref, o_ref, *, n_lat, n_ctx, rows):
    rb = pl.program_id(1)
    ws = jnp.clip(rb * NA_QROWS - NA_WIN_R // 2, 0, rows - NA_UNION)
    off = pl.multiple_of(ws * GRID_W, GRID_W)
    nwin = NA_UNION * GRID_W
    def window_and_context(ref, pair):
        ls = slice(pair * LANES, (pair + 1) * LANES)
        return jnp.concatenate([ref[0, pl.ds(off, nwin), ls], ref[0, n_lat:n_lat + n_ctx, ls]], axis=0)

    lane_half = lax.broadcasted_iota(jnp.int32, (q_ref.shape[1], LANES), 1) // HEAD_DIM

    def scores(h):
        kcat = window_and_context(k_ref, h // 2)
        qt = q_ref[0, :, (h // 2) * LANES:(h // 2 + 1) * LANES]
        qh = jnp.where(lane_half == h % 2, qt, jnp.zeros((), qt.dtype))
        return _dot_nt(qh, kcat) + tab_ref[0, h]

    s_next = scores(0)
    o_pair = [None, None]
    for h in range(NA_HEADS):
        s = s_next
        if h + 1 < NA_HEADS:
            s_next = scores(h + 1)
        m = jnp.max(s, axis=-1, keepdims=True)
        p = jnp.exp2(s - m)
        l = jnp.sum(p, axis=-1, keepdims=True)
        o_pair[h % 2] = _dot(p.astype(BF), window_and_context(v_ref, h // 2)) / l
        if h % 2:
            ts = slice((h // 2) * LANES, (h // 2 + 1) * LANES)
            o_ref[0, :, ts] = jnp.where(lane_half == 0, o_pair[0], o_pair[1]).astype(o_ref.dtype)


def na_attention(qn, kn, vn, table, n_lat, n_q_tiles):
    b, s, _ = qn.shape
    tq = NA_QROWS * GRID_W
    rows = n_lat // GRID_W
    n_lat_tiles = n_lat // tq
    tw = table.shape[-1]

    def tab_idx(bi, rb):
        case = jnp.where(rb == 0, 0, jnp.where(rb < n_lat_tiles - 1, 1, jnp.where(rb == n_lat_tiles - 1, 2, 3)))
        return (case, 0, 0, 0)

    kern = functools.partial(_na_kernel, n_lat=n_lat, n_ctx=s - n_lat, rows=rows)
    return pl.pallas_call(
        kern, grid=(b, n_q_tiles),
        in_specs=[pl.BlockSpec((1, tq, NA_W), lambda bi, rb: (bi, rb, 0)),
                  pl.BlockSpec((1, s, NA_W), lambda bi, rb: (bi, 0, 0)),
                  pl.BlockSpec((1, s, NA_W), lambda bi, rb: (bi, 0, 0)),
                  pl.BlockSpec((1, NA_HEADS, tq, tw), tab_idx)],
        out_specs=pl.BlockSpec((1, tq, NA_W), lambda bi, rb: (bi, rb, 0)),
        out_shape=jax.ShapeDtypeStruct((b, n_q_tiles * tq, NA_W), BF),
        compiler_params=_cp(("arbitrary", "arbitrary"), 56), name="na",
    )(qn, kn, vn, table)


def na_table(rpb, rows, n_ctx):
    tq = NA_QROWS * GRID_W
    qr = jnp.arange(NA_QROWS)[:, None, None, None]
    qc = jnp.arange(GRID_W)[None, :, None, None]
    kr = jnp.arange(NA_UNION)[None, None, :, None]
    kc = jnp.arange(GRID_W)[None, None, None, :]
    cs = jnp.clip(qc - NA_WIN_C // 2, 0, GRID_W - NA_WIN_C)
    col_ok = (kc >= cs) & (kc < cs + NA_WIN_C)
    col_idx = (kc - qc + NA_WIN_C - 1)[:, :, 0, :]
    col_hot = (col_idx[0, :, :, None] == jnp.arange(2 * NA_WIN_C - 1)).astype(F32)
    n_blocks = rows // NA_QROWS
    hi = lax.Precision.HIGHEST
    tabs = []
    for blk in (0, 1, n_blocks - 1):
        r = blk * NA_QROWS + qr
        ws = min(max(blk * NA_QROWS - NA_WIN_R // 2, 0), rows - NA_UNION)
        rs = jnp.clip(r - NA_WIN_R // 2, 0, rows - NA_WIN_R)
        ka = ws + kr
        ok = (ka >= rs) & (ka < rs + NA_WIN_R) & col_ok
        row_idx = (ka - r + NA_WIN_R - 1)[:, 0, :, 0]
        row_hot = (row_idx[:, :, None] == jnp.arange(2 * NA_WIN_R - 1)).astype(F32)
        by_row = jnp.einsum('hab,rka->hrkb', rpb.astype(F32), row_hot, precision=hi)
        bias = jnp.einsum('hrkb,qcb->hrqkc', by_row, col_hot, precision=hi)
        tabs.append(jnp.where(ok[None], bias, NEG).reshape(NA_HEADS, tq, NA_UNION * GRID_W))
    tabs.append(jnp.full((NA_HEADS, tq, NA_UNION * GRID_W), NEG, F32))
    win = jnp.stack(tabs, axis=0)
    return jnp.concatenate([win * LOG2E, jnp.zeros(win.shape[:3] + (n_ctx,), F32)], axis=-1)


def _merge_kernel(x_ref, y_ref, og_ref, on_ref, gt_ref, mod_ref, gw_ref, gb_ref, ws_ref, wg_ref, wn_ref, wo_ref,
                  o_ref):
    d = x_ref.shape[-1]
    y = jnp.concatenate([y_ref[q, 0].astype(F32) for q in range(SSM_QT)], axis=1)
    gy = _gelu_tanh(y)
    ys = gy * jax.nn.sigmoid(_dot(gy.astype(BF), gw_ref[...]) + gb_ref[...])
    gt = gt_ref[0]
    m = gt[:, :d].astype(F32) * _dot(ys.astype(BF), ws_ref[...])
    m += gt[:, d:2 * d].astype(F32) * _dot(og_ref[0], wg_ref[...])
    m += gt[:, 2 * d:].astype(F32) * _dot(on_ref[0], wn_ref[...])
    o_ref[0] = x_ref[0] + mod_ref[0][2:3] * _dot(m.astype(BF), wo_ref[...])


def merge(xall, y4, og, on, gt, mod, gw, gb, ws, wg, wn, wo, n_tiles, n_lat_tiles):
    b, s, d = xall.shape
    tm = TOKEN_TILE
    nb = mod.shape[0] - 1

    def row(bi, j):
        return (bi, j, 0)

    return pl.pallas_call(
        _merge_kernel, grid=(b, n_tiles),
        in_specs=[pl.BlockSpec((1, tm, d), row),
                  pl.BlockSpec((SSM_QT, 1, tm, LANES), lambda bi, j: (0, bi, j, 0)),
                  pl.BlockSpec((1, tm, GQA_Q_W), row),
                  pl.BlockSpec((1, tm, NA_W), row),
                  pl.BlockSpec((1, tm, 3 * d), row),
                  pl.BlockSpec((1, 6, d), lambda bi, j: (jnp.where(j < n_lat_tiles, bi, nb), 0, 0)),
                  _const_spec(gw.shape), _const_spec(gb.shape), _const_spec(ws.shape),
                  _const_spec(wg.shape), _const_spec(wn.shape), _const_spec(wo.shape)],
        out_specs=pl.BlockSpec((1, tm, d), row),
        out_shape=jax.ShapeDtypeStruct((b, n_tiles * tm, d), F32),
        compiler_params=_cp(("arbitrary", "arbitrary")), name="merge",
    )(xall, y4, og, on, gt, mod, gw, gb, ws, wg, wn, wo)


def _ffn_kernel(x_ref, mod_ref, g_ref, wg_ref, wu_ref, wd_ref, o_ref):
    x = x_ref[0]
    mod = mod_ref[0]
    hb = (_rms(x, g_ref[...]) * (1.0 + mod[4:5]) + mod[3:4]).astype(BF)
    a = _silu(_dot(hb, wg_ref[...])) * _dot(hb, wu_ref[...])
    o_ref[0] = x + mod[5:6] * _dot(a.astype(BF), wd_ref[...])


def ffn(xall, mod, g, wg, wu, wd, n_lat_tiles):
    b, s, d = xall.shape
    tm = TOKEN_TILE
    nb = mod.shape[0] - 1

    def row(bi, j):
        return (bi, j, 0)

    return pl.pallas_call(
        _ffn_kernel, grid=(b, s // tm),
        in_specs=[pl.BlockSpec((1, tm, d), row),
                  pl.BlockSpec((1, 6, d), lambda bi, j: (jnp.where(j < n_lat_tiles, bi, nb), 0, 0)),
                  _const_spec(g.shape), _const_spec(wg.shape), _const_spec(wu.shape), _const_spec(wd.shape)],
        out_specs=pl.BlockSpec((1, tm, d), row),
        out_shape=jax.ShapeDtypeStruct((b, s, d), F32),
        compiler_params=_cp(("arbitrary", "arbitrary"), 56), name="ffn",
    )(xall, mod, g, wg, wu, wd)


def _route_kernel(x_ref, mod_ref, g_ref, rw_ref, h_ref, info_ref):
    x = x_ref[0]
    mod = mod_ref[0]
    h = _rms(x, g_ref[...]) * (1.0 + mod[4:5]) + mod[3:4]
    h_ref[0] = h
    logits = jnp.dot(h, rw_ref[...], preferred_element_type=F32, precision=lax.Precision.HIGHEST)
    lane = lax.broadcasted_iota(jnp.int32, logits.shape, 1)
    lanef = lane.astype(F32)
    logits = jnp.where(lane < N_EXPERTS, logits, -jnp.inf)
    m1 = jnp.max(logits, axis=-1, keepdims=True)
    i1 = jnp.min(jnp.where(logits == m1, lanef, float(LANES)), axis=-1, keepdims=True)
    rest = jnp.where(lanef == i1, -jnp.inf, logits)
    m2 = jnp.max(rest, axis=-1, keepdims=True)
    i2 = jnp.min(jnp.where(rest == m2, lanef, float(LANES)), axis=-1, keepdims=True)
    e2 = jnp.exp(m2 - m1)
    w1 = 1.0 / (1.0 + e2)
    w2 = e2 / (1.0 + e2)
    info_ref[0] = jnp.where(lane == 0, i1, jnp.where(lane == 1, i2, jnp.where(lane == 2, w1,
                            jnp.where(lane == 3, w2, 0.0))))


def moe_route(xall, mod, g, rw, n_tiles):
    b, s, d = xall.shape
    tm = TOKEN_TILE

    def row(bi, j):
        return (bi, j, 0)

    return pl.pallas_call(
        _route_kernel, grid=(b, n_tiles),
        in_specs=[pl.BlockSpec((1, tm, d), row), pl.BlockSpec((1, 6, d), lambda bi, j: (bi, 0, 0)),
                  _const_spec(g.shape), _const_spec(rw.shape)],
        out_specs=[pl.BlockSpec((1, tm, d), row), pl.BlockSpec((1, tm, LANES), row)],
        out_shape=[jax.ShapeDtypeStruct((b, n_tiles * tm, d), F32),
                   jax.ShapeDtypeStruct((b, n_tiles * tm, LANES), F32)],
        compiler_params=_cp(("arbitrary", "arbitrary")), name="moe_route",
    )(xall, mod, g, rw)


def _dispatch_kernel(pos_ref, h_ref, xs_in_ref, xs_ref, sem):
    del xs_in_ref
    tm = h_ref.shape[0]

    def row_copy(r, k):
        return pltpu.make_async_copy(h_ref.at[pl.ds(r, 1)], xs_ref.at[pl.ds(pos_ref[0, 0, 2 * r + k], 1)], sem)

    def issue(r, c):
        row_copy(r, 0).start(priority=0)
        row_copy(r, 1).start(priority=1)
        return c

    def drain(r, c):
        row_copy(r, 0).wait()
        row_copy(r, 1).wait()
        return c

    lax.fori_loop(0, tm, issue, 0, unroll=4)
    lax.fori_loop(0, tm, drain, 0)


def moe_dispatch(h2, pos, n_slots):
    t, d = h2.shape
    tm = TOKEN_TILE
    zeros = jnp.zeros((n_slots, d), F32)
    return pl.pallas_call(
        _dispatch_kernel, grid=(t // tm,),
        in_specs=[pl.BlockSpec((1, 1, 2 * tm), lambda i: (i, 0, 0), memory_space=pltpu.SMEM),
                  pl.BlockSpec((tm, d), lambda i: (i, 0)),
                  pl.BlockSpec(memory_space=pl.ANY)],
        out_specs=pl.BlockSpec(memory_space=pl.ANY),
        out_shape=jax.ShapeDtypeStruct((n_slots, d), F32),
        scratch_shapes=[pltpu.SemaphoreType.DMA(())],
        input_output_aliases={2: 0},
        compiler_params=_cp(("arbitrary",)), name="moe_dispatch",
    )(pos, h2, zeros)


def _experts_kernel(te_ref, nv_ref, x_ref, wg_ref, wu_ref, wd_ref, o_ref, acc_ref):
    i = pl.program_id(0)
    j = pl.program_id(1)

    @pl.when(i < nv_ref[0])
    def _():
        xb = x_ref[...].astype(BF)
        a = _silu(_dot(xb, wg_ref[0])) * _dot(xb, wu_ref[0])
        part = _dot(a.astype(BF), wd_ref[0])

        @pl.when(j == 0)
        def _():
            acc_ref[...] = part

        @pl.when(j > 0)
        def _():
            acc_ref[...] += part

        @pl.when(j == pl.num_programs(1) - 1)
        def _():
            o_ref[...] = acc_ref[...]

    @pl.when((i >= nv_ref[0]) & (j == pl.num_programs(1) - 1))
    def _():
        o_ref[...] = jnp.zeros(o_ref.shape, o_ref.dtype)


def moe_experts(xs, tile_expert, n_valid, wg, wu, wd):
    n_slots, d = xs.shape
    tm = MOE_TILE
    tf = MOE_FCHUNK
    f = wg.shape[-1]
    nf = f // tf

    def xrow(i, j, te, nv):
        return (jnp.minimum(i, nv[0] - 1), 0)

    def fcol(i, j, nv):
        return jnp.where(i < nv[0], j, nf - 1)

    grid_spec = pltpu.PrefetchScalarGridSpec(
        num_scalar_prefetch=2, grid=(n_slots // tm, nf),
        in_specs=[pl.BlockSpec((tm, d), xrow),
                  pl.BlockSpec((1, d, tf), lambda i, j, te, nv: (te[i], 0, fcol(i, j, nv))),
                  pl.BlockSpec((1, d, tf), lambda i, j, te, nv: (te[i], 0, fcol(i, j, nv))),
                  pl.BlockSpec((1, tf, d), lambda i, j, te, nv: (te[i], fcol(i, j, nv), 0))],
        out_specs=pl.BlockSpec((tm, d), lambda i, j, te, nv: (i, 0)),
        scratch_shapes=[pltpu.VMEM((tm, d), F32)])
    return pl.pallas_call(
        _experts_kernel, grid_spec=grid_spec,
        out_shape=jax.ShapeDtypeStruct((n_slots, d), F32),
        compiler_params=_cp(("arbitrary", "arbitrary"), 56), name="moe_experts",
    )(tile_expert, n_valid, xs, wg, wu, wd)


def _combine_kernel(pos_ref, x_ref, info_ref, mod_ref, fg_ref, ys_ref, o_ref, y1_ref, y2_ref, sem):
    tm = x_ref.shape[1]

    def row_copy(r, k, dst):
        return pltpu.make_async_copy(ys_ref.at[pl.ds(pos_ref[0, 0, 2 * r + k], 1)], dst.at[pl.ds(r, 1)], sem)

    def issue(r, c):
        row_copy(r, 0, y1_ref).start(priority=0)
        row_copy(r, 1, y2_ref).start(priority=1)
        return c

    def drain(r, c):
        row_copy(r, 0, y1_ref).wait()
        row_copy(r, 1, y2_ref).wait()
        return c

    lax.fori_loop(0, tm, issue, 0, unroll=4)
    lax.fori_loop(0, tm, drain, 0)
    info = info_ref[0]
    y = info[:, 2:3] * y1_ref[...] + info[:, 3:4] * y2_ref[...]
    xn = x_ref[0] + mod_ref[0][5:6] * y
    o_ref[0] = _rms(xn, fg_ref[...])


def moe_combine(x, info, mod, fg, ys, pos, n_tiles):
    b, s, d = x.shape
    tm = TOKEN_TILE

    def row(bi, j):
        return (bi, j, 0)

    return pl.pallas_call(
        _combine_kernel, grid=(b, n_tiles),
        in_specs=[pl.BlockSpec((1, 1, 2 * tm), lambda bi, j: (bi * n_tiles + j, 0, 0), memory_space=pltpu.SMEM),
                  pl.BlockSpec((1, tm, d), row), pl.BlockSpec((1, tm, LANES), row),
                  pl.BlockSpec((1, 6, d), lambda bi, j: (bi, 0, 0)), _const_spec(fg.shape),
                  pl.BlockSpec(memory_space=pl.ANY)],
        out_specs=pl.BlockSpec((1, tm, d), row),
        out_shape=jax.ShapeDtypeStruct((b, n_tiles * tm, d), F32),
        scratch_shapes=[pltpu.VMEM((tm, d), F32), pltpu.VMEM((tm, d), F32), pltpu.SemaphoreType.DMA(())],
        compiler_params=_cp(("arbitrary", "arbitrary")), name="moe_combine",
    )(pos, x, info, mod, fg, ys)


def moe_layer(x, mod, g2, rw, wg, wu, wd, fg, n_lat):
    b, s, d = x.shape
    n_tiles = n_lat // TOKEN_TILE
    t = b * n_lat
    rw_pad = jnp.zeros((d, LANES), F32).at[:, :N_EXPERTS].set(rw.astype(F32))
    h2, info = moe_route(x, mod, g2, rw_pad, n_tiles)
    info2 = info.reshape(t, LANES)

    e_pair = info2[:, :2].astype(jnp.int32).reshape(2 * t)
    onehot = (e_pair[:, None] == jnp.arange(N_EXPERTS)[None, :]).astype(jnp.int32)
    csum = jnp.cumsum(onehot, axis=0)
    rank = jnp.sum((csum - onehot) * onehot, axis=1)
    counts = csum[-1]
    tiles_e = (counts + MOE_TILE - 1) // MOE_TILE
    tile_end = jnp.cumsum(tiles_e)
    slot_off = (tile_end - tiles_e) * MOE_TILE
    pos = (slot_off[e_pair] + rank).astype(jnp.int32).reshape(t // TOKEN_TILE, 1, 2 * TOKEN_TILE)
    n_tiles_max = (2 * t) // MOE_TILE + N_EXPERTS
    n_valid = tile_end[-1:].astype(jnp.int32)
    tile_ids = jnp.minimum(jnp.arange(n_tiles_max), n_valid[0] - 1)
    tile_expert = jnp.sum((tile_ids[:, None] >= tile_end[None, :]).astype(jnp.int32), axis=1).astype(jnp.int32)

    xs = moe_dispatch(h2.reshape(t, d), pos, n_tiles_max * MOE_TILE)
    ys = moe_experts(xs, tile_expert, n_valid, wg, wu, wd)
    return moe_combine(x, info, mod, fg, ys, pos, n_tiles)


def _rope_tables(n_lat, n_ctx):
    t = jnp.arange(n_lat)
    pos = jnp.stack([t // GRID_W, t % GRID_W], axis=-1).astype(F32)
    half = HEAD_DIM // 2
    inv = 1.0 / (ROPE_BASE ** (jnp.arange(0, half, 2, dtype=F32) / half))
    ang = pos[:, :, None] * inv
    ang = jnp.concatenate([ang, ang], axis=-1).reshape(n_lat, HEAD_DIM)
    cos = jnp.concatenate([jnp.cos(ang), jnp.ones((n_ctx, HEAD_DIM), F32)], axis=0)
    sin = jnp.concatenate([jnp.sin(ang), jnp.zeros((n_ctx, HEAD_DIM), F32)], axis=0)
    return jnp.tile(cos, (1, 2)), jnp.tile(sin, (1, 2))


_GQA_HEAD_ORDER = tuple(g * GQA_GROUP + t for t in range(GQA_GROUP) for g in range(GQA_KV_HEADS))


def _gqa_cols(w):
    return jnp.concatenate([w[:, h * HEAD_DIM:(h + 1) * HEAD_DIM] for h in _GQA_HEAD_ORDER], axis=1)


def _gqa_rows(w):
    return jnp.concatenate([w[h * HEAD_DIM:(h + 1) * HEAD_DIM] for h in _GQA_HEAD_ORDER], axis=0)


def kernel(x, c, ctx, c_ctx, w_mod, b_mod, norm1_g, w_in, ssm_a_re, ssm_a_im, ssm_b_re, ssm_b_im, ssm_c_re,
           ssm_c_im, ssm_log_dt, ssm_d, glu_w, glu_b, q_norm_g, k_norm_g, na_rpb, w_branch_ssm, w_branch_gqa,
           w_branch_na, w_out, norm2_g, ffn_w_gate, ffn_w_up, ffn_w_down, router_w, moe_w_gate, moe_w_up,
           moe_w_down, final_norm_g):
    b, n_lat, d = x.shape
    n_ctx = ctx.shape[1]
    s = n_lat + n_ctx
    depth = w_mod.shape[0]
    assert d == D_MODEL and n_lat % (NA_QROWS * GRID_W) == 0 and n_ctx == TOKEN_TILE and n_lat % 512 == 0
    assert depth == 2 and n_lat // GRID_W >= NA_UNION
    n_lat_tiles = n_lat // TOKEN_TILE
    n_all_tiles = s // TOKEN_TILE

    n_rows = -(-(b + 1) // 8) * 8
    cv = jnp.zeros((n_rows, d), F32).at[:b].set(c).at[b].set(c_ctx)
    mods = adaln(cv, w_mod, b_mod)[:, :b + 1].reshape(depth, b + 1, 6, d)

    cos, sin = _rope_tables(n_lat, n_ctx)
    table_shape_rows = n_lat // GRID_W
    xall = jnp.concatenate([x, ctx], axis=1)

    out = None
    for i in range(depth):
        last = i == depth - 1
        wi = w_in[i]
        c0 = SSM_WIDTH
        wu = wi[:, :c0].astype(BF)
        wkv = wi[:, c0:KV_COLS].astype(BF)
        wqg = _gqa_cols(wi[:, KV_COLS:KV_COLS + GQA_Q_W]).astype(BF)
        wqn = wi[:, KV_COLS + GQA_Q_W:KV_COLS + GQA_Q_W + NA_W].astype(BF)
        wgt = wi[:, KV_COLS + GQA_Q_W + NA_W:].astype(BF)
        qgain = jnp.tile(q_norm_g[i].astype(F32), 2)[None, :]
        kgain = jnp.tile(k_norm_g[i].astype(F32), GQA_KV_HEADS)[None, :]

        uq, kg, vg, kn, vn, qg, qn, gt = inproj(xall, mods[i], norm1_g[i][None, :], cos, sin, wu, wkv, wqg, wqn,
                                                wgt, qgain, kgain, n_lat_tiles)

        mats = s5_matrices(ssm_a_re[i], ssm_a_im[i], ssm_b_re[i], ssm_b_im[i], ssm_c_re[i], ssm_c_im[i],
                           ssm_log_dt[i], ssm_d[i])
        y4 = s5_branch(uq, mats, n_lat // SSM_CHUNK)

        n_q_tiles = n_lat_tiles if last else n_all_tiles
        og = gqa_attention(qg, kg, vg, n_lat, n_q_tiles)
        on = na_attention(qn, kn, vn, na_table(na_rpb[i], table_shape_rows, n_ctx), n_lat, n_q_tiles)

        xall = merge(xall, y4, og, on, gt, mods[i], glu_w[i].astype(BF), glu_b[i][None, :].astype(F32),
                     w_branch_ssm[i].astype(BF), _gqa_rows(w_branch_gqa[i]).astype(BF),
                     w_branch_na[i].astype(BF), w_out[i].astype(BF), n_q_tiles, n_lat_tiles)

        j = i // 2
        if i % 2 == 0:
            xall = ffn(xall, mods[i], norm2_g[i][None, :], ffn_w_gate[j].astype(BF), ffn_w_up[j].astype(BF),
                       ffn_w_down[j].astype(BF), n_lat_tiles)
        else:
            out = moe_layer(xall, mods[i], norm2_g[i][None, :], router_w[j], moe_w_gate[j].astype(BF),
                            moe_w_up[j].astype(BF), moe_w_down[j].astype(BF), final_norm_g[None, :], n_lat)
    return out
```

```python
import functools
import math

import jax
import jax.numpy as jnp
from jax import lax
from jax.experimental import pallas as pl
from jax.experimental.pallas import tpu as pltpu

D_MODEL = 1024
GRID_W = 64
SSM_WIDTH = 512
SSM_GROUP = 16
SSM_GROUPS = SSM_WIDTH // SSM_GROUP
SSM_STATE = 64
HEAD_DIM = 64
GQA_HEADS = 8
GQA_KV_HEADS = 2
GQA_GROUP = GQA_HEADS // GQA_KV_HEADS
ROPE_BASE = 10000.0
NA_HEADS = 8
NA_WIN_R = 8
NA_WIN_C = 16
ATTN_SCALE = HEAD_DIM ** -0.5
LOG2E = math.log2(math.e)
GQA_Q_W = GQA_HEADS * HEAD_DIM
GQA_KV_W = GQA_KV_HEADS * HEAD_DIM
NA_W = NA_HEADS * HEAD_DIM
KV_COLS = SSM_WIDTH + 2 * GQA_KV_W + 2 * NA_W
N_EXPERTS = 8
NORM_EPS = 1e-6

LANES = 128
TOKEN_TILE = 256
SSM_CHUNK = 16
SSM_QT = SSM_WIDTH // LANES
SSM_GPT = LANES // SSM_GROUP
SSM_ST = SSM_GPT * SSM_STATE
NA_QROWS = 4
NA_UNION = 12
MOE_TILE = 512
MOE_ROW_TILE = 1024
MOE_FCHUNK = 1792
NEG = -1e30

BF = jnp.bfloat16
F32 = jnp.float32


def _cp(sem, vmem_mb=48):
    return pltpu.CompilerParams(dimension_semantics=sem, vmem_limit_bytes=vmem_mb * 1024 * 1024)


def _const_spec(shape):
    nd = len(shape)
    return pl.BlockSpec(shape, lambda *_: (0,) * nd, pipeline_mode=pl.Buffered(1))


def _dot(a, b):
    return jnp.dot(a, b, preferred_element_type=F32)


def _dot_nt(a, b):
    return lax.dot_general(a, b, (((1,), (1,)), ((), ())), preferred_element_type=F32)


def _rms(x, g):
    return x * lax.rsqrt(jnp.mean(x * x, axis=-1, keepdims=True) + NORM_EPS) * g


def _silu(x):
    return x * jax.nn.sigmoid(x)


def _gelu_tanh(x):
    return 0.5 * x * (1.0 + jnp.tanh(math.sqrt(2.0 / math.pi) * (x + 0.044715 * x * x * x)))


def _adaln_kernel(cv_ref, w_ref, b_ref, o_ref):
    a = _silu(cv_ref[...])
    w = w_ref[0]
    a_hi = a.astype(BF)
    a_lo = (a - a_hi.astype(F32)).astype(BF)
    w_hi = w.astype(BF)
    w_lo = (w - w_hi.astype(F32)).astype(BF)
    o_ref[0] = _dot(a_hi, w_hi) + _dot(a_hi, w_lo) + _dot(a_lo, w_hi) + b_ref[0]


def adaln(cv, w_mod, b_mod):
    depth, d, n = w_mod.shape
    r = cv.shape[0]
    tn = 512
    return pl.pallas_call(
        _adaln_kernel,
        grid=(depth, n // tn),
        in_specs=[pl.BlockSpec((r, d), lambda i, j: (0, 0)),
                  pl.BlockSpec((1, d, tn), lambda i, j: (i, 0, j)),
                  pl.BlockSpec((1, 1, tn), lambda i, j: (i, 0, j))],
        out_specs=pl.BlockSpec((1, r, tn), lambda i, j: (i, 0, j)),
        out_shape=jax.ShapeDtypeStruct((depth, r, n), F32),
        compiler_params=_cp(("arbitrary", "arbitrary")),
        name="adaln",
    )(cv, w_mod, b_mod.reshape(depth, 1, n))


def _rope(xn, cos, sin):
    lane = lax.broadcasted_iota(jnp.int32, xn.shape, 1)
    first = (lane % 32) < 16
    rot = jnp.where(first, -pltpu.roll(xn, LANES - 16, 1), pltpu.roll(xn, 16, 1))
    return xn * cos + rot * sin


def _stream_specs(x_lat, x_ctx, n_lat_tiles):
    tm = TOKEN_TILE
    d = x_lat.shape[-1]
    ctx_tile0 = x_ctx.shape[1] // tm - 1
    return [pl.BlockSpec((1, tm, d), lambda bi, j: (bi, jnp.minimum(j, n_lat_tiles - 1), 0)),
            pl.BlockSpec((1, tm, d), lambda bi, j: (bi, ctx_tile0, 0))]


def _stream_tile(xl_ref, xc_ref, n_lat_tiles):
    return jnp.where(pl.program_id(1) < n_lat_tiles, xl_ref[0], xc_ref[0])


def _inproj_kernel(xl_ref, xc_ref, mod_ref, g_ref, cos_ref, sin_ref, wu_ref, wkv_ref, wqg_ref, wqn_ref, wgt_ref,
                   qgain_ref, kgain_ref,
                   u_ref, kg_ref, vg_ref, kn_ref, vn_ref, qg_ref, qn_ref, gt_ref, *, n_lat_tiles):
    x = _stream_tile(xl_ref, xc_ref, n_lat_tiles)
    mod = mod_ref[0]
    h = _rms(x, g_ref[...]) * (1.0 + mod[1:2]) + mod[0:1]
    hb = h.astype(BF)
    cos = cos_ref[...]
    sin = sin_ref[...]

    u = _dot(hb, wu_ref[...])
    for q in range(SSM_QT):
        u_ref[q, 0] = u[:, q * LANES:(q + 1) * LANES]

    def head_pair_norm_rope(t, gain):
        lane = lax.broadcasted_iota(jnp.int32, t.shape, 1)
        lo = lane < HEAD_DIM
        sq = t * t
        ms_lo = jnp.sum(jnp.where(lo, sq, 0.0), axis=-1, keepdims=True)
        ms_hi = jnp.sum(jnp.where(lo, 0.0, sq), axis=-1, keepdims=True)
        ms = jnp.where(lo, ms_lo, ms_hi) * (1.0 / HEAD_DIM)
        return _rope(t * lax.rsqrt(ms + NORM_EPS) * gain, cos, sin)

    kv = _dot(hb, wkv_ref[...])
    kg_ref[0] = head_pair_norm_rope(kv[:, :LANES], kgain_ref[...]).astype(BF)
    vg_ref[0] = kv[:, LANES:2 * LANES].astype(BF)
    kn_ref[0] = kv[:, 2 * LANES:2 * LANES + NA_W].astype(BF)
    vn_ref[0] = kv[:, 2 * LANES + NA_W:].astype(BF)

    qg = _dot(hb, wqg_ref[...])
    for t in range(GQA_Q_W // LANES):
        ts = slice(t * LANES, (t + 1) * LANES)
        qg_ref[0, :, ts] = (head_pair_norm_rope(qg[:, ts], qgain_ref[...]) * (ATTN_SCALE * LOG2E)).astype(BF)

    qn_ref[0] = (_dot(hb, wqn_ref[...]) * (ATTN_SCALE * LOG2E)).astype(BF)
    gt_ref[0] = jax.nn.sigmoid(_dot(hb, wgt_ref[...])).astype(BF)


def inproj(x_lat, x_ctx, mod, g, cos, sin, wu, wkv, wqg, wqn, wgt, qgain, kgain, n_lat_tiles):
    b, _, d = x_lat.shape
    tm = TOKEN_TILE
    nt = n_lat_tiles + 1
    s = nt * tm
    nb = mod.shape[0] - 1

    def row(bi, j):
        return (bi, j, 0)

    def modrow(bi, j):
        return (jnp.where(j < n_lat_tiles, bi, nb), 0, 0)

    outs = [
        jax.ShapeDtypeStruct((SSM_QT, b, s, LANES), F32),
        jax.ShapeDtypeStruct((b, s, LANES), BF),
        jax.ShapeDtypeStruct((b, s, LANES), BF),
        jax.ShapeDtypeStruct((b, s, NA_W), BF),
        jax.ShapeDtypeStruct((b, s, NA_W), BF),
        jax.ShapeDtypeStruct((b, s, GQA_Q_W), BF),
        jax.ShapeDtypeStruct((b, s, NA_W), BF),
        jax.ShapeDtypeStruct((b, s, 3 * d), BF),
    ]
    out_specs = [
        pl.BlockSpec((SSM_QT, 1, tm, LANES), lambda bi, j: (0, bi, j, 0)),
        pl.BlockSpec((1, tm, LANES), row),
        pl.BlockSpec((1, tm, LANES), row),
        pl.BlockSpec((1, tm, NA_W), row),
        pl.BlockSpec((1, tm, NA_W), row),
        pl.BlockSpec((1, tm, GQA_Q_W), row),
        pl.BlockSpec((1, tm, NA_W), row),
        pl.BlockSpec((1, tm, 3 * d), row),
    ]
    in_specs = _stream_specs(x_lat, x_ctx, n_lat_tiles) + [
        pl.BlockSpec((1, 6, d), modrow),
        _const_spec(g.shape),
        pl.BlockSpec((tm, LANES), lambda bi, j: (j, 0)),
        pl.BlockSpec((tm, LANES), lambda bi, j: (j, 0)),
        _const_spec(wu.shape), _const_spec(wkv.shape), _const_spec(wqg.shape),
        _const_spec(wqn.shape), _const_spec(wgt.shape), _const_spec(qgain.shape), _const_spec(kgain.shape),
    ]
    return pl.pallas_call(
        functools.partial(_inproj_kernel, n_lat_tiles=n_lat_tiles), grid=(b, nt), in_specs=in_specs,
        out_specs=out_specs, out_shape=outs,
        compiler_params=_cp(("arbitrary", "arbitrary"), 56), name="inproj",
    )(x_lat, x_ctx, mod, g, cos, sin, wu, wkv, wqg, wqn, wgt, qgain, kgain)


def _chunk_rows(u_ref):
    nk = u_ref.shape[2] // SSM_CHUNK
    cols = [u_ref[0, 0, pl.ds(j, nk, stride=SSM_CHUNK), :] for j in range(SSM_CHUNK)]
    return jnp.concatenate(cols, axis=1).astype(BF)


def _ssm_in_kernel(u_ref, m_ref, zf_ref, zr_ref):
    z = _dot(_chunk_rows(u_ref), m_ref[0, 0])
    half = z.shape[1] // 2
    zf_ref[0] = z[:, :half]
    zr_ref[0] = z[:, half:]


def ssm_in(uq, m1, layer):
    qt, b, s, _ = uq.shape
    nk = s // SSM_CHUNK
    k = SSM_CHUNK * LANES
    w = 2 * SSM_ST
    zspec = pl.BlockSpec((1, nk, w), lambda q, bi: (bi, 0, q))
    return pl.pallas_call(
        _ssm_in_kernel, grid=(qt, b),
        in_specs=[pl.BlockSpec((1, 1, s, LANES), lambda q, bi: (q, bi, 0, 0)),
                  pl.BlockSpec((1, 1, k, 2 * w), lambda q, bi: (layer, q, 0, 0))],
        out_specs=[zspec, zspec],
        out_shape=[jax.ShapeDtypeStruct((b, nk, qt * w), F32)] * 2,
        compiler_params=_cp(("arbitrary", "arbitrary")), name="ssm_in",
    )(uq, m1)


def _ssm_scan_kernel(zf_ref, zr_ref, af_ref, ar_ref, sf_ref, sr_ref, *, n_lat_chunks):
    nk = zf_ref.shape[1]
    st = SSM_ST
    af_re, af_im = af_ref[0, :, :st], af_ref[0, :, st:]
    ar_re, ar_im = ar_ref[0, :, :st], ar_ref[0, :, st:]

    def step(i, carry):
        fre, fim, rre, rim = carry
        kf = lax.rem(i + n_lat_chunks, nk)
        kr = nk - 1 - i
        sf_ref[0, pl.ds(kf, 1), :] = jnp.concatenate([fre, fim], axis=1)
        sr_ref[0, pl.ds(kr, 1), :] = jnp.concatenate([rre, rim], axis=1)
        zf = zf_ref[0, pl.ds(kf, 1), :]
        zr = zr_ref[0, pl.ds(kr, 1), :]
        nfre = af_re * fre - af_im * fim + zf[:, :st]
        nfim = af_re * fim + af_im * fre + zf[:, st:]
        nrre = ar_re * rre - ar_im * rim + zr[:, :st]
        nrim = ar_re * rim + ar_im * rre + zr[:, st:]
        return nfre, nfim, nrre, nrim

    z0 = jnp.zeros((1, st), F32)
    lax.fori_loop(0, nk, step, (z0, z0, z0, z0))


def ssm_scan(zf, zr, af, ar, layer, n_lat_chunks):
    b, nk, n = zf.shape
    w = 2 * SSM_ST
    spec = pl.BlockSpec((1, nk, w), lambda bi, q: (bi, 0, q))
    aspec = pl.BlockSpec((1, 1, w), lambda bi, q: (layer, 0, q))
    return pl.pallas_call(
        functools.partial(_ssm_scan_kernel, n_lat_chunks=n_lat_chunks),
        grid=(b, n // w), in_specs=[spec, spec, aspec, aspec], out_specs=[spec, spec],
        out_shape=[jax.ShapeDtypeStruct((b, nk, n), F32)] * 2,
        compiler_params=_cp(("arbitrary", "arbitrary")), name="ssm_scan",
    )(zf, zr, af, ar)


def _ssm_out_kernel(u_ref, sf_ref, sr_ref, mu_ref, mf_ref, mr_ref, y_ref):
    nk = sf_ref.shape[1]
    y = _dot(_chunk_rows(u_ref), mu_ref[0, 0])
    y += _dot(sf_ref[0].astype(BF), mf_ref[0, 0])
    y += _dot(sr_ref[0].astype(BF), mr_ref[0, 0])
    for j in range(SSM_CHUNK):
        y_ref[0, 0, pl.ds(j, nk, stride=SSM_CHUNK), :] = y[:, j * LANES:(j + 1) * LANES]


def ssm_out(uq, sf, sr, m3u, m3f, m3r, layer):
    qt, b, s, _ = uq.shape
    nk = s // SSM_CHUNK
    k = SSM_CHUNK * LANES
    w = 2 * SSM_ST
    one = pl.Buffered(1)
    uspec = pl.BlockSpec((1, 1, s, LANES), lambda q, bi: (q, bi, 0, 0))
    sspec = pl.BlockSpec((1, nk, w), lambda q, bi: (bi, 0, q))
    return pl.pallas_call(
        _ssm_out_kernel, grid=(qt, b),
        in_specs=[uspec, sspec, sspec,
                  pl.BlockSpec((1, 1, k, k), lambda q, bi: (layer, q, 0, 0), pipeline_mode=one),
                  pl.BlockSpec((1, 1, w, k), lambda q, bi: (layer, q, 0, 0), pipeline_mode=one),
                  pl.BlockSpec((1, 1, w, k), lambda q, bi: (layer, q, 0, 0), pipeline_mode=one)],
        out_specs=uspec,
        out_shape=jax.ShapeDtypeStruct((qt, b, s, LANES), F32),
        compiler_params=_cp(("arbitrary", "arbitrary")), name="ssm_out",
    )(uq, sf, sr, m3u, m3f, m3r)


def s5_matrices(a_re, a_im, b_re, b_im, c_re, c_im, log_dt, d_skip):
    tc, g, p, hh = SSM_CHUNK, SSM_GROUPS, SSM_STATE, SSM_GROUP
    lam = lax.complex(a_re.astype(F32), a_im.astype(F32))
    dt = jnp.exp(log_dt.astype(F32))[..., None]
    ldt = lam * dt
    a_bar = jnp.exp(ldt)
    b_bar = ((a_bar - 1.0) / lam)[..., None] * lax.complex(b_re.astype(F32), b_im.astype(F32))
    cm = lax.complex(c_re.astype(F32), c_im.astype(F32))
    steps = jnp.arange(tc + 1, dtype=F32)[:, None, None, None]
    pw = jnp.exp(ldt[None] * steps)
    pw_desc = jnp.exp(ldt[None] * (tc - steps))
    hi = lax.Precision.HIGHEST
    gpt = SSM_GPT

    def widen(compact, col_expand, row_div, col_div):
        rows, cols = compact.shape[1], col_expand.shape[1]
        wide = jnp.einsum('qrc,cn->qrn', compact.astype(BF), col_expand, preferred_element_type=F32)
        rg = (jnp.arange(rows) // row_div) % gpt
        cg = (jnp.arange(cols) // col_div) % gpt
        return jnp.where((rg[:, None] == cg[None, :])[None], wide, 0.0).astype(BF)

    x_tok = jnp.einsum('ab,m,cd->acbmd', jnp.eye(tc, dtype=F32), jnp.ones((gpt,), F32),
                       jnp.eye(hh, dtype=F32)).reshape(tc * hh, tc * LANES).astype(BF)
    x_st = jnp.einsum('ab,m,cd->acbmd', jnp.eye(4, dtype=F32), jnp.ones((gpt,), F32),
                      jnp.eye(p, dtype=F32)).reshape(4 * p, 4 * SSM_ST).astype(BF)

    kk = jnp.real(jnp.einsum('dghp,ldgp,dgpk->dlghk', cm, pw[:tc], b_bar))
    jj = jnp.arange(tc)[None, :, None]
    ii = jnp.arange(tc)[None, None, :]
    lag = jnp.arange(tc)[:, None, None]
    shift_f = (ii - jj == lag).astype(F32)
    shift_r = (jj - ii == lag).astype(F32)
    dd = (jnp.eye(tc, dtype=F32)[:, :, None, None, None] * jnp.eye(hh, dtype=F32)[None, None, None]
          * d_skip.astype(F32).reshape(g, hh)[None, None, :, :, None])
    t5 = (jnp.einsum('lji,lghk->jighk', shift_f, kk[0], precision=hi)
          + jnp.einsum('lji,lghk->jighk', shift_r, kk[1], precision=hi) + dd)
    tcomp = t5.reshape(tc, tc, SSM_QT, gpt, hh, hh).transpose(2, 0, 3, 5, 1, 4)
    m3u = widen(tcomp.reshape(SSM_QT, tc * LANES, tc * hh), x_tok, hh, hh)

    inf = pw_desc[1:, 0][..., None] * b_bar[0][None]
    inr = pw[:tc, 1][..., None] * b_bar[1][None]
    m1 = jnp.stack([jnp.real(inf), jnp.imag(inf), jnp.real(inr), jnp.imag(inr)], axis=0)
    m1 = m1.reshape(2, 2, tc, SSM_QT, gpt, p, hh).transpose(3, 2, 4, 6, 0, 1, 5)
    m1 = widen(m1.reshape(SSM_QT, tc * LANES, 4 * p), x_st, hh, p)

    def out_mat(d, powers):
        ca = cm[d][None] * powers[:, :, None, :]
        mm = jnp.stack([jnp.real(ca), -jnp.imag(ca)], axis=0)
        mm = mm.reshape(2, tc, SSM_QT, gpt, hh, p).transpose(2, 0, 3, 5, 1, 4)
        return widen(mm.reshape(SSM_QT, 2 * SSM_ST, tc * hh), x_tok, p, hh)

    m3f = out_mat(0, pw[1:, 0])
    m3r = out_mat(1, pw_desc[:tc, 1])

    def decay(d):
        a16 = pw[tc, d].reshape(SSM_QT, SSM_ST)
        return jnp.concatenate([jnp.real(a16), jnp.imag(a16)], axis=1).reshape(1, SSM_QT * 2 * SSM_ST)

    return m1.astype(BF), m3u.astype(BF), m3f.astype(BF), m3r.astype(BF), decay(0), decay(1)


def s5_branch(uq4, mats, layer, n_lat_chunks):
    m1, m3u, m3f, m3r, a16f, a16r = mats
    zf, zr = ssm_in(uq4, m1, layer)
    sf, sr = ssm_scan(zf, zr, a16f, a16r, layer, n_lat_chunks)
    return ssm_out(uq4, sf, sr, m3u, m3f, m3r, layer)


def _gqa_kernel(q_ref, k_ref, vt_ref, vtc_ref, o_ref, qs_ref, m_ref, l_ref, acc_ref, sa_ref, sb_ref, sctx_ref, *,
                n_lat, n_ctx, tk, n_lat_tiles):
    tq = q_ref.shape[1]
    qi = pl.program_id(1)
    n_slots = GQA_HEADS
    lane_half = lax.broadcasted_iota(jnp.int32, (tq, LANES), 1) // HEAD_DIM
    for t in range(GQA_Q_W // LANES):
        qt = q_ref[0, :, t * LANES:(t + 1) * LANES]
        for g in range(GQA_KV_HEADS):
            qs_ref[2 * t + g] = jnp.where(lane_half == g, qt, jnp.zeros((), qt.dtype))
    m_ref[...] = jnp.full(m_ref.shape, NEG, F32)
    l_ref[...] = jnp.zeros(l_ref.shape, F32)
    acc_ref[...] = jnp.zeros(acc_ref.shape, F32)

    n_chunks = n_lat // tk

    def scores(kc, s):
        return _dot_nt(kc, qs_ref[s])

    def lat_keys(c):
        return k_ref[0, pl.ds(pl.multiple_of(c * tk, tk), tk), :]

    def update(st, s, vtc):
        cs = slice(s * tq, (s + 1) * tq)
        m_old = m_ref[:, cs]
        m_new = jnp.maximum(m_old, jnp.max(st, axis=0, keepdims=True))
        alpha = jnp.exp2(m_old - m_new)
        p = jnp.exp2(st - m_new)
        l_ref[:, cs] = alpha * l_ref[:, cs] + jnp.sum(p, axis=0, keepdims=True)
        acc_ref[:, cs] = alpha * acc_ref[:, cs] + _dot(vtc, p.astype(BF))
        m_ref[:, cs] = m_new

    kcc = k_ref[0, n_lat:n_lat + n_ctx, :]
    for s in range(n_slots):
        sctx_ref[s] = scores(kcc, s)

    @pl.when(qi < n_lat_tiles)
    def _():
        bufs = (sa_ref, sb_ref)
        kc0 = lat_keys(0)
        for s in range(n_slots):
            sa_ref[s] = scores(kc0, s)
            update(sctx_ref[s], s, vtc_ref[0])

        def step(c, cur, nxt):
            kn = lat_keys(c + 1)
            for s in range(n_slots):
                nxt[s] = scores(kn, s)
                update(cur[s], s, vt_ref[0, c])

        def pair(i, carry):
            step(2 * i, sa_ref, sb_ref)
            step(2 * i + 1, sb_ref, sa_ref)
            return carry

        n_steps = n_chunks - 1
        lax.fori_loop(0, n_steps // 2, pair, 0)
        if n_steps % 2:
            step(n_steps - 1, sa_ref, sb_ref)
        last = bufs[n_steps % 2]
        for s in range(n_slots):
            update(last[s], s, vt_ref[0, n_chunks - 1])

    @pl.when(qi >= n_lat_tiles)
    def _():
        for s in range(n_slots):
            update(sctx_ref[s], s, vtc_ref[0])

    o = acc_ref[...] / l_ref[...]
    row_half = lax.broadcasted_iota(jnp.int32, (LANES, tq), 0) // HEAD_DIM
    for t in range(GQA_Q_W // LANES):
        o0 = o[:, (2 * t) * tq:(2 * t + 1) * tq]
        o1 = o[:, (2 * t + 1) * tq:(2 * t + 2) * tq]
        o_ref[0, :, t * LANES:(t + 1) * LANES] = jnp.where(row_half == 0, o0, o1).T.astype(o_ref.dtype)


def gqa_attention(qg, kg, vg, n_lat, n_q_tiles):
    b, s, _ = qg.shape
    tq = TOKEN_TILE
    tk = 512
    n_ctx = s - n_lat
    nc = n_lat // tk
    vt = jnp.swapaxes(vg[:, :n_lat].reshape(b, nc, tk, LANES), 2, 3)
    vtc = jnp.swapaxes(vg[:, n_lat:], 1, 2)
    kern = functools.partial(_gqa_kernel, n_lat=n_lat, n_ctx=n_ctx, tk=tk, n_lat_tiles=n_lat // tq)
    return pl.pallas_call(
        kern, grid=(b, n_q_tiles),
        in_specs=[pl.BlockSpec((1, tq, GQA_Q_W), lambda bi, i: (bi, i, 0)),
                  pl.BlockSpec((1, s, LANES), lambda bi, i: (bi, 0, 0)),
                  pl.BlockSpec((1, nc, LANES, tk), lambda bi, i: (bi, 0, 0, 0)),
                  pl.BlockSpec((1, LANES, n_ctx), lambda bi, i: (bi, 0, 0))],
        out_specs=pl.BlockSpec((1, tq, GQA_Q_W), lambda bi, i: (bi, i, 0)),
        out_shape=jax.ShapeDtypeStruct((b, n_q_tiles * tq, GQA_Q_W), BF),
        scratch_shapes=[pltpu.VMEM((GQA_HEADS, tq, LANES), BF),
                        pltpu.VMEM((1, GQA_HEADS * tq), F32),
                        pltpu.VMEM((1, GQA_HEADS * tq), F32),
                        pltpu.VMEM((LANES, GQA_HEADS * tq), F32),
                        pltpu.VMEM((GQA_HEADS, tk, tq), F32),
                        pltpu.VMEM((GQA_HEADS, tk, tq), F32),
                        pltpu.VMEM((GQA_HEADS, n_ctx, tq), F32)],
        compiler_params=_cp(("arbitrary", "arbitrary")), name="gqa",
    )(qg, kg, vt, vtc)


def _na_kernel(q_ref, k_ref, v_ref, tab_ref, o_ref, *, n_lat, n_ctx, rows):
    rb = pl.program_id(1)
    ws = jnp.clip(rb * NA_QROWS - NA_WIN_R // 2, 0, rows - NA_UNION)
    off = pl.multiple_of(ws * GRID_W, GRID_W)
    nwin = NA_UNION * GRID_W
    def window_and_context(ref, pair):
        ls = slice(pair * LANES, (pair + 1) * LANES)
        return jnp.concatenate([ref[0, pl.ds(off, nwin), ls], ref[0, n_lat:n_lat + n_ctx, ls]], axis=0)

    lane_half = lax.broadcasted_iota(jnp.int32, (q_ref.shape[1], LANES), 1) // HEAD_DIM

    def scores(h):
        kcat = window_and_context(k_ref, h // 2)
        qt = q_ref[0, :, (h // 2) * LANES:(h // 2 + 1) * LANES]
        qh = jnp.where(lane_half == h % 2, qt, jnp.zeros((), qt.dtype))
        return _dot_nt(qh, kcat) + tab_ref[0, 0, h]

    s_next = scores(0)
    o_pair = [None, None]
    for h in range(NA_HEADS):
        s = s_next
        if h + 1 < NA_HEADS:
            s_next = scores(h + 1)
        m = jnp.max(s, axis=-1, keepdims=True)
        p = jnp.exp2(s - m)
        l = jnp.sum(p, axis=-1, keepdims=True)
        o_pair[h % 2] = _dot(p.astype(BF), window_and_context(v_ref, h // 2)) / l
        if h % 2:
            ts = slice((h // 2) * LANES, (h // 2 + 1) * LANES)
            o_ref[0, :, ts] = jnp.where(lane_half == 0, o_pair[0], o_pair[1]).astype(o_ref.dtype)


def na_attention(qn, kn, vn, table, layer, n_lat, n_q_tiles):
    b, s, _ = qn.shape
    tq = NA_QROWS * GRID_W
    rows = n_lat // GRID_W
    n_lat_tiles = n_lat // tq
    tw = table.shape[-1]

    def tab_idx(bi, rb):
        case = jnp.where(rb == 0, 0, jnp.where(rb < n_lat_tiles - 1, 1, jnp.where(rb == n_lat_tiles - 1, 2, 3)))
        return (layer, case, 0, 0, 0)

    kern = functools.partial(_na_kernel, n_lat=n_lat, n_ctx=s - n_lat, rows=rows)
    return pl.pallas_call(
        kern, grid=(b, n_q_tiles),
        in_specs=[pl.BlockSpec((1, tq, NA_W), lambda bi, rb: (bi, rb, 0)),
                  pl.BlockSpec((1, s, NA_W), lambda bi, rb: (bi, 0, 0)),
                  pl.BlockSpec((1, s, NA_W), lambda bi, rb: (bi, 0, 0)),
                  pl.BlockSpec((1, 1, NA_HEADS, tq, tw), tab_idx)],
        out_specs=pl.BlockSpec((1, tq, NA_W), lambda bi, rb: (bi, rb, 0)),
        out_shape=jax.ShapeDtypeStruct((b, n_q_tiles * tq, NA_W), BF),
        compiler_params=_cp(("arbitrary", "arbitrary"), 56), name="na",
    )(qn, kn, vn, table)


def na_table(rpb, rows, n_ctx):
    tq = NA_QROWS * GRID_W
    qr = jnp.arange(NA_QROWS)[:, None, None, None]
    qc = jnp.arange(GRID_W)[None, :, None, None]
    kr = jnp.arange(NA_UNION)[None, None, :, None]
    kc = jnp.arange(GRID_W)[None, None, None, :]
    cs = jnp.clip(qc - NA_WIN_C // 2, 0, GRID_W - NA_WIN_C)
    col_ok = (kc >= cs) & (kc < cs + NA_WIN_C)
    col_idx = (kc - qc + NA_WIN_C - 1)[:, :, 0, :]
    col_hot = (col_idx[0, :, :, None] == jnp.arange(2 * NA_WIN_C - 1)).astype(F32)
    n_blocks = rows // NA_QROWS
    hi = lax.Precision.HIGHEST
    tabs = []
    for blk in (0, 1, n_blocks - 1):
        r = blk * NA_QROWS + qr
        ws = min(max(blk * NA_QROWS - NA_WIN_R // 2, 0), rows - NA_UNION)
        rs = jnp.clip(r - NA_WIN_R // 2, 0, rows - NA_WIN_R)
        ka = ws + kr
        ok = (ka >= rs) & (ka < rs + NA_WIN_R) & col_ok
        row_idx = (ka - r + NA_WIN_R - 1)[:, 0, :, 0]
        row_hot = (row_idx[:, :, None] == jnp.arange(2 * NA_WIN_R - 1)).astype(F32)
        by_row = jnp.einsum('hab,rka->hrkb', rpb.astype(F32), row_hot, precision=hi)
        bias = jnp.einsum('hrkb,qcb->hrqkc', by_row, col_hot, precision=hi)
        tabs.append(jnp.where(ok[None], bias, NEG).reshape(NA_HEADS, tq, NA_UNION * GRID_W))
    tabs.append(jnp.full((NA_HEADS, tq, NA_UNION * GRID_W), NEG, F32))
    win = jnp.stack(tabs, axis=0)
    return jnp.concatenate([win * LOG2E, jnp.zeros(win.shape[:3] + (n_ctx,), F32)], axis=-1)


def _merge_kernel(xl_ref, xc_ref, y_ref, og_ref, on_ref, gt_ref, mod_ref, gw_ref, gb_ref, ws_ref, wg_ref, wn_ref,
                  wo_ref, o_ref, *, n_lat_tiles):
    d = xl_ref.shape[-1]
    y = jnp.concatenate([y_ref[q, 0].astype(F32) for q in range(SSM_QT)], axis=1)
    gy = _gelu_tanh(y)
    ys = gy * jax.nn.sigmoid(_dot(gy.astype(BF), gw_ref[...]) + gb_ref[...])
    gt = gt_ref[0]
    m = gt[:, :d].astype(F32) * _dot(ys.astype(BF), ws_ref[...])
    m += gt[:, d:2 * d].astype(F32) * _dot(og_ref[0], wg_ref[...])
    m += gt[:, 2 * d:].astype(F32) * _dot(on_ref[0], wn_ref[...])
    x = _stream_tile(xl_ref, xc_ref, n_lat_tiles)
    o_ref[0] = x + mod_ref[0][2:3] * _dot(m.astype(BF), wo_ref[...])


def merge(x_lat, x_ctx, y4, og, on, gt, mod, gw, gb, ws, wg, wn, wo, n_tiles, n_lat_tiles):
    b, _, d = x_lat.shape
    tm = TOKEN_TILE
    nb = mod.shape[0] - 1

    def row(bi, j):
        return (bi, j, 0)

    return pl.pallas_call(
        functools.partial(_merge_kernel, n_lat_tiles=n_lat_tiles), grid=(b, n_tiles),
        in_specs=_stream_specs(x_lat, x_ctx, n_lat_tiles) + [
                  pl.BlockSpec((SSM_QT, 1, tm, LANES), lambda bi, j: (0, bi, j, 0)),
                  pl.BlockSpec((1, tm, GQA_Q_W), row),
                  pl.BlockSpec((1, tm, NA_W), row),
                  pl.BlockSpec((1, tm, 3 * d), row),
                  pl.BlockSpec((1, 6, d), lambda bi, j: (jnp.where(j < n_lat_tiles, bi, nb), 0, 0)),
                  _const_spec(gw.shape), _const_spec(gb.shape), _const_spec(ws.shape),
                  _const_spec(wg.shape), _const_spec(wn.shape), _const_spec(wo.shape)],
        out_specs=pl.BlockSpec((1, tm, d), row),
        out_shape=jax.ShapeDtypeStruct((b, n_tiles * tm, d), F32),
        compiler_params=_cp(("arbitrary", "arbitrary")), name="merge",
    )(x_lat, x_ctx, y4, og, on, gt, mod, gw, gb, ws, wg, wn, wo)


def _ffn_kernel(x_ref, mod_ref, g_ref, wg_ref, wu_ref, wd_ref, o_ref):
    x = x_ref[0]
    mod = mod_ref[0]
    hb = (_rms(x, g_ref[...]) * (1.0 + mod[4:5]) + mod[3:4]).astype(BF)
    a = _silu(_dot(hb, wg_ref[...])) * _dot(hb, wu_ref[...])
    o_ref[0] = x + mod[5:6] * _dot(a.astype(BF), wd_ref[...])


def ffn(xall, mod, g, wg, wu, wd, n_lat_tiles):
    b, s, d = xall.shape
    tm = TOKEN_TILE
    nb = mod.shape[0] - 1

    def row(bi, j):
        return (bi, j, 0)

    return pl.pallas_call(
        _ffn_kernel, grid=(b, s // tm),
        in_specs=[pl.BlockSpec((1, tm, d), row),
                  pl.BlockSpec((1, 6, d), lambda bi, j: (jnp.where(j < n_lat_tiles, bi, nb), 0, 0)),
                  _const_spec(g.shape), _const_spec(wg.shape), _const_spec(wu.shape), _const_spec(wd.shape)],
        out_specs=pl.BlockSpec((1, tm, d), row),
        out_shape=jax.ShapeDtypeStruct((b, s, d), F32),
        compiler_params=_cp(("arbitrary", "arbitrary"), 56), name="ffn",
    )(xall, mod, g, wg, wu, wd)


def _route_kernel(x_ref, mod_ref, g_ref, rw_ref, h_ref, info_ref):
    x = x_ref[0]
    mod = mod_ref[0]
    h = _rms(x, g_ref[...]) * (1.0 + mod[4:5]) + mod[3:4]
    h_ref[0] = h
    logits = jnp.dot(h, rw_ref[...], preferred_element_type=F32, precision=lax.Precision.HIGHEST)
    lane = lax.broadcasted_iota(jnp.int32, logits.shape, 1)
    lanef = lane.astype(F32)
    logits = jnp.where(lane < N_EXPERTS, logits, -jnp.inf)
    m1 = jnp.max(logits, axis=-1, keepdims=True)
    i1 = jnp.min(jnp.where(logits == m1, lanef, float(LANES)), axis=-1, keepdims=True)
    rest = jnp.where(lanef == i1, -jnp.inf, logits)
    m2 = jnp.max(rest, axis=-1, keepdims=True)
    i2 = jnp.min(jnp.where(rest == m2, lanef, float(LANES)), axis=-1, keepdims=True)
    e2 = jnp.exp(m2 - m1)
    w1 = 1.0 / (1.0 + e2)
    w2 = e2 / (1.0 + e2)
    info_ref[0] = jnp.where(lane == 0, i1, jnp.where(lane == 1, i2, jnp.where(lane == 2, w1,
                            jnp.where(lane == 3, w2, 0.0))))


def moe_route(xall, mod, g, rw, n_tiles):
    b, s, d = xall.shape
    tm = TOKEN_TILE

    def row(bi, j):
        return (bi, j, 0)

    return pl.pallas_call(
        _route_kernel, grid=(b, n_tiles),
        in_specs=[pl.BlockSpec((1, tm, d), row), pl.BlockSpec((1, 6, d), lambda bi, j: (bi, 0, 0)),
                  _const_spec(g.shape), _const_spec(rw.shape)],
        out_specs=[pl.BlockSpec((1, tm, d), row), pl.BlockSpec((1, tm, LANES), row)],
        out_shape=[jax.ShapeDtypeStruct((b, n_tiles * tm, d), F32),
                   jax.ShapeDtypeStruct((b, n_tiles * tm, LANES), F32)],
        compiler_params=_cp(("arbitrary", "arbitrary")), name="moe_route",
    )(xall, mod, g, rw)


def _dispatch_kernel(pos_ref, h_ref, xs_in_ref, xs_ref, sem):
    del xs_in_ref
    tm = h_ref.shape[0]

    def row_copy(r, k):
        return pltpu.make_async_copy(h_ref.at[pl.ds(r, 1)], xs_ref.at[pl.ds(pos_ref[0, 0, 2 * r + k], 1)], sem)

    def issue(r, c):
        row_copy(r, 0).start(priority=0)
        row_copy(r, 1).start(priority=1)
        return c

    def drain(r, c):
        row_copy(r, 0).wait()
        row_copy(r, 1).wait()
        return c

    lax.fori_loop(0, tm, issue, 0, unroll=4)
    lax.fori_loop(0, tm, drain, 0)


def moe_dispatch(h2, pos, n_slots):
    t, d = h2.shape
    tm = MOE_ROW_TILE
    zeros = jnp.zeros((n_slots, d), F32)
    return pl.pallas_call(
        _dispatch_kernel, grid=(t // tm,),
        in_specs=[pl.BlockSpec((1, 1, 2 * tm), lambda i: (i, 0, 0), memory_space=pltpu.SMEM),
                  pl.BlockSpec((tm, d), lambda i: (i, 0)),
                  pl.BlockSpec(memory_space=pl.ANY)],
        out_specs=pl.BlockSpec(memory_space=pl.ANY),
        out_shape=jax.ShapeDtypeStruct((n_slots, d), F32),
        scratch_shapes=[pltpu.SemaphoreType.DMA(())],
        input_output_aliases={2: 0},
        compiler_params=_cp(("arbitrary",)), name="moe_dispatch",
    )(pos, h2, zeros)


def _experts_kernel(te_ref, nv_ref, x_ref, wg_ref, wu_ref, wd_ref, o_ref, acc_ref):
    i = pl.program_id(0)
    j = pl.program_id(1)

    @pl.when(i < nv_ref[0])
    def _():
        xb = x_ref[...].astype(BF)
        a = _silu(_dot(xb, wg_ref[0])) * _dot(xb, wu_ref[0])
        part = _dot(a.astype(BF), wd_ref[0])

        @pl.when(j == 0)
        def _():
            acc_ref[...] = part

        @pl.when(j > 0)
        def _():
            acc_ref[...] += part

        @pl.when(j == pl.num_programs(1) - 1)
        def _():
            o_ref[...] = acc_ref[...]

    @pl.when((i >= nv_ref[0]) & (j == pl.num_programs(1) - 1))
    def _():
        o_ref[...] = jnp.zeros(o_ref.shape, o_ref.dtype)


def moe_experts(xs, tile_expert, n_valid, wg, wu, wd):
    n_slots, d = xs.shape
    tm = MOE_TILE
    tf = MOE_FCHUNK
    f = wg.shape[-1]
    nf = f // tf

    def xrow(i, j, te, nv):
        return (jnp.minimum(i, nv[0] - 1), 0)

    def fcol(i, j, nv):
        return jnp.where(i < nv[0], j, nf - 1)

    grid_spec = pltpu.PrefetchScalarGridSpec(
        num_scalar_prefetch=2, grid=(n_slots // tm, nf),
        in_specs=[pl.BlockSpec((tm, d), xrow),
                  pl.BlockSpec((1, d, tf), lambda i, j, te, nv: (te[i], 0, fcol(i, j, nv))),
                  pl.BlockSpec((1, d, tf), lambda i, j, te, nv: (te[i], 0, fcol(i, j, nv))),
                  pl.BlockSpec((1, tf, d), lambda i, j, te, nv: (te[i], fcol(i, j, nv), 0))],
        out_specs=pl.BlockSpec((tm, d), lambda i, j, te, nv: (i, 0)),
        scratch_shapes=[pltpu.VMEM((tm, d), F32)])
    return pl.pallas_call(
        _experts_kernel, grid_spec=grid_spec,
        out_shape=jax.ShapeDtypeStruct((n_slots, d), F32),
        compiler_params=_cp(("arbitrary", "arbitrary"), 56), name="moe_experts",
    )(tile_expert, n_valid, xs, wg, wu, wd)


def _combine_kernel(pos_ref, x_ref, info_ref, mod_ref, fg_ref, ys_ref, o_ref, y1_ref, y2_ref, sem):
    tm = x_ref.shape[1]

    def row_copy(r, k, dst):
        return pltpu.make_async_copy(ys_ref.at[pl.ds(pos_ref[0, 0, 2 * r + k], 1)], dst.at[pl.ds(r, 1)], sem)

    def issue(r, c):
        row_copy(r, 0, y1_ref).start(priority=0)
        row_copy(r, 1, y2_ref).start(priority=1)
        return c

    def drain(r, c):
        row_copy(r, 0, y1_ref).wait()
        row_copy(r, 1, y2_ref).wait()
        return c

    lax.fori_loop(0, tm, issue, 0, unroll=4)
    lax.fori_loop(0, tm, drain, 0)
    info = info_ref[0]
    y = info[:, 2:3] * y1_ref[...] + info[:, 3:4] * y2_ref[...]
    xn = x_ref[0] + mod_ref[0][5:6] * y
    o_ref[0] = _rms(xn, fg_ref[...])


def moe_combine(x, info, mod, fg, ys, pos, n_tiles):
    b, s, d = x.shape
    tm = MOE_ROW_TILE

    def row(bi, j):
        return (bi, j, 0)

    return pl.pallas_call(
        _combine_kernel, grid=(b, n_tiles),
        in_specs=[pl.BlockSpec((1, 1, 2 * tm), lambda bi, j: (bi * n_tiles + j, 0, 0), memory_space=pltpu.SMEM),
                  pl.BlockSpec((1, tm, d), row), pl.BlockSpec((1, tm, LANES), row),
                  pl.BlockSpec((1, 6, d), lambda bi, j: (bi, 0, 0)), _const_spec(fg.shape),
                  pl.BlockSpec(memory_space=pl.ANY)],
        out_specs=pl.BlockSpec((1, tm, d), row),
        out_shape=jax.ShapeDtypeStruct((b, n_tiles * tm, d), F32),
        scratch_shapes=[pltpu.VMEM((tm, d), F32), pltpu.VMEM((tm, d), F32), pltpu.SemaphoreType.DMA(())],
        compiler_params=_cp(("arbitrary", "arbitrary")), name="moe_combine",
    )(pos, x, info, mod, fg, ys)


def moe_layer(x, mod, g2, rw, wg, wu, wd, fg, n_lat):
    b, s, d = x.shape
    n_tiles = n_lat // TOKEN_TILE
    t = b * n_lat
    rw_pad = jnp.zeros((d, LANES), F32).at[:, :N_EXPERTS].set(rw.astype(F32))
    h2, info = moe_route(x, mod, g2, rw_pad, n_tiles)
    info2 = info.reshape(t, LANES)

    e_pair = info2[:, :2].astype(jnp.int32).reshape(2 * t)
    onehot = (e_pair[:, None] == jnp.arange(N_EXPERTS)[None, :]).astype(jnp.int32)
    csum = jnp.cumsum(onehot, axis=0)
    rank = jnp.sum((csum - onehot) * onehot, axis=1)
    counts = csum[-1]
    tiles_e = (counts + MOE_TILE - 1) // MOE_TILE
    tile_end = jnp.cumsum(tiles_e)
    slot_off = (tile_end - tiles_e) * MOE_TILE
    pos = (slot_off[e_pair] + rank).astype(jnp.int32).reshape(t // MOE_ROW_TILE, 1, 2 * MOE_ROW_TILE)
    n_tiles_max = (2 * t) // MOE_TILE + N_EXPERTS
    n_valid = tile_end[-1:].astype(jnp.int32)
    tile_ids = jnp.minimum(jnp.arange(n_tiles_max), n_valid[0] - 1)
    tile_expert = jnp.sum((tile_ids[:, None] >= tile_end[None, :]).astype(jnp.int32), axis=1).astype(jnp.int32)

    xs = moe_dispatch(h2.reshape(t, d), pos, n_tiles_max * MOE_TILE)
    ys = moe_experts(xs, tile_expert, n_valid, wg, wu, wd)
    return moe_combine(x, info, mod, fg, ys, pos, n_lat // MOE_ROW_TILE)


def _rope_tables(n_lat, n_ctx):
    t = jnp.arange(n_lat)
    pos = jnp.stack([t // GRID_W, t % GRID_W], axis=-1).astype(F32)
    half = HEAD_DIM // 2
    inv = 1.0 / (ROPE_BASE ** (jnp.arange(0, half, 2, dtype=F32) / half))
    ang = pos[:, :, None] * inv
    ang = jnp.concatenate([ang, ang], axis=-1).reshape(n_lat, HEAD_DIM)
    cos = jnp.concatenate([jnp.cos(ang), jnp.ones((n_ctx, HEAD_DIM), F32)], axis=0)
    sin = jnp.concatenate([jnp.sin(ang), jnp.zeros((n_ctx, HEAD_DIM), F32)], axis=0)
    return jnp.tile(cos, (1, 2)), jnp.tile(sin, (1, 2))


_GQA_HEAD_ORDER = tuple(g * GQA_GROUP + t for t in range(GQA_GROUP) for g in range(GQA_KV_HEADS))


def _gqa_cols(w):
    return jnp.concatenate([w[:, h * HEAD_DIM:(h + 1) * HEAD_DIM] for h in _GQA_HEAD_ORDER], axis=1)


def _gqa_rows(w):
    return jnp.concatenate([w[h * HEAD_DIM:(h + 1) * HEAD_DIM] for h in _GQA_HEAD_ORDER], axis=0)


def kernel(x, c, ctx, c_ctx, w_mod, b_mod, norm1_g, w_in, ssm_a_re, ssm_a_im, ssm_b_re, ssm_b_im, ssm_c_re,
           ssm_c_im, ssm_log_dt, ssm_d, glu_w, glu_b, q_norm_g, k_norm_g, na_rpb, w_branch_ssm, w_branch_gqa,
           w_branch_na, w_out, norm2_g, ffn_w_gate, ffn_w_up, ffn_w_down, router_w, moe_w_gate, moe_w_up,
           moe_w_down, final_norm_g):
    b, n_lat, d = x.shape
    n_ctx = ctx.shape[1]
    s = n_lat + n_ctx
    depth = w_mod.shape[0]
    assert d == D_MODEL and n_lat % (NA_QROWS * GRID_W) == 0 and n_ctx == TOKEN_TILE and n_lat % MOE_ROW_TILE == 0
    assert depth == 2 and n_lat // GRID_W >= NA_UNION
    n_lat_tiles = n_lat // TOKEN_TILE
    n_all_tiles = s // TOKEN_TILE

    n_rows = -(-(b + 1) // 8) * 8
    cv = jnp.zeros((n_rows, d), F32).at[:b].set(c).at[b].set(c_ctx)
    mods = adaln(cv, w_mod, b_mod)[:, :b + 1].reshape(depth, b + 1, 6, d)

    cos, sin = _rope_tables(n_lat, n_ctx)
    table_shape_rows = n_lat // GRID_W
    x_lat, x_ctx = x, ctx

    mats = jax.vmap(s5_matrices)(ssm_a_re, ssm_a_im, ssm_b_re, ssm_b_im, ssm_c_re, ssm_c_im, ssm_log_dt, ssm_d)
    tables = jax.vmap(lambda r: na_table(r, table_shape_rows, n_ctx))(na_rpb)

    out = None
    for i in range(depth):
        last = i == depth - 1
        wi = w_in[i]
        c0 = SSM_WIDTH
        wu = wi[:, :c0].astype(BF)
        wkv = wi[:, c0:KV_COLS].astype(BF)
        wqg = _gqa_cols(wi[:, KV_COLS:KV_COLS + GQA_Q_W]).astype(BF)
        wqn = wi[:, KV_COLS + GQA_Q_W:KV_COLS + GQA_Q_W + NA_W].astype(BF)
        wgt = wi[:, KV_COLS + GQA_Q_W + NA_W:].astype(BF)
        qgain = jnp.tile(q_norm_g[i].astype(F32), 2)[None, :]
        kgain = jnp.tile(k_norm_g[i].astype(F32), GQA_KV_HEADS)[None, :]

        uq, kg, vg, kn, vn, qg, qn, gt = inproj(x_lat, x_ctx, mods[i], norm1_g[i][None, :], cos, sin, wu, wkv,
                                                wqg, wqn, wgt, qgain, kgain, n_lat_tiles)

        y4 = s5_branch(uq, mats, i, n_lat // SSM_CHUNK)

        n_q_tiles = n_lat_tiles if last else n_all_tiles
        og = gqa_attention(qg, kg, vg, n_lat, n_q_tiles)
        on = na_attention(qn, kn, vn, tables, i, n_lat, n_q_tiles)

        xall = merge(x_lat, x_ctx, y4, og, on, gt, mods[i], glu_w[i].astype(BF), glu_b[i][None, :].astype(F32),
                     w_branch_ssm[i].astype(BF), _gqa_rows(w_branch_gqa[i]).astype(BF),
                     w_branch_na[i].astype(BF), w_out[i].astype(BF), n_q_tiles, n_lat_tiles)

        j = i // 2
        if i % 2 == 0:
            xall = ffn(xall, mods[i], norm2_g[i][None, :], ffn_w_gate[j].astype(BF), ffn_w_up[j].astype(BF),
                       ffn_w_down[j].astype(BF), n_lat_tiles)
            x_lat = x_ctx = xall
        else:
            out = moe_layer(xall, mods[i], norm2_g[i][None, :], router_w[j], moe_w_gate[j].astype(BF),
                            moe_w_up[j].astype(BF), moe_w_down[j].astype(BF), final_norm_g[None, :], n_lat)
    return out
```

```python
import functools
import math

import jax
import jax.numpy as jnp
from jax import lax
from jax.experimental import pallas as pl
from jax.experimental.pallas import tpu as pltpu

D_MODEL = 1024
GRID_W = 64
SSM_WIDTH = 512
SSM_GROUP = 16
SSM_GROUPS = SSM_WIDTH // SSM_GROUP
SSM_STATE = 64
HEAD_DIM = 64
GQA_HEADS = 8
GQA_KV_HEADS = 2
GQA_GROUP = GQA_HEADS // GQA_KV_HEADS
ROPE_BASE = 10000.0
NA_HEADS = 8
NA_WIN_R = 8
NA_WIN_C = 16
ATTN_SCALE = HEAD_DIM ** -0.5
LOG2E = math.log2(math.e)
GQA_Q_W = GQA_HEADS * HEAD_DIM
GQA_KV_W = GQA_KV_HEADS * HEAD_DIM
NA_W = NA_HEADS * HEAD_DIM
KV_COLS = SSM_WIDTH + 2 * GQA_KV_W + 2 * NA_W
N_EXPERTS = 8
NORM_EPS = 1e-6

LANES = 128
TOKEN_TILE = 256
SSM_CHUNK = 16
SSM_QT = SSM_WIDTH // LANES
SSM_GPT = LANES // SSM_GROUP
SSM_ST = SSM_GPT * SSM_STATE
NA_QROWS = 4
NA_UNION = 12
MOE_TILE = 512
MOE_ROW_TILE = 1024
MOE_FCHUNK = 1792
NEG = -1e30

BF = jnp.bfloat16
F32 = jnp.float32


def _cp(sem, vmem_mb=48):
    return pltpu.CompilerParams(dimension_semantics=sem, vmem_limit_bytes=vmem_mb * 1024 * 1024)


def _const_spec(shape):
    nd = len(shape)
    return pl.BlockSpec(shape, lambda *_: (0,) * nd, pipeline_mode=pl.Buffered(1))


def _dot(a, b):
    return jnp.dot(a, b, preferred_element_type=F32)


def _dot_nt(a, b):
    return lax.dot_general(a, b, (((1,), (1,)), ((), ())), preferred_element_type=F32)


def _rms(x, g):
    return x * lax.rsqrt(jnp.mean(x * x, axis=-1, keepdims=True) + NORM_EPS) * g


def _silu(x):
    return x * jax.nn.sigmoid(x)


def _gelu_tanh(x):
    return 0.5 * x * (1.0 + jnp.tanh(math.sqrt(2.0 / math.pi) * (x + 0.044715 * x * x * x)))


def _adaln_kernel(cv_ref, w_ref, b_ref, o_ref):
    a = _silu(cv_ref[...])
    w = w_ref[0]
    a_hi = a.astype(BF)
    a_lo = (a - a_hi.astype(F32)).astype(BF)
    w_hi = w.astype(BF)
    w_lo = (w - w_hi.astype(F32)).astype(BF)
    o_ref[0] = _dot(a_hi, w_hi) + _dot(a_hi, w_lo) + _dot(a_lo, w_hi) + b_ref[0]


def adaln(cv, w_mod, b_mod):
    depth, d, n = w_mod.shape
    r = cv.shape[0]
    tn = 512
    return pl.pallas_call(
        _adaln_kernel,
        grid=(depth, n // tn),
        in_specs=[pl.BlockSpec((r, d), lambda i, j: (0, 0)),
                  pl.BlockSpec((1, d, tn), lambda i, j: (i, 0, j)),
                  pl.BlockSpec((1, 1, tn), lambda i, j: (i, 0, j))],
        out_specs=pl.BlockSpec((1, r, tn), lambda i, j: (i, 0, j)),
        out_shape=jax.ShapeDtypeStruct((depth, r, n), F32),
        compiler_params=_cp(("arbitrary", "arbitrary")),
        name="adaln",
    )(cv, w_mod, b_mod.reshape(depth, 1, n))


def _rope(xn, cos, sin):
    lane = lax.broadcasted_iota(jnp.int32, xn.shape, 1)
    first = (lane % 32) < 16
    rot = jnp.where(first, -pltpu.roll(xn, LANES - 16, 1), pltpu.roll(xn, 16, 1))
    return xn * cos + rot * sin


def _stream_specs(x_lat, x_ctx, n_lat_tiles):
    tm = TOKEN_TILE
    d = x_lat.shape[-1]
    ctx_tile0 = x_ctx.shape[1] // tm - 1
    return [pl.BlockSpec((1, tm, d), lambda bi, j: (bi, jnp.minimum(j, n_lat_tiles - 1), 0)),
            pl.BlockSpec((1, tm, d), lambda bi, j: (bi, ctx_tile0, 0))]


def _stream_tile(xl_ref, xc_ref, n_lat_tiles):
    return jnp.where(pl.program_id(1) < n_lat_tiles, xl_ref[0], xc_ref[0])


def _inproj_kernel(xl_ref, xc_ref, mod_ref, g_ref, cos_ref, sin_ref, wu_ref, wkv_ref, wqg_ref, wqn_ref, wgt_ref,
                   qgain_ref, kgain_ref,
                   u_ref, kg_ref, vg_ref, kn_ref, vn_ref, qg_ref, qn_ref, gt_ref, *, n_lat_tiles):
    x = _stream_tile(xl_ref, xc_ref, n_lat_tiles)
    mod = mod_ref[0]
    h = _rms(x, g_ref[...]) * (1.0 + mod[1:2]) + mod[0:1]
    hb = h.astype(BF)
    cos = cos_ref[...]
    sin = sin_ref[...]

    u = _dot(hb, wu_ref[...])
    for q in range(SSM_QT):
        u_ref[q, 0] = u[:, q * LANES:(q + 1) * LANES]

    def head_pair_norm_rope(t, gain):
        lane = lax.broadcasted_iota(jnp.int32, t.shape, 1)
        lo = lane < HEAD_DIM
        sq = t * t
        ms_lo = jnp.sum(jnp.where(lo, sq, 0.0), axis=-1, keepdims=True)
        ms_hi = jnp.sum(jnp.where(lo, 0.0, sq), axis=-1, keepdims=True)
        ms = jnp.where(lo, ms_lo, ms_hi) * (1.0 / HEAD_DIM)
        return _rope(t * lax.rsqrt(ms + NORM_EPS) * gain, cos, sin)

    kv = _dot(hb, wkv_ref[...])
    kg_ref[0] = head_pair_norm_rope(kv[:, :LANES], kgain_ref[...]).astype(BF)
    vg_ref[0] = kv[:, LANES:2 * LANES].astype(BF)
    kn_ref[0] = kv[:, 2 * LANES:2 * LANES + NA_W].astype(BF)
    vn_ref[0] = kv[:, 2 * LANES + NA_W:].astype(BF)

    qg = _dot(hb, wqg_ref[...])
    for t in range(GQA_Q_W // LANES):
        ts = slice(t * LANES, (t + 1) * LANES)
        qg_ref[0, :, ts] = (head_pair_norm_rope(qg[:, ts], qgain_ref[...]) * (ATTN_SCALE * LOG2E)).astype(BF)

    qn_ref[0] = (_dot(hb, wqn_ref[...]) * (ATTN_SCALE * LOG2E)).astype(BF)
    gt_ref[0] = jax.nn.sigmoid(_dot(hb, wgt_ref[...])).astype(BF)


def inproj(x_lat, x_ctx, mod, g, cos, sin, wu, wkv, wqg, wqn, wgt, qgain, kgain, n_lat_tiles):
    b, _, d = x_lat.shape
    tm = TOKEN_TILE
    nt = n_lat_tiles + 1
    s = nt * tm
    nb = mod.shape[0] - 1

    def row(bi, j):
        return (bi, j, 0)

    def modrow(bi, j):
        return (jnp.where(j < n_lat_tiles, bi, nb), 0, 0)

    outs = [
        jax.ShapeDtypeStruct((SSM_QT, b, s, LANES), F32),
        jax.ShapeDtypeStruct((b, s, LANES), BF),
        jax.ShapeDtypeStruct((b, s, LANES), BF),
        jax.ShapeDtypeStruct((b, s, NA_W), BF),
        jax.ShapeDtypeStruct((b, s, NA_W), BF),
        jax.ShapeDtypeStruct((b, s, GQA_Q_W), BF),
        jax.ShapeDtypeStruct((b, s, NA_W), BF),
        jax.ShapeDtypeStruct((b, s, 3 * d), BF),
    ]
    out_specs = [
        pl.BlockSpec((SSM_QT, 1, tm, LANES), lambda bi, j: (0, bi, j, 0)),
        pl.BlockSpec((1, tm, LANES), row),
        pl.BlockSpec((1, tm, LANES), row),
        pl.BlockSpec((1, tm, NA_W), row),
        pl.BlockSpec((1, tm, NA_W), row),
        pl.BlockSpec((1, tm, GQA_Q_W), row),
        pl.BlockSpec((1, tm, NA_W), row),
        pl.BlockSpec((1, tm, 3 * d), row),
    ]
    in_specs = _stream_specs(x_lat, x_ctx, n_lat_tiles) + [
        pl.BlockSpec((1, 6, d), modrow),
        _const_spec(g.shape),
        pl.BlockSpec((tm, LANES), lambda bi, j: (j, 0)),
        pl.BlockSpec((tm, LANES), lambda bi, j: (j, 0)),
        _const_spec(wu.shape), _const_spec(wkv.shape), _const_spec(wqg.shape),
        _const_spec(wqn.shape), _const_spec(wgt.shape), _const_spec(qgain.shape), _const_spec(kgain.shape),
    ]
    return pl.pallas_call(
        functools.partial(_inproj_kernel, n_lat_tiles=n_lat_tiles), grid=(b, nt), in_specs=in_specs,
        out_specs=out_specs, out_shape=outs,
        compiler_params=_cp(("arbitrary", "arbitrary"), 56), name="inproj",
    )(x_lat, x_ctx, mod, g, cos, sin, wu, wkv, wqg, wqn, wgt, qgain, kgain)


def _chunk_rows(u_ref):
    nk = u_ref.shape[2] // SSM_CHUNK
    cols = [u_ref[0, 0, pl.ds(j, nk, stride=SSM_CHUNK), :] for j in range(SSM_CHUNK)]
    return jnp.concatenate(cols, axis=1).astype(BF)


def _ssm_in_kernel(u_ref, m_ref, zf_ref, zr_ref):
    z = _dot(_chunk_rows(u_ref), m_ref[0, 0])
    half = z.shape[1] // 2
    zf_ref[0] = z[:, :half]
    zr_ref[0] = z[:, half:]


def ssm_in(uq, m1, layer):
    qt, b, s, _ = uq.shape
    nk = s // SSM_CHUNK
    k = SSM_CHUNK * LANES
    w = 2 * SSM_ST
    zspec = pl.BlockSpec((1, nk, w), lambda q, bi: (bi, 0, q))
    return pl.pallas_call(
        _ssm_in_kernel, grid=(qt, b),
        in_specs=[pl.BlockSpec((1, 1, s, LANES), lambda q, bi: (q, bi, 0, 0)),
                  pl.BlockSpec((1, 1, k, 2 * w), lambda q, bi: (layer, q, 0, 0))],
        out_specs=[zspec, zspec],
        out_shape=[jax.ShapeDtypeStruct((b, nk, qt * w), F32)] * 2,
        compiler_params=_cp(("arbitrary", "arbitrary")), name="ssm_in",
    )(uq, m1)


def _ssm_scan_kernel(zf_ref, zr_ref, af_ref, ar_ref, sf_ref, sr_ref, *, n_lat_chunks):
    nk = zf_ref.shape[1]
    st = SSM_ST
    af_re, af_im = af_ref[0, :, :st], af_ref[0, :, st:]
    ar_re, ar_im = ar_ref[0, :, :st], ar_ref[0, :, st:]

    def step(i, carry):
        fre, fim, rre, rim = carry
        kf = lax.rem(i + n_lat_chunks, nk)
        kr = nk - 1 - i
        sf_ref[0, pl.ds(kf, 1), :] = jnp.concatenate([fre, fim], axis=1)
        sr_ref[0, pl.ds(kr, 1), :] = jnp.concatenate([rre, rim], axis=1)
        zf = zf_ref[0, pl.ds(kf, 1), :]
        zr = zr_ref[0, pl.ds(kr, 1), :]
        nfre = af_re * fre - af_im * fim + zf[:, :st]
        nfim = af_re * fim + af_im * fre + zf[:, st:]
        nrre = ar_re * rre - ar_im * rim + zr[:, :st]
        nrim = ar_re * rim + ar_im * rre + zr[:, st:]
        return nfre, nfim, nrre, nrim

    z0 = jnp.zeros((1, st), F32)
    lax.fori_loop(0, nk, step, (z0, z0, z0, z0))


def ssm_scan(zf, zr, af, ar, layer, n_lat_chunks):
    b, nk, n = zf.shape
    w = 2 * SSM_ST
    spec = pl.BlockSpec((1, nk, w), lambda bi, q: (bi, 0, q))
    aspec = pl.BlockSpec((1, 1, w), lambda bi, q: (layer, 0, q))
    return pl.pallas_call(
        functools.partial(_ssm_scan_kernel, n_lat_chunks=n_lat_chunks),
        grid=(b, n // w), in_specs=[spec, spec, aspec, aspec], out_specs=[spec, spec],
        out_shape=[jax.ShapeDtypeStruct((b, nk, n), F32)] * 2,
        compiler_params=_cp(("arbitrary", "arbitrary")), name="ssm_scan",
    )(zf, zr, af, ar)


def _ssm_out_kernel(u_ref, sf_ref, sr_ref, mu_ref, mf_ref, mr_ref, y_ref):
    nk = sf_ref.shape[1]
    y = _dot(_chunk_rows(u_ref), mu_ref[0, 0])
    y += _dot(sf_ref[0].astype(BF), mf_ref[0, 0])
    y += _dot(sr_ref[0].astype(BF), mr_ref[0, 0])
    for j in range(SSM_CHUNK):
        y_ref[0, 0, pl.ds(j, nk, stride=SSM_CHUNK), :] = y[:, j * LANES:(j + 1) * LANES]


def ssm_out(uq, sf, sr, m3u, m3f, m3r, layer):
    qt, b, s, _ = uq.shape
    nk = s // SSM_CHUNK
    k = SSM_CHUNK * LANES
    w = 2 * SSM_ST
    one = pl.Buffered(1)
    uspec = pl.BlockSpec((1, 1, s, LANES), lambda q, bi: (q, bi, 0, 0))
    sspec = pl.BlockSpec((1, nk, w), lambda q, bi: (bi, 0, q))
    return pl.pallas_call(
        _ssm_out_kernel, grid=(qt, b),
        in_specs=[uspec, sspec, sspec,
                  pl.BlockSpec((1, 1, k, k), lambda q, bi: (layer, q, 0, 0), pipeline_mode=one),
                  pl.BlockSpec((1, 1, w, k), lambda q, bi: (layer, q, 0, 0), pipeline_mode=one),
                  pl.BlockSpec((1, 1, w, k), lambda q, bi: (layer, q, 0, 0), pipeline_mode=one)],
        out_specs=uspec,
        out_shape=jax.ShapeDtypeStruct((qt, b, s, LANES), F32),
        compiler_params=_cp(("arbitrary", "arbitrary")), name="ssm_out",
    )(uq, sf, sr, m3u, m3f, m3r)


def s5_matrices(a_re, a_im, b_re, b_im, c_re, c_im, log_dt, d_skip):
    tc, g, p, hh = SSM_CHUNK, SSM_GROUPS, SSM_STATE, SSM_GROUP
    lam = lax.complex(a_re.astype(F32), a_im.astype(F32))
    dt = jnp.exp(log_dt.astype(F32))[..., None]
    ldt = lam * dt
    a_bar = jnp.exp(ldt)
    b_bar = ((a_bar - 1.0) / lam)[..., None] * lax.complex(b_re.astype(F32), b_im.astype(F32))
    cm = lax.complex(c_re.astype(F32), c_im.astype(F32))
    steps = jnp.arange(tc + 1, dtype=F32)[:, None, None, None]
    pw = jnp.exp(ldt[None] * steps)
    pw_desc = jnp.exp(ldt[None] * (tc - steps))
    hi = lax.Precision.HIGHEST
    gpt = SSM_GPT

    def widen(compact, col_expand, row_div, col_div):
        rows, cols = compact.shape[1], col_expand.shape[1]
        wide = jnp.einsum('qrc,cn->qrn', compact.astype(BF), col_expand, preferred_element_type=F32)
        rg = (jnp.arange(rows) // row_div) % gpt
        cg = (jnp.arange(cols) // col_div) % gpt
        return jnp.where((rg[:, None] == cg[None, :])[None], wide, 0.0).astype(BF)

    x_tok = jnp.einsum('ab,m,cd->acbmd', jnp.eye(tc, dtype=F32), jnp.ones((gpt,), F32),
                       jnp.eye(hh, dtype=F32)).reshape(tc * hh, tc * LANES).astype(BF)
    x_st = jnp.einsum('ab,m,cd->acbmd', jnp.eye(4, dtype=F32), jnp.ones((gpt,), F32),
                      jnp.eye(p, dtype=F32)).reshape(4 * p, 4 * SSM_ST).astype(BF)

    kk = jnp.real(jnp.einsum('dghp,ldgp,dgpk->dlghk', cm, pw[:tc], b_bar))
    jj = jnp.arange(tc)[None, :, None]
    ii = jnp.arange(tc)[None, None, :]
    lag = jnp.arange(tc)[:, None, None]
    shift_f = (ii - jj == lag).astype(F32)
    shift_r = (jj - ii == lag).astype(F32)
    dd = (jnp.eye(tc, dtype=F32)[:, :, None, None, None] * jnp.eye(hh, dtype=F32)[None, None, None]
          * d_skip.astype(F32).reshape(g, hh)[None, None, :, :, None])
    t5 = (jnp.einsum('lji,lghk->jighk', shift_f, kk[0], precision=hi)
          + jnp.einsum('lji,lghk->jighk', shift_r, kk[1], precision=hi) + dd)
    tcomp = t5.reshape(tc, tc, SSM_QT, gpt, hh, hh).transpose(2, 0, 3, 5, 1, 4)
    m3u = widen(tcomp.reshape(SSM_QT, tc * LANES, tc * hh), x_tok, hh, hh)

    inf = pw_desc[1:, 0][..., None] * b_bar[0][None]
    inr = pw[:tc, 1][..., None] * b_bar[1][None]
    m1 = jnp.stack([jnp.real(inf), jnp.imag(inf), jnp.real(inr), jnp.imag(inr)], axis=0)
    m1 = m1.reshape(2, 2, tc, SSM_QT, gpt, p, hh).transpose(3, 2, 4, 6, 0, 1, 5)
    m1 = widen(m1.reshape(SSM_QT, tc * LANES, 4 * p), x_st, hh, p)

    def out_mat(d, powers):
        ca = cm[d][None] * powers[:, :, None, :]
        mm = jnp.stack([jnp.real(ca), -jnp.imag(ca)], axis=0)
        mm = mm.reshape(2, tc, SSM_QT, gpt, hh, p).transpose(2, 0, 3, 5, 1, 4)
        return widen(mm.reshape(SSM_QT, 2 * SSM_ST, tc * hh), x_tok, p, hh)

    m3f = out_mat(0, pw[1:, 0])
    m3r = out_mat(1, pw_desc[:tc, 1])

    def decay(d):
        a16 = pw[tc, d].reshape(SSM_QT, SSM_ST)
        return jnp.concatenate([jnp.real(a16), jnp.imag(a16)], axis=1).reshape(1, SSM_QT * 2 * SSM_ST)

    return m1.astype(BF), m3u.astype(BF), m3f.astype(BF), m3r.astype(BF), decay(0), decay(1)


def s5_branch(uq4, mats, layer, n_lat_chunks):
    m1, m3u, m3f, m3r, a16f, a16r = mats
    zf, zr = ssm_in(uq4, m1, layer)
    sf, sr = ssm_scan(zf, zr, a16f, a16r, layer, n_lat_chunks)
    return ssm_out(uq4, sf, sr, m3u, m3f, m3r, layer)


def _gqa_kernel(q_ref, k_ref, vt_ref, vtc_ref, o_ref, qs_ref, m_ref, l_ref, acc_ref, sa_ref, sb_ref, sctx_ref, *,
                n_lat, n_ctx, tk, n_lat_tiles):
    tq = q_ref.shape[1]
    qi = pl.program_id(1)
    n_slots = GQA_HEADS
    lane_half = lax.broadcasted_iota(jnp.int32, (tq, LANES), 1) // HEAD_DIM
    for t in range(GQA_Q_W // LANES):
        qt = q_ref[0, :, t * LANES:(t + 1) * LANES]
        for g in range(GQA_KV_HEADS):
            qs_ref[2 * t + g] = jnp.where(lane_half == g, qt, jnp.zeros((), qt.dtype))
    m_ref[...] = jnp.full(m_ref.shape, NEG, F32)
    l_ref[...] = jnp.zeros(l_ref.shape, F32)
    acc_ref[...] = jnp.zeros(acc_ref.shape, F32)

    n_chunks = n_lat // tk

    def scores(kc, s):
        return _dot_nt(kc, qs_ref[s])

    def lat_keys(c):
        return k_ref[0, pl.ds(pl.multiple_of(c * tk, tk), tk), :]

    def update(st, s, vtc):
        cs = slice(s * tq, (s + 1) * tq)
        m_old = m_ref[:, cs]
        m_new = jnp.maximum(m_old, jnp.max(st, axis=0, keepdims=True))
        alpha = jnp.exp2(m_old - m_new)
        p = jnp.exp2(st - m_new)
        l_ref[:, cs] = alpha * l_ref[:, cs] + jnp.sum(p, axis=0, keepdims=True)
        acc_ref[:, cs] = alpha * acc_ref[:, cs] + _dot(vtc, p.astype(BF))
        m_ref[:, cs] = m_new

    kcc = k_ref[0, n_lat:n_lat + n_ctx, :]
    for s in range(n_slots):
        sctx_ref[s] = scores(kcc, s)

    @pl.when(qi < n_lat_tiles)
    def _():
        bufs = (sa_ref, sb_ref)
        kc0 = lat_keys(0)
        for s in range(n_slots):
            sa_ref[s] = scores(kc0, s)
            update(sctx_ref[s], s, vtc_ref[0])

        def step(c, cur, nxt):
            kn = lat_keys(c + 1)
            for s in range(n_slots):
                nxt[s] = scores(kn, s)
                update(cur[s], s, vt_ref[0, c])

        def pair(i, carry):
            step(2 * i, sa_ref, sb_ref)
            step(2 * i + 1, sb_ref, sa_ref)
            return carry

        n_steps = n_chunks - 1
        lax.fori_loop(0, n_steps // 2, pair, 0)
        if n_steps % 2:
            step(n_steps - 1, sa_ref, sb_ref)
        last = bufs[n_steps % 2]
        for s in range(n_slots):
            update(last[s], s, vt_ref[0, n_chunks - 1])

    @pl.when(qi >= n_lat_tiles)
    def _():
        for s in range(n_slots):
            update(sctx_ref[s], s, vtc_ref[0])

    o = acc_ref[...] / l_ref[...]
    row_half = lax.broadcasted_iota(jnp.int32, (LANES, tq), 0) // HEAD_DIM
    for t in range(GQA_Q_W // LANES):
        o0 = o[:, (2 * t) * tq:(2 * t + 1) * tq]
        o1 = o[:, (2 * t + 1) * tq:(2 * t + 2) * tq]
        o_ref[0, :, t * LANES:(t + 1) * LANES] = jnp.where(row_half == 0, o0, o1).T.astype(o_ref.dtype)


def gqa_attention(qg, kg, vg, n_lat, n_q_tiles):
    b, s, _ = qg.shape
    tq = TOKEN_TILE
    tk = 512
    n_ctx = s - n_lat
    nc = n_lat // tk
    vt = jnp.swapaxes(vg[:, :n_lat].reshape(b, nc, tk, LANES), 2, 3)
    vtc = jnp.swapaxes(vg[:, n_lat:], 1, 2)
    kern = functools.partial(_gqa_kernel, n_lat=n_lat, n_ctx=n_ctx, tk=tk, n_lat_tiles=n_lat // tq)
    return pl.pallas_call(
        kern, grid=(b, n_q_tiles),
        in_specs=[pl.BlockSpec((1, tq, GQA_Q_W), lambda bi, i: (bi, i, 0)),
                  pl.BlockSpec((1, s, LANES), lambda bi, i: (bi, 0, 0)),
                  pl.BlockSpec((1, nc, LANES, tk), lambda bi, i: (bi, 0, 0, 0)),
                  pl.BlockSpec((1, LANES, n_ctx), lambda bi, i: (bi, 0, 0))],
        out_specs=pl.BlockSpec((1, tq, GQA_Q_W), lambda bi, i: (bi, i, 0)),
        out_shape=jax.ShapeDtypeStruct((b, n_q_tiles * tq, GQA_Q_W), BF),
        scratch_shapes=[pltpu.VMEM((GQA_HEADS, tq, LANES), BF),
                        pltpu.VMEM((1, GQA_HEADS * tq), F32),
                        pltpu.VMEM((1, GQA_HEADS * tq), F32),
                        pltpu.VMEM((LANES, GQA_HEADS * tq), F32),
                        pltpu.VMEM((GQA_HEADS, tk, tq), F32),
                        pltpu.VMEM((GQA_HEADS, tk, tq), F32),
                        pltpu.VMEM((GQA_HEADS, n_ctx, tq), F32)],
        compiler_params=_cp(("arbitrary", "arbitrary")), name="gqa",
    )(qg, kg, vt, vtc)


def _na_kernel(q_ref, k_ref, v_ref, tab_ref, o_ref, *, n_lat, n_ctx, rows):
    rb = pl.program_id(1)
    ws = jnp.clip(rb * NA_QROWS - NA_WIN_R // 2, 0, rows - NA_UNION)
    off = pl.multiple_of(ws * GRID_W, GRID_W)
    nwin = NA_UNION * GRID_W
    def window_and_context(ref, pair):
        ls = slice(pair * LANES, (pair + 1) * LANES)
        return jnp.concatenate([ref[0, pl.ds(off, nwin), ls], ref[0, n_lat:n_lat + n_ctx, ls]], axis=0)

    lane_half = lax.broadcasted_iota(jnp.int32, (q_ref.shape[1], LANES), 1) // HEAD_DIM

    def scores(h):
        kcat = window_and_context(k_ref, h // 2)
        qt = q_ref[0, :, (h // 2) * LANES:(h // 2 + 1) * LANES]
        qh = jnp.where(lane_half == h % 2, qt, jnp.zeros((), qt.dtype))
        return _dot_nt(qh, kcat) + tab_ref[0, 0, h]

    s_next = scores(0)
    o_pair = [None, None]
    for h in range(NA_HEADS):
        s = s_next
        if h + 1 < NA_HEADS:
            s_next = scores(h + 1)
        m = jnp.max(s, axis=-1, keepdims=True)
        p = jnp.exp2(s - m)
        l = jnp.sum(p, axis=-1, keepdims=True)
        o_pair[h % 2] = _dot(p.astype(BF), window_and_context(v_ref, h // 2)) / l
        if h % 2:
            ts = slice((h // 2) * LANES, (h // 2 + 1) * LANES)
            o_ref[0, :, ts] = jnp.where(lane_half == 0, o_pair[0], o_pair[1]).astype(o_ref.dtype)


def na_attention(qn, kn, vn, table, layer, n_lat, n_q_tiles):
    b, s, _ = qn.shape
    tq = NA_QROWS * GRID_W
    rows = n_lat // GRID_W
    n_lat_tiles = n_lat // tq
    tw = table.shape[-1]

    def tab_idx(bi, rb):
        case = jnp.where(rb == 0, 0, jnp.where(rb < n_lat_tiles - 1, 1, jnp.where(rb == n_lat_tiles - 1, 2, 3)))
        return (layer, case, 0, 0, 0)

    kern = functools.partial(_na_kernel, n_lat=n_lat, n_ctx=s - n_lat, rows=rows)
    return pl.pallas_call(
        kern, grid=(b, n_q_tiles),
        in_specs=[pl.BlockSpec((1, tq, NA_W), lambda bi, rb: (bi, rb, 0)),
                  pl.BlockSpec((1, s, NA_W), lambda bi, rb: (bi, 0, 0)),
                  pl.BlockSpec((1, s, NA_W), lambda bi, rb: (bi, 0, 0)),
                  pl.BlockSpec((1, 1, NA_HEADS, tq, tw), tab_idx)],
        out_specs=pl.BlockSpec((1, tq, NA_W), lambda bi, rb: (bi, rb, 0)),
        out_shape=jax.ShapeDtypeStruct((b, n_q_tiles * tq, NA_W), BF),
        compiler_params=_cp(("arbitrary", "arbitrary"), 56), name="na",
    )(qn, kn, vn, table)


def na_table(rpb, rows, n_ctx):
    tq = NA_QROWS * GRID_W
    qr = jnp.arange(NA_QROWS)[:, None, None, None]
    qc = jnp.arange(GRID_W)[None, :, None, None]
    kr = jnp.arange(NA_UNION)[None, None, :, None]
    kc = jnp.arange(GRID_W)[None, None, None, :]
    cs = jnp.clip(qc - NA_WIN_C // 2, 0, GRID_W - NA_WIN_C)
    col_ok = (kc >= cs) & (kc < cs + NA_WIN_C)
    col_idx = (kc - qc + NA_WIN_C - 1)[:, :, 0, :]
    col_hot = (col_idx[0, :, :, None] == jnp.arange(2 * NA_WIN_C - 1)).astype(F32)
    n_blocks = rows // NA_QROWS
    hi = lax.Precision.HIGHEST
    tabs = []
    for blk in (0, 1, n_blocks - 1):
        r = blk * NA_QROWS + qr
        ws = min(max(blk * NA_QROWS - NA_WIN_R // 2, 0), rows - NA_UNION)
        rs = jnp.clip(r - NA_WIN_R // 2, 0, rows - NA_WIN_R)
        ka = ws + kr
        ok = (ka >= rs) & (ka < rs + NA_WIN_R) & col_ok
        row_idx = (ka - r + NA_WIN_R - 1)[:, 0, :, 0]
        row_hot = (row_idx[:, :, None] == jnp.arange(2 * NA_WIN_R - 1)).astype(F32)
        by_row = jnp.einsum('hab,rka->hrkb', rpb.astype(F32), row_hot, precision=hi)
        bias = jnp.einsum('hrkb,qcb->hrqkc', by_row, col_hot, precision=hi)
        tabs.append(jnp.where(ok[None], bias, NEG).reshape(NA_HEADS, tq, NA_UNION * GRID_W))
    tabs.append(jnp.full((NA_HEADS, tq, NA_UNION * GRID_W), NEG, F32))
    win = jnp.stack(tabs, axis=0)
    return jnp.concatenate([win * LOG2E, jnp.zeros(win.shape[:3] + (n_ctx,), F32)], axis=-1)


def _merge_kernel(xl_ref, xc_ref, y_ref, og_ref, on_ref, gt_ref, mod_ref, gw_ref, gb_ref, ws_ref, wg_ref, wn_ref,
                  wo_ref, o_ref, *, n_lat_tiles):
    d = xl_ref.shape[-1]
    y = jnp.concatenate([y_ref[q, 0].astype(F32) for q in range(SSM_QT)], axis=1)
    gy = _gelu_tanh(y)
    ys = gy * jax.nn.sigmoid(_dot(gy.astype(BF), gw_ref[...]) + gb_ref[...])
    gt = gt_ref[0]
    m = gt[:, :d].astype(F32) * _dot(ys.astype(BF), ws_ref[...])
    m += gt[:, d:2 * d].astype(F32) * _dot(og_ref[0], wg_ref[...])
    m += gt[:, 2 * d:].astype(F32) * _dot(on_ref[0], wn_ref[...])
    x = _stream_tile(xl_ref, xc_ref, n_lat_tiles)
    o_ref[0] = x + mod_ref[0][2:3] * _dot(m.astype(BF), wo_ref[...])


def merge(x_lat, x_ctx, y4, og, on, gt, mod, gw, gb, ws, wg, wn, wo, n_tiles, n_lat_tiles):
    b, _, d = x_lat.shape
    tm = TOKEN_TILE
    nb = mod.shape[0] - 1

    def row(bi, j):
        return (bi, j, 0)

    return pl.pallas_call(
        functools.partial(_merge_kernel, n_lat_tiles=n_lat_tiles), grid=(b, n_tiles),
        in_specs=_stream_specs(x_lat, x_ctx, n_lat_tiles) + [
                  pl.BlockSpec((SSM_QT, 1, tm, LANES), lambda bi, j: (0, bi, j, 0)),
                  pl.BlockSpec((1, tm, GQA_Q_W), row),
                  pl.BlockSpec((1, tm, NA_W), row),
                  pl.BlockSpec((1, tm, 3 * d), row),
                  pl.BlockSpec((1, 6, d), lambda bi, j: (jnp.where(j < n_lat_tiles, bi, nb), 0, 0)),
                  _const_spec(gw.shape), _const_spec(gb.shape), _const_spec(ws.shape),
                  _const_spec(wg.shape), _const_spec(wn.shape), _const_spec(wo.shape)],
        out_specs=pl.BlockSpec((1, tm, d), row),
        out_shape=jax.ShapeDtypeStruct((b, n_tiles * tm, d), F32),
        compiler_params=_cp(("arbitrary", "arbitrary")), name="merge",
    )(x_lat, x_ctx, y4, og, on, gt, mod, gw, gb, ws, wg, wn, wo)


def _ffn_kernel(x_ref, mod_ref, g_ref, wg_ref, wu_ref, wd_ref, o_ref):
    x = x_ref[0]
    mod = mod_ref[0]
    hb = (_rms(x, g_ref[...]) * (1.0 + mod[4:5]) + mod[3:4]).astype(BF)
    a = _silu(_dot(hb, wg_ref[...])) * _dot(hb, wu_ref[...])
    o_ref[0] = x + mod[5:6] * _dot(a.astype(BF), wd_ref[...])


def ffn(xall, mod, g, wg, wu, wd, n_lat_tiles):
    b, s, d = xall.shape
    tm = TOKEN_TILE
    nb = mod.shape[0] - 1

    def row(bi, j):
        return (bi, j, 0)

    return pl.pallas_call(
        _ffn_kernel, grid=(b, s // tm),
        in_specs=[pl.BlockSpec((1, tm, d), row),
                  pl.BlockSpec((1, 6, d), lambda bi, j: (jnp.where(j < n_lat_tiles, bi, nb), 0, 0)),
                  _const_spec(g.shape), _const_spec(wg.shape), _const_spec(wu.shape), _const_spec(wd.shape)],
        out_specs=pl.BlockSpec((1, tm, d), row),
        out_shape=jax.ShapeDtypeStruct((b, s, d), F32),
        compiler_params=_cp(("arbitrary", "arbitrary"), 56), name="ffn",
    )(xall, mod, g, wg, wu, wd)


ROW_TILES = D_MODEL // LANES


def _store_row_tiles(ref, lead, val):
    n = val.shape[0]
    for k in range(ROW_TILES):
        ref[lead + (pl.ds(k, n, stride=ROW_TILES), slice(None))] = val[:, k * LANES:(k + 1) * LANES]


def _load_row_tiles(ref):
    n = ref.shape[0] // ROW_TILES
    return jnp.concatenate([ref[pl.ds(k, n, stride=ROW_TILES), :] for k in range(ROW_TILES)], axis=1)


def _route_kernel(x_ref, mod_ref, g_ref, rw_ref, h_ref, info_ref):
    x = x_ref[0]
    mod = mod_ref[0]
    h = _rms(x, g_ref[...]) * (1.0 + mod[4:5]) + mod[3:4]
    _store_row_tiles(h_ref, (0,), h)
    logits = jnp.dot(h, rw_ref[...], preferred_element_type=F32, precision=lax.Precision.HIGHEST)
    lane = lax.broadcasted_iota(jnp.int32, logits.shape, 1)
    lanef = lane.astype(F32)
    logits = jnp.where(lane < N_EXPERTS, logits, -jnp.inf)
    m1 = jnp.max(logits, axis=-1, keepdims=True)
    i1 = jnp.min(jnp.where(logits == m1, lanef, float(LANES)), axis=-1, keepdims=True)
    rest = jnp.where(lanef == i1, -jnp.inf, logits)
    m2 = jnp.max(rest, axis=-1, keepdims=True)
    i2 = jnp.min(jnp.where(rest == m2, lanef, float(LANES)), axis=-1, keepdims=True)
    e2 = jnp.exp(m2 - m1)
    w1 = 1.0 / (1.0 + e2)
    w2 = e2 / (1.0 + e2)
    info_ref[0] = jnp.where(lane == 0, i1, jnp.where(lane == 1, i2, jnp.where(lane == 2, w1,
                            jnp.where(lane == 3, w2, 0.0))))


def moe_route(xall, mod, g, rw, n_tiles):
    b, s, d = xall.shape
    tm = TOKEN_TILE

    def row(bi, j):
        return (bi, j, 0)

    return pl.pallas_call(
        _route_kernel, grid=(b, n_tiles),
        in_specs=[pl.BlockSpec((1, tm, d), row), pl.BlockSpec((1, 6, d), lambda bi, j: (bi, 0, 0)),
                  _const_spec(g.shape), _const_spec(rw.shape)],
        out_specs=[pl.BlockSpec((1, tm * ROW_TILES, LANES), row),
                   pl.BlockSpec((1, tm, LANES), row)],
        out_shape=[jax.ShapeDtypeStruct((b, n_tiles * tm * ROW_TILES, LANES), F32),
                   jax.ShapeDtypeStruct((b, n_tiles * tm, LANES), F32)],
        compiler_params=_cp(("arbitrary", "arbitrary")), name="moe_route",
    )(xall, mod, g, rw)


def _dispatch_kernel(pos_ref, h_ref, xs_in_ref, xs_ref, sem):
    del xs_in_ref
    tm = h_ref.shape[0] // ROW_TILES

    def row_copy(r, k):
        dst = pl.multiple_of(pos_ref[0, 0, 2 * r + k] * ROW_TILES, ROW_TILES)
        src = pl.multiple_of(r * ROW_TILES, ROW_TILES)
        return pltpu.make_async_copy(h_ref.at[pl.ds(src, ROW_TILES)], xs_ref.at[pl.ds(dst, ROW_TILES)], sem)

    def issue(r, c):
        row_copy(r, 0).start(priority=0)
        row_copy(r, 1).start(priority=1)
        return c

    def drain(r, c):
        row_copy(r, 0).wait()
        row_copy(r, 1).wait()
        return c

    lax.fori_loop(0, tm, issue, 0, unroll=4)
    lax.fori_loop(0, tm, drain, 0)


def moe_dispatch(h2, pos, n_slots):
    t = h2.shape[0] // ROW_TILES
    tm = MOE_ROW_TILE
    zeros = jnp.zeros((n_slots * ROW_TILES, LANES), F32)
    return pl.pallas_call(
        _dispatch_kernel, grid=(t // tm,),
        in_specs=[pl.BlockSpec((1, 1, 2 * tm), lambda i: (i, 0, 0), memory_space=pltpu.SMEM),
                  pl.BlockSpec((tm * ROW_TILES, LANES), lambda i: (i, 0)),
                  pl.BlockSpec(memory_space=pl.ANY)],
        out_specs=pl.BlockSpec(memory_space=pl.ANY),
        out_shape=jax.ShapeDtypeStruct((n_slots * ROW_TILES, LANES), F32),
        scratch_shapes=[pltpu.SemaphoreType.DMA(())],
        input_output_aliases={2: 0},
        compiler_params=_cp(("arbitrary",)), name="moe_dispatch",
    )(pos, h2, zeros)


def _experts_kernel(te_ref, nv_ref, x_ref, wg_ref, wu_ref, wd_ref, o_ref, acc_ref):
    i = pl.program_id(0)
    j = pl.program_id(1)

    @pl.when(i < nv_ref[0])
    def _():
        xb = _load_row_tiles(x_ref).astype(BF)
        a = _silu(_dot(xb, wg_ref[0])) * _dot(xb, wu_ref[0])
        part = _dot(a.astype(BF), wd_ref[0])

        @pl.when(j == 0)
        def _():
            acc_ref[...] = part

        @pl.when(j > 0)
        def _():
            acc_ref[...] += part

        @pl.when(j == pl.num_programs(1) - 1)
        def _():
            _store_row_tiles(o_ref, (), acc_ref[...])

    @pl.when((i >= nv_ref[0]) & (j == pl.num_programs(1) - 1))
    def _():
        o_ref[...] = jnp.zeros(o_ref.shape, o_ref.dtype)


def moe_experts(xs, tile_expert, n_valid, wg, wu, wd):
    n_slots = xs.shape[0] // ROW_TILES
    d = ROW_TILES * LANES
    tm = MOE_TILE
    tf = MOE_FCHUNK
    f = wg.shape[-1]
    nf = f // tf

    def xrow(i, j, te, nv):
        return (jnp.minimum(i, nv[0] - 1), 0)

    def fcol(i, j, nv):
        return jnp.where(i < nv[0], j, nf - 1)

    grid_spec = pltpu.PrefetchScalarGridSpec(
        num_scalar_prefetch=2, grid=(n_slots // tm, nf),
        in_specs=[pl.BlockSpec((tm * ROW_TILES, LANES), xrow),
                  pl.BlockSpec((1, d, tf), lambda i, j, te, nv: (te[i], 0, fcol(i, j, nv))),
                  pl.BlockSpec((1, d, tf), lambda i, j, te, nv: (te[i], 0, fcol(i, j, nv))),
                  pl.BlockSpec((1, tf, d), lambda i, j, te, nv: (te[i], fcol(i, j, nv), 0))],
        out_specs=pl.BlockSpec((tm * ROW_TILES, LANES), lambda i, j, te, nv: (i, 0)),
        scratch_shapes=[pltpu.VMEM((tm, d), F32)])
    return pl.pallas_call(
        _experts_kernel, grid_spec=grid_spec,
        out_shape=jax.ShapeDtypeStruct((n_slots * ROW_TILES, LANES), F32),
        compiler_params=_cp(("arbitrary", "arbitrary"), 56), name="moe_experts",
    )(tile_expert, n_valid, xs, wg, wu, wd)


def _combine_kernel(pos_ref, x_ref, info_ref, mod_ref, fg_ref, ys_ref, o_ref, y1_ref, y2_ref, sem):
    tm = x_ref.shape[1]

    def row_copy(r, k, dst):
        src = pl.multiple_of(pos_ref[0, 0, 2 * r + k] * ROW_TILES, ROW_TILES)
        row = pl.multiple_of(r * ROW_TILES, ROW_TILES)
        return pltpu.make_async_copy(ys_ref.at[pl.ds(src, ROW_TILES)], dst.at[pl.ds(row, ROW_TILES)], sem)

    def issue(r, c):
        row_copy(r, 0, y1_ref).start(priority=0)
        row_copy(r, 1, y2_ref).start(priority=1)
        return c

    def drain(r, c):
        row_copy(r, 0, y1_ref).wait()
        row_copy(r, 1, y2_ref).wait()
        return c

    lax.fori_loop(0, tm, issue, 0, unroll=4)
    lax.fori_loop(0, tm, drain, 0)
    info = info_ref[0]
    y = info[:, 2:3] * _load_row_tiles(y1_ref) + info[:, 3:4] * _load_row_tiles(y2_ref)
    xn = x_ref[0] + mod_ref[0][5:6] * y
    o_ref[0] = _rms(xn, fg_ref[...])


def moe_combine(x, info, mod, fg, ys, pos, n_tiles):
    b, s, d = x.shape
    tm = MOE_ROW_TILE

    def row(bi, j):
        return (bi, j, 0)

    return pl.pallas_call(
        _combine_kernel, grid=(b, n_tiles),
        in_specs=[pl.BlockSpec((1, 1, 2 * tm), lambda bi, j: (bi * n_tiles + j, 0, 0), memory_space=pltpu.SMEM),
                  pl.BlockSpec((1, tm, d), row), pl.BlockSpec((1, tm, LANES), row),
                  pl.BlockSpec((1, 6, d), lambda bi, j: (bi, 0, 0)), _const_spec(fg.shape),
                  pl.BlockSpec(memory_space=pl.ANY)],
        out_specs=pl.BlockSpec((1, tm, d), row),
        out_shape=jax.ShapeDtypeStruct((b, n_tiles * tm, d), F32),
        scratch_shapes=[pltpu.VMEM((tm * ROW_TILES, LANES), F32), pltpu.VMEM((tm * ROW_TILES, LANES), F32),
                        pltpu.SemaphoreType.DMA(())],
        compiler_params=_cp(("arbitrary", "arbitrary")), name="moe_combine",
    )(pos, x, info, mod, fg, ys)


def moe_layer(x, mod, g2, rw, wg, wu, wd, fg, n_lat):
    b, s, d = x.shape
    n_tiles = n_lat // TOKEN_TILE
    t = b * n_lat
    rw_pad = jnp.zeros((d, LANES), F32).at[:, :N_EXPERTS].set(rw.astype(F32))
    h2, info = moe_route(x, mod, g2, rw_pad, n_tiles)
    info2 = info.reshape(t, LANES)

    e_pair = info2[:, :2].astype(jnp.int32).reshape(2 * t)
    onehot = (e_pair[:, None] == jnp.arange(N_EXPERTS)[None, :]).astype(jnp.int32)
    csum = jnp.cumsum(onehot, axis=0)
    rank = jnp.sum((csum - onehot) * onehot, axis=1)
    counts = csum[-1]
    tiles_e = (counts + MOE_TILE - 1) // MOE_TILE
    tile_end = jnp.cumsum(tiles_e)
    slot_off = (tile_end - tiles_e) * MOE_TILE
    pos = (slot_off[e_pair] + rank).astype(jnp.int32).reshape(t // MOE_ROW_TILE, 1, 2 * MOE_ROW_TILE)
    n_tiles_max = (2 * t) // MOE_TILE + N_EXPERTS
    n_valid = tile_end[-1:].astype(jnp.int32)
    tile_ids = jnp.minimum(jnp.arange(n_tiles_max), n_valid[0] - 1)
    tile_expert = jnp.sum((tile_ids[:, None] >= tile_end[None, :]).astype(jnp.int32), axis=1).astype(jnp.int32)

    xs = moe_dispatch(h2.reshape(t * ROW_TILES, LANES), pos, n_tiles_max * MOE_TILE)
    ys = moe_experts(xs, tile_expert, n_valid, wg, wu, wd)
    return moe_combine(x, info, mod, fg, ys, pos, n_lat // MOE_ROW_TILE)


def _rope_tables(n_lat, n_ctx):
    t = jnp.arange(n_lat)
    pos = jnp.stack([t // GRID_W, t % GRID_W], axis=-1).astype(F32)
    half = HEAD_DIM // 2
    inv = 1.0 / (ROPE_BASE ** (jnp.arange(0, half, 2, dtype=F32) / half))
    ang = pos[:, :, None] * inv
    ang = jnp.concatenate([ang, ang], axis=-1).reshape(n_lat, HEAD_DIM)
    cos = jnp.concatenate([jnp.cos(ang), jnp.ones((n_ctx, HEAD_DIM), F32)], axis=0)
    sin = jnp.concatenate([jnp.sin(ang), jnp.zeros((n_ctx, HEAD_DIM), F32)], axis=0)
    return jnp.tile(cos, (1, 2)), jnp.tile(sin, (1, 2))


_GQA_HEAD_ORDER = tuple(g * GQA_GROUP + t for t in range(GQA_GROUP) for g in range(GQA_KV_HEADS))


def _gqa_cols(w):
    return jnp.concatenate([w[:, h * HEAD_DIM:(h + 1) * HEAD_DIM] for h in _GQA_HEAD_ORDER], axis=1)


def _gqa_rows(w):
    return jnp.concatenate([w[h * HEAD_DIM:(h + 1) * HEAD_DIM] for h in _GQA_HEAD_ORDER], axis=0)


def kernel(x, c, ctx, c_ctx, w_mod, b_mod, norm1_g, w_in, ssm_a_re, ssm_a_im, ssm_b_re, ssm_b_im, ssm_c_re,
           ssm_c_im, ssm_log_dt, ssm_d, glu_w, glu_b, q_norm_g, k_norm_g, na_rpb, w_branch_ssm, w_branch_gqa,
           w_branch_na, w_out, norm2_g, ffn_w_gate, ffn_w_up, ffn_w_down, router_w, moe_w_gate, moe_w_up,
           moe_w_down, final_norm_g):
    b, n_lat, d = x.shape
    n_ctx = ctx.shape[1]
    s = n_lat + n_ctx
    depth = w_mod.shape[0]
    assert d == D_MODEL and n_lat % (NA_QROWS * GRID_W) == 0 and n_ctx == TOKEN_TILE and n_lat % MOE_ROW_TILE == 0
    assert depth == 2 and n_lat // GRID_W >= NA_UNION
    n_lat_tiles = n_lat // TOKEN_TILE
    n_all_tiles = s // TOKEN_TILE

    n_rows = -(-(b + 1) // 8) * 8
    cv = jnp.zeros((n_rows, d), F32).at[:b].set(c).at[b].set(c_ctx)
    mods = adaln(cv, w_mod, b_mod)[:, :b + 1].reshape(depth, b + 1, 6, d)

    cos, sin = _rope_tables(n_lat, n_ctx)
    table_shape_rows = n_lat // GRID_W
    x_lat, x_ctx = x, ctx

    mats = jax.vmap(s5_matrices)(ssm_a_re, ssm_a_im, ssm_b_re, ssm_b_im, ssm_c_re, ssm_c_im, ssm_log_dt, ssm_d)
    tables = jax.vmap(lambda r: na_table(r, table_shape_rows, n_ctx))(na_rpb)

    out = None
    for i in range(depth):
        last = i == depth - 1
        wi = w_in[i]
        c0 = SSM_WIDTH
        wu = wi[:, :c0].astype(BF)
        wkv = wi[:, c0:KV_COLS].astype(BF)
        wqg = _gqa_cols(wi[:, KV_COLS:KV_COLS + GQA_Q_W]).astype(BF)
        wqn = wi[:, KV_COLS + GQA_Q_W:KV_COLS + GQA_Q_W + NA_W].astype(BF)
        wgt = wi[:, KV_COLS + GQA_Q_W + NA_W:].astype(BF)
        qgain = jnp.tile(q_norm_g[i].astype(F32), 2)[None, :]
        kgain = jnp.tile(k_norm_g[i].astype(F32), GQA_KV_HEADS)[None, :]

        uq, kg, vg, kn, vn, qg, qn, gt = inproj(x_lat, x_ctx, mods[i], norm1_g[i][None, :], cos, sin, wu, wkv,
                                                wqg, wqn, wgt, qgain, kgain, n_lat_tiles)

        y4 = s5_branch(uq, mats, i, n_lat // SSM_CHUNK)

        n_q_tiles = n_lat_tiles if last else n_all_tiles
        og = gqa_attention(qg, kg, vg, n_lat, n_q_tiles)
        on = na_attention(qn, kn, vn, tables, i, n_lat, n_q_tiles)

        xall = merge(x_lat, x_ctx, y4, og, on, gt, mods[i], glu_w[i].astype(BF), glu_b[i][None, :].astype(F32),
                     w_branch_ssm[i].astype(BF), _gqa_rows(w_branch_gqa[i]).astype(BF),
                     w_branch_na[i].astype(BF), w_out[i].astype(BF), n_q_tiles, n_lat_tiles)

        j = i // 2
        if i % 2 == 0:
            xall = ffn(xall, mods[i], norm2_g[i][None, :], ffn_w_gate[j].astype(BF), ffn_w_up[j].astype(BF),
                       ffn_w_down[j].astype(BF), n_lat_tiles)
            x_lat = x_ctx = xall
        else:
            out = moe_layer(xall, mods[i], norm2_g[i][None, :], router_w[j], moe_w_gate[j].astype(BF),
                            moe_w_up[j].astype(BF), moe_w_down[j].astype(BF), final_norm_g[None, :], n_lat)
    return out
```

```python
import functools
import math

import jax
import jax.numpy as jnp
from jax import lax
from jax.experimental import pallas as pl
from jax.experimental.pallas import tpu as pltpu

D_MODEL = 1024
GRID_W = 64
SSM_WIDTH = 512
SSM_GROUP = 16
SSM_GROUPS = SSM_WIDTH // SSM_GROUP
SSM_STATE = 64
HEAD_DIM = 64
GQA_HEADS = 8
GQA_KV_HEADS = 2
GQA_GROUP = GQA_HEADS // GQA_KV_HEADS
ROPE_BASE = 10000.0
NA_HEADS = 8
NA_WIN_R = 8
NA_WIN_C = 16
ATTN_SCALE = HEAD_DIM ** -0.5
LOG2E = math.log2(math.e)
GQA_Q_W = GQA_HEADS * HEAD_DIM
GQA_KV_W = GQA_KV_HEADS * HEAD_DIM
NA_W = NA_HEADS * HEAD_DIM
KV_COLS = SSM_WIDTH + 2 * GQA_KV_W + 2 * NA_W
N_EXPERTS = 8
NORM_EPS = 1e-6

LANES = 128
TOKEN_TILE = 256
SSM_CHUNK = 16
SSM_QT = SSM_WIDTH // LANES
SSM_GPT = LANES // SSM_GROUP
SSM_ST = SSM_GPT * SSM_STATE
NA_QROWS = 4
NA_UNION = 12
MOE_TILE = 512
MOE_ROW_TILE = 1024
MOE_FCHUNK = 1792
NEG = -1e30

BF = jnp.bfloat16
F32 = jnp.float32


def _cp(sem, vmem_mb=48):
    return pltpu.CompilerParams(dimension_semantics=sem, vmem_limit_bytes=vmem_mb * 1024 * 1024)


def _const_spec(shape):
    nd = len(shape)
    return pl.BlockSpec(shape, lambda *_: (0,) * nd, pipeline_mode=pl.Buffered(1))


def _dot(a, b):
    return jnp.dot(a, b, preferred_element_type=F32)


def _dot_nt(a, b):
    return lax.dot_general(a, b, (((1,), (1,)), ((), ())), preferred_element_type=F32)


def _rms(x, g):
    return x * lax.rsqrt(jnp.mean(x * x, axis=-1, keepdims=True) + NORM_EPS) * g


def _silu(x):
    return x * jax.nn.sigmoid(x)


def _gelu_tanh(x):
    return 0.5 * x * (1.0 + jnp.tanh(math.sqrt(2.0 / math.pi) * (x + 0.044715 * x * x * x)))


def _adaln_kernel(cv_ref, w_ref, b_ref, o_ref):
    a = _silu(cv_ref[...])
    w = w_ref[0]
    a_hi = a.astype(BF)
    a_lo = (a - a_hi.astype(F32)).astype(BF)
    w_hi = w.astype(BF)
    w_lo = (w - w_hi.astype(F32)).astype(BF)
    o_ref[0] = _dot(a_hi, w_hi) + _dot(a_hi, w_lo) + _dot(a_lo, w_hi) + b_ref[0]


def adaln(cv, w_mod, b_mod):
    depth, d, n = w_mod.shape
    r = cv.shape[0]
    tn = 512
    return pl.pallas_call(
        _adaln_kernel,
        grid=(depth, n // tn),
        in_specs=[pl.BlockSpec((r, d), lambda i, j: (0, 0)),
                  pl.BlockSpec((1, d, tn), lambda i, j: (i, 0, j)),
                  pl.BlockSpec((1, 1, tn), lambda i, j: (i, 0, j))],
        out_specs=pl.BlockSpec((1, r, tn), lambda i, j: (i, 0, j)),
        out_shape=jax.ShapeDtypeStruct((depth, r, n), F32),
        compiler_params=_cp(("arbitrary", "arbitrary")),
        name="adaln",
    )(cv, w_mod, b_mod.reshape(depth, 1, n))


def _rope(xn, cos, sin):
    lane = lax.broadcasted_iota(jnp.int32, xn.shape, 1)
    first = (lane % 32) < 16
    rot = jnp.where(first, -pltpu.roll(xn, LANES - 16, 1), pltpu.roll(xn, 16, 1))
    return xn * cos + rot * sin


def _stream_specs(x_lat, x_ctx, n_lat_tiles):
    tm = TOKEN_TILE
    d = x_lat.shape[-1]
    ctx_tile0 = x_ctx.shape[1] // tm - 1
    return [pl.BlockSpec((1, tm, d), lambda bi, j: (bi, jnp.minimum(j, n_lat_tiles - 1), 0)),
            pl.BlockSpec((1, tm, d), lambda bi, j: (bi, ctx_tile0, 0))]


def _stream_tile(xl_ref, xc_ref, n_lat_tiles):
    return jnp.where(pl.program_id(1) < n_lat_tiles, xl_ref[0], xc_ref[0])


def _inproj_kernel(xl_ref, xc_ref, mod_ref, g_ref, cos_ref, sin_ref, wu_ref, wkv_ref, wqg_ref, wqn_ref, wgt_ref,
                   qgain_ref, kgain_ref,
                   u_ref, kg_ref, vg_ref, kn_ref, vn_ref, qg_ref, qn_ref, gt_ref, *, n_lat_tiles):
    x = _stream_tile(xl_ref, xc_ref, n_lat_tiles)
    mod = mod_ref[0]
    h = _rms(x, g_ref[...]) * (1.0 + mod[1:2]) + mod[0:1]
    hb = h.astype(BF)
    cos = cos_ref[...]
    sin = sin_ref[...]

    u = _dot(hb, wu_ref[...])
    for q in range(SSM_QT):
        u_ref[q, 0] = u[:, q * LANES:(q + 1) * LANES]

    def head_pair_norm_rope(t, gain):
        lane = lax.broadcasted_iota(jnp.int32, t.shape, 1)
        lo = lane < HEAD_DIM
        sq = t * t
        ms_lo = jnp.sum(jnp.where(lo, sq, 0.0), axis=-1, keepdims=True)
        ms_hi = jnp.sum(jnp.where(lo, 0.0, sq), axis=-1, keepdims=True)
        ms = jnp.where(lo, ms_lo, ms_hi) * (1.0 / HEAD_DIM)
        return _rope(t * lax.rsqrt(ms + NORM_EPS) * gain, cos, sin)

    kv = _dot(hb, wkv_ref[...])
    kg_ref[0] = head_pair_norm_rope(kv[:, :LANES], kgain_ref[...]).astype(BF)
    vg_ref[0] = kv[:, LANES:2 * LANES].astype(BF)
    kn_ref[0] = kv[:, 2 * LANES:2 * LANES + NA_W].astype(BF)
    vn_ref[0] = kv[:, 2 * LANES + NA_W:].astype(BF)

    qg = _dot(hb, wqg_ref[...])
    for t in range(GQA_Q_W // LANES):
        ts = slice(t * LANES, (t + 1) * LANES)
        qg_ref[0, :, ts] = (head_pair_norm_rope(qg[:, ts], qgain_ref[...]) * (ATTN_SCALE * LOG2E)).astype(BF)

    qn_ref[0] = (_dot(hb, wqn_ref[...]) * (ATTN_SCALE * LOG2E)).astype(BF)
    gt_ref[0] = jax.nn.sigmoid(_dot(hb, wgt_ref[...])).astype(BF)


def inproj(x_lat, x_ctx, mod, g, cos, sin, wu, wkv, wqg, wqn, wgt, qgain, kgain, n_lat_tiles):
    b, _, d = x_lat.shape
    tm = TOKEN_TILE
    nt = n_lat_tiles + 1
    s = nt * tm
    nb = mod.shape[0] - 1

    def row(bi, j):
        return (bi, j, 0)

    def modrow(bi, j):
        return (jnp.where(j < n_lat_tiles, bi, nb), 0, 0)

    outs = [
        jax.ShapeDtypeStruct((SSM_QT, b, s, LANES), F32),
        jax.ShapeDtypeStruct((b, s, LANES), BF),
        jax.ShapeDtypeStruct((b, s, LANES), BF),
        jax.ShapeDtypeStruct((b, s, NA_W), BF),
        jax.ShapeDtypeStruct((b, s, NA_W), BF),
        jax.ShapeDtypeStruct((b, s, GQA_Q_W), BF),
        jax.ShapeDtypeStruct((b, s, NA_W), BF),
        jax.ShapeDtypeStruct((b, s, 3 * d), BF),
    ]
    out_specs = [
        pl.BlockSpec((SSM_QT, 1, tm, LANES), lambda bi, j: (0, bi, j, 0)),
        pl.BlockSpec((1, tm, LANES), row),
        pl.BlockSpec((1, tm, LANES), row),
        pl.BlockSpec((1, tm, NA_W), row),
        pl.BlockSpec((1, tm, NA_W), row),
        pl.BlockSpec((1, tm, GQA_Q_W), row),
        pl.BlockSpec((1, tm, NA_W), row),
        pl.BlockSpec((1, tm, 3 * d), row),
    ]
    in_specs = _stream_specs(x_lat, x_ctx, n_lat_tiles) + [
        pl.BlockSpec((1, 6, d), modrow),
        _const_spec(g.shape),
        pl.BlockSpec((tm, LANES), lambda bi, j: (j, 0)),
        pl.BlockSpec((tm, LANES), lambda bi, j: (j, 0)),
        _const_spec(wu.shape), _const_spec(wkv.shape), _const_spec(wqg.shape),
        _const_spec(wqn.shape), _const_spec(wgt.shape), _const_spec(qgain.shape), _const_spec(kgain.shape),
    ]
    return pl.pallas_call(
        functools.partial(_inproj_kernel, n_lat_tiles=n_lat_tiles), grid=(b, nt), in_specs=in_specs,
        out_specs=out_specs, out_shape=outs,
        compiler_params=_cp(("arbitrary", "arbitrary"), 56), name="inproj",
    )(x_lat, x_ctx, mod, g, cos, sin, wu, wkv, wqg, wqn, wgt, qgain, kgain)


def _chunk_rows(u_ref):
    nk = u_ref.shape[2] // SSM_CHUNK
    cols = [u_ref[0, 0, pl.ds(j, nk, stride=SSM_CHUNK), :] for j in range(SSM_CHUNK)]
    return jnp.concatenate(cols, axis=1).astype(BF)


def _ssm_in_kernel(u_ref, m_ref, zf_ref, zr_ref):
    z = _dot(_chunk_rows(u_ref), m_ref[0, 0])
    half = z.shape[1] // 2
    zf_ref[0] = z[:, :half]
    zr_ref[0] = z[:, half:]


def ssm_in(uq, m1, layer):
    qt, b, s, _ = uq.shape
    nk = s // SSM_CHUNK
    k = SSM_CHUNK * LANES
    w = 2 * SSM_ST
    zspec = pl.BlockSpec((1, nk, w), lambda q, bi: (bi, 0, q))
    return pl.pallas_call(
        _ssm_in_kernel, grid=(qt, b),
        in_specs=[pl.BlockSpec((1, 1, s, LANES), lambda q, bi: (q, bi, 0, 0)),
                  pl.BlockSpec((1, 1, k, 2 * w), lambda q, bi: (layer, q, 0, 0))],
        out_specs=[zspec, zspec],
        out_shape=[jax.ShapeDtypeStruct((b, nk, qt * w), F32)] * 2,
        compiler_params=_cp(("arbitrary", "arbitrary")), name="ssm_in",
    )(uq, m1)


def _ssm_scan_kernel(zf_ref, zr_ref, af_ref, ar_ref, sf_ref, sr_ref, *, n_lat_chunks):
    nk = zf_ref.shape[1]
    st = SSM_ST
    af_re, af_im = af_ref[0, :, :st], af_ref[0, :, st:]
    ar_re, ar_im = ar_ref[0, :, :st], ar_ref[0, :, st:]

    def step(i, carry):
        fre, fim, rre, rim = carry
        kf = lax.rem(i + n_lat_chunks, nk)
        kr = nk - 1 - i
        sf_ref[0, pl.ds(kf, 1), :] = jnp.concatenate([fre, fim], axis=1)
        sr_ref[0, pl.ds(kr, 1), :] = jnp.concatenate([rre, rim], axis=1)
        zf = zf_ref[0, pl.ds(kf, 1), :]
        zr = zr_ref[0, pl.ds(kr, 1), :]
        nfre = af_re * fre - af_im * fim + zf[:, :st]
        nfim = af_re * fim + af_im * fre + zf[:, st:]
        nrre = ar_re * rre - ar_im * rim + zr[:, :st]
        nrim = ar_re * rim + ar_im * rre + zr[:, st:]
        return nfre, nfim, nrre, nrim

    z0 = jnp.zeros((1, st), F32)
    lax.fori_loop(0, nk, step, (z0, z0, z0, z0), unroll=4 if nk % 4 == 0 else 1)


def ssm_scan(zf, zr, af, ar, layer, n_lat_chunks):
    b, nk, n = zf.shape
    w = 2 * SSM_ST
    spec = pl.BlockSpec((1, nk, w), lambda bi, q: (bi, 0, q))
    aspec = pl.BlockSpec((1, 1, w), lambda bi, q: (layer, 0, q))
    return pl.pallas_call(
        functools.partial(_ssm_scan_kernel, n_lat_chunks=n_lat_chunks),
        grid=(b, n // w), in_specs=[spec, spec, aspec, aspec], out_specs=[spec, spec],
        out_shape=[jax.ShapeDtypeStruct((b, nk, n), F32)] * 2,
        compiler_params=_cp(("arbitrary", "arbitrary")), name="ssm_scan",
    )(zf, zr, af, ar)


def _ssm_out_kernel(u_ref, sf_ref, sr_ref, mu_ref, mf_ref, mr_ref, y_ref):
    nk = sf_ref.shape[1]
    y = _dot(_chunk_rows(u_ref), mu_ref[0, 0])
    y += _dot(sf_ref[0].astype(BF), mf_ref[0, 0])
    y += _dot(sr_ref[0].astype(BF), mr_ref[0, 0])
    for j in range(SSM_CHUNK):
        y_ref[0, 0, pl.ds(j, nk, stride=SSM_CHUNK), :] = y[:, j * LANES:(j + 1) * LANES]


def ssm_out(uq, sf, sr, m3u, m3f, m3r, layer):
    qt, b, s, _ = uq.shape
    nk = s // SSM_CHUNK
    k = SSM_CHUNK * LANES
    w = 2 * SSM_ST
    one = pl.Buffered(1)
    uspec = pl.BlockSpec((1, 1, s, LANES), lambda q, bi: (q, bi, 0, 0))
    sspec = pl.BlockSpec((1, nk, w), lambda q, bi: (bi, 0, q))
    return pl.pallas_call(
        _ssm_out_kernel, grid=(qt, b),
        in_specs=[uspec, sspec, sspec,
                  pl.BlockSpec((1, 1, k, k), lambda q, bi: (layer, q, 0, 0), pipeline_mode=one),
                  pl.BlockSpec((1, 1, w, k), lambda q, bi: (layer, q, 0, 0), pipeline_mode=one),
                  pl.BlockSpec((1, 1, w, k), lambda q, bi: (layer, q, 0, 0), pipeline_mode=one)],
        out_specs=uspec,
        out_shape=jax.ShapeDtypeStruct((qt, b, s, LANES), F32),
        compiler_params=_cp(("arbitrary", "arbitrary")), name="ssm_out",
    )(uq, sf, sr, m3u, m3f, m3r)


def s5_matrices(a_re, a_im, b_re, b_im, c_re, c_im, log_dt, d_skip):
    tc, g, p, hh = SSM_CHUNK, SSM_GROUPS, SSM_STATE, SSM_GROUP
    lam = lax.complex(a_re.astype(F32), a_im.astype(F32))
    dt = jnp.exp(log_dt.astype(F32))[..., None]
    ldt = lam * dt
    a_bar = jnp.exp(ldt)
    b_bar = ((a_bar - 1.0) / lam)[..., None] * lax.complex(b_re.astype(F32), b_im.astype(F32))
    cm = lax.complex(c_re.astype(F32), c_im.astype(F32))
    steps = jnp.arange(tc + 1, dtype=F32)[:, None, None, None]
    pw = jnp.exp(ldt[None] * steps)
    pw_desc = jnp.exp(ldt[None] * (tc - steps))
    hi = lax.Precision.HIGHEST
    gpt = SSM_GPT

    def widen(compact, col_expand, row_div, col_div):
        rows, cols = compact.shape[1], col_expand.shape[1]
        wide = jnp.einsum('qrc,cn->qrn', compact.astype(BF), col_expand, preferred_element_type=F32)
        rg = (jnp.arange(rows) // row_div) % gpt
        cg = (jnp.arange(cols) // col_div) % gpt
        return jnp.where((rg[:, None] == cg[None, :])[None], wide, 0.0).astype(BF)

    x_tok = jnp.einsum('ab,m,cd->acbmd', jnp.eye(tc, dtype=F32), jnp.ones((gpt,), F32),
                       jnp.eye(hh, dtype=F32)).reshape(tc * hh, tc * LANES).astype(BF)
    x_st = jnp.einsum('ab,m,cd->acbmd', jnp.eye(4, dtype=F32), jnp.ones((gpt,), F32),
                      jnp.eye(p, dtype=F32)).reshape(4 * p, 4 * SSM_ST).astype(BF)

    kk = jnp.real(jnp.einsum('dghp,ldgp,dgpk->dlghk', cm, pw[:tc], b_bar))
    jj = jnp.arange(tc)[None, :, None]
    ii = jnp.arange(tc)[None, None, :]
    lag = jnp.arange(tc)[:, None, None]
    shift_f = (ii - jj == lag).astype(F32)
    shift_r = (jj - ii == lag).astype(F32)
    dd = (jnp.eye(tc, dtype=F32)[:, :, None, None, None] * jnp.eye(hh, dtype=F32)[None, None, None]
          * d_skip.astype(F32).reshape(g, hh)[None, None, :, :, None])
    t5 = (jnp.einsum('lji,lghk->jighk', shift_f, kk[0], precision=hi)
          + jnp.einsum('lji,lghk->jighk', shift_r, kk[1], precision=hi) + dd)
    tcomp = t5.reshape(tc, tc, SSM_QT, gpt, hh, hh).transpose(2, 0, 3, 5, 1, 4)
    m3u = widen(tcomp.reshape(SSM_QT, tc * LANES, tc * hh), x_tok, hh, hh)

    inf = pw_desc[1:, 0][..., None] * b_bar[0][None]
    inr = pw[:tc, 1][..., None] * b_bar[1][None]
    m1 = jnp.stack([jnp.real(inf), jnp.imag(inf), jnp.real(inr), jnp.imag(inr)], axis=0)
    m1 = m1.reshape(2, 2, tc, SSM_QT, gpt, p, hh).transpose(3, 2, 4, 6, 0, 1, 5)
    m1 = widen(m1.reshape(SSM_QT, tc * LANES, 4 * p), x_st, hh, p)

    def out_mat(d, powers):
        ca = cm[d][None] * powers[:, :, None, :]
        mm = jnp.stack([jnp.real(ca), -jnp.imag(ca)], axis=0)
        mm = mm.reshape(2, tc, SSM_QT, gpt, hh, p).transpose(2, 0, 3, 5, 1, 4)
        return widen(mm.reshape(SSM_QT, 2 * SSM_ST, tc * hh), x_tok, p, hh)

    m3f = out_mat(0, pw[1:, 0])
    m3r = out_mat(1, pw_desc[:tc, 1])

    def decay(d):
        a16 = pw[tc, d].reshape(SSM_QT, SSM_ST)
        return jnp.concatenate([jnp.real(a16), jnp.imag(a16)], axis=1).reshape(1, SSM_QT * 2 * SSM_ST)

    return m1.astype(BF), m3u.astype(BF), m3f.astype(BF), m3r.astype(BF), decay(0), decay(1)


def s5_branch(uq4, mats, layer, n_lat_chunks):
    m1, m3u, m3f, m3r, a16f, a16r = mats
    zf, zr = ssm_in(uq4, m1, layer)
    sf, sr = ssm_scan(zf, zr, a16f, a16r, layer, n_lat_chunks)
    return ssm_out(uq4, sf, sr, m3u, m3f, m3r, layer)


def _gqa_kernel(q_ref, k_ref, vt_ref, vtc_ref, o_ref, qs_ref, m_ref, l_ref, acc_ref, sa_ref, sb_ref, sctx_ref, *,
                n_lat, n_ctx, tk, n_lat_tiles):
    tq = q_ref.shape[1]
    qi = pl.program_id(1)
    n_slots = GQA_HEADS
    lane_half = lax.broadcasted_iota(jnp.int32, (tq, LANES), 1) // HEAD_DIM
    for t in range(GQA_Q_W // LANES):
        qt = q_ref[0, :, t * LANES:(t + 1) * LANES]
        for g in range(GQA_KV_HEADS):
            qs_ref[2 * t + g] = jnp.where(lane_half == g, qt, jnp.zeros((), qt.dtype))
    m_ref[...] = jnp.full(m_ref.shape, NEG, F32)
    l_ref[...] = jnp.zeros(l_ref.shape, F32)
    acc_ref[...] = jnp.zeros(acc_ref.shape, F32)

    n_chunks = n_lat // tk

    def scores(kc, s):
        return _dot_nt(kc, qs_ref[s])

    def lat_keys(c):
        return k_ref[0, pl.ds(pl.multiple_of(c * tk, tk), tk), :]

    def update(st, s, vtc):
        cs = slice(s * tq, (s + 1) * tq)
        m_old = m_ref[:, cs]
        m_new = jnp.maximum(m_old, jnp.max(st, axis=0, keepdims=True))
        alpha = jnp.exp2(m_old - m_new)
        p = jnp.exp2(st - m_new)
        l_ref[:, cs] = alpha * l_ref[:, cs] + jnp.sum(p, axis=0, keepdims=True)
        acc_ref[:, cs] = alpha * acc_ref[:, cs] + _dot(vtc, p.astype(BF))
        m_ref[:, cs] = m_new

    kcc = k_ref[0, n_lat:n_lat + n_ctx, :]
    for s in range(n_slots):
        sctx_ref[s] = scores(kcc, s)

    @pl.when(qi < n_lat_tiles)
    def _():
        bufs = (sa_ref, sb_ref)
        kc0 = lat_keys(0)
        for s in range(n_slots):
            sa_ref[s] = scores(kc0, s)
            update(sctx_ref[s], s, vtc_ref[0])

        def step(c, cur, nxt):
            kn = lat_keys(c + 1)
            for s in range(n_slots):
                nxt[s] = scores(kn, s)
                update(cur[s], s, vt_ref[0, c])

        def pair(i, carry):
            step(2 * i, sa_ref, sb_ref)
            step(2 * i + 1, sb_ref, sa_ref)
            return carry

        n_steps = n_chunks - 1
        lax.fori_loop(0, n_steps // 2, pair, 0)
        if n_steps % 2:
            step(n_steps - 1, sa_ref, sb_ref)
        last = bufs[n_steps % 2]
        for s in range(n_slots):
            update(last[s], s, vt_ref[0, n_chunks - 1])

    @pl.when(qi >= n_lat_tiles)
    def _():
        for s in range(n_slots):
            update(sctx_ref[s], s, vtc_ref[0])

    o = acc_ref[...] / l_ref[...]
    row_half = lax.broadcasted_iota(jnp.int32, (LANES, tq), 0) // HEAD_DIM
    for t in range(GQA_Q_W // LANES):
        o0 = o[:, (2 * t) * tq:(2 * t + 1) * tq]
        o1 = o[:, (2 * t + 1) * tq:(2 * t + 2) * tq]
        o_ref[0, :, t * LANES:(t + 1) * LANES] = jnp.where(row_half == 0, o0, o1).T.astype(o_ref.dtype)


def gqa_attention(qg, kg, vg, n_lat, n_q_tiles):
    b, s, _ = qg.shape
    tq = TOKEN_TILE
    tk = 512
    n_ctx = s - n_lat
    nc = n_lat // tk
    vt = jnp.swapaxes(vg[:, :n_lat].reshape(b, nc, tk, LANES), 2, 3)
    vtc = jnp.swapaxes(vg[:, n_lat:], 1, 2)
    kern = functools.partial(_gqa_kernel, n_lat=n_lat, n_ctx=n_ctx, tk=tk, n_lat_tiles=n_lat // tq)
    return pl.pallas_call(
        kern, grid=(b, n_q_tiles),
        in_specs=[pl.BlockSpec((1, tq, GQA_Q_W), lambda bi, i: (bi, i, 0)),
                  pl.BlockSpec((1, s, LANES), lambda bi, i: (bi, 0, 0)),
                  pl.BlockSpec((1, nc, LANES, tk), lambda bi, i: (bi, 0, 0, 0)),
                  pl.BlockSpec((1, LANES, n_ctx), lambda bi, i: (bi, 0, 0))],
        out_specs=pl.BlockSpec((1, tq, GQA_Q_W), lambda bi, i: (bi, i, 0)),
        out_shape=jax.ShapeDtypeStruct((b, n_q_tiles * tq, GQA_Q_W), BF),
        scratch_shapes=[pltpu.VMEM((GQA_HEADS, tq, LANES), BF),
                        pltpu.VMEM((1, GQA_HEADS * tq), F32),
                        pltpu.VMEM((1, GQA_HEADS * tq), F32),
                        pltpu.VMEM((LANES, GQA_HEADS * tq), F32),
                        pltpu.VMEM((GQA_HEADS, tk, tq), F32),
                        pltpu.VMEM((GQA_HEADS, tk, tq), F32),
                        pltpu.VMEM((GQA_HEADS, n_ctx, tq), F32)],
        compiler_params=_cp(("arbitrary", "arbitrary")), name="gqa",
    )(qg, kg, vt, vtc)


def _na_kernel(q_ref, k_ref, v_ref, tab_ref, o_ref, *, n_lat, n_ctx, rows):
    rb = pl.program_id(1)
    ws = jnp.clip(rb * NA_QROWS - NA_WIN_R // 2, 0, rows - NA_UNION)
    off = pl.multiple_of(ws * GRID_W, GRID_W)
    nwin = NA_UNION * GRID_W
    def window_and_context(ref, pair):
        ls = slice(pair * LANES, (pair + 1) * LANES)
        return jnp.concatenate([ref[0, pl.ds(off, nwin), ls], ref[0, n_lat:n_lat + n_ctx, ls]], axis=0)

    lane_half = lax.broadcasted_iota(jnp.int32, (q_ref.shape[1], LANES), 1) // HEAD_DIM

    def scores(h):
        kcat = window_and_context(k_ref, h // 2)
        qt = q_ref[0, :, (h // 2) * LANES:(h // 2 + 1) * LANES]
        qh = jnp.where(lane_half == h % 2, qt, jnp.zeros((), qt.dtype))
        return _dot_nt(qh, kcat) + tab_ref[0, 0, h]

    s_next = scores(0)
    o_pair = [None, None]
    for h in range(NA_HEADS):
        s = s_next
        if h + 1 < NA_HEADS:
            s_next = scores(h + 1)
        m = jnp.max(s, axis=-1, keepdims=True)
        p = jnp.exp2(s - m)
        l = jnp.sum(p, axis=-1, keepdims=True)
        o_pair[h % 2] = _dot(p.astype(BF), window_and_context(v_ref, h // 2)) / l
        if h % 2:
            ts = slice((h // 2) * LANES, (h // 2 + 1) * LANES)
            o_ref[0, :, ts] = jnp.where(lane_half == 0, o_pair[0], o_pair[1]).astype(o_ref.dtype)


def na_attention(qn, kn, vn, table, layer, n_lat, n_q_tiles):
    b, s, _ = qn.shape
    tq = NA_QROWS * GRID_W
    rows = n_lat // GRID_W
    n_lat_tiles = n_lat // tq
    tw = table.shape[-1]

    def tab_idx(bi, rb):
        case = jnp.where(rb == 0, 0, jnp.where(rb < n_lat_tiles - 1, 1, jnp.where(rb == n_lat_tiles - 1, 2, 3)))
        return (layer, case, 0, 0, 0)

    kern = functools.partial(_na_kernel, n_lat=n_lat, n_ctx=s - n_lat, rows=rows)
    return pl.pallas_call(
        kern, grid=(b, n_q_tiles),
        in_specs=[pl.BlockSpec((1, tq, NA_W), lambda bi, rb: (bi, rb, 0)),
                  pl.BlockSpec((1, s, NA_W), lambda bi, rb: (bi, 0, 0)),
                  pl.BlockSpec((1, s, NA_W), lambda bi, rb: (bi, 0, 0)),
                  pl.BlockSpec((1, 1, NA_HEADS, tq, tw), tab_idx)],
        out_specs=pl.BlockSpec((1, tq, NA_W), lambda bi, rb: (bi, rb, 0)),
        out_shape=jax.ShapeDtypeStruct((b, n_q_tiles * tq, NA_W), BF),
        compiler_params=_cp(("arbitrary", "arbitrary"), 56), name="na",
    )(qn, kn, vn, table)


def na_table(rpb, rows, n_ctx):
    tq = NA_QROWS * GRID_W
    qr = jnp.arange(NA_QROWS)[:, None, None, None]
    qc = jnp.arange(GRID_W)[None, :, None, None]
    kr = jnp.arange(NA_UNION)[None, None, :, None]
    kc = jnp.arange(GRID_W)[None, None, None, :]
    cs = jnp.clip(qc - NA_WIN_C // 2, 0, GRID_W - NA_WIN_C)
    col_ok = (kc >= cs) & (kc < cs + NA_WIN_C)
    col_idx = (kc - qc + NA_WIN_C - 1)[:, :, 0, :]
    col_hot = (col_idx[0, :, :, None] == jnp.arange(2 * NA_WIN_C - 1)).astype(F32)
    n_blocks = rows // NA_QROWS
    hi = lax.Precision.HIGHEST
    tabs = []
    for blk in (0, 1, n_blocks - 1):
        r = blk * NA_QROWS + qr
        ws = min(max(blk * NA_QROWS - NA_WIN_R // 2, 0), rows - NA_UNION)
        rs = jnp.clip(r - NA_WIN_R // 2, 0, rows - NA_WIN_R)
        ka = ws + kr
        ok = (ka >= rs) & (ka < rs + NA_WIN_R) & col_ok
        row_idx = (ka - r + NA_WIN_R - 1)[:, 0, :, 0]
        row_hot = (row_idx[:, :, None] == jnp.arange(2 * NA_WIN_R - 1)).astype(F32)
        by_row = jnp.einsum('hab,rka->hrkb', rpb.astype(F32), row_hot, precision=hi)
        bias = jnp.einsum('hrkb,qcb->hrqkc', by_row, col_hot, precision=hi)
        tabs.append(jnp.where(ok[None], bias, NEG).reshape(NA_HEADS, tq, NA_UNION * GRID_W))
    tabs.append(jnp.full((NA_HEADS, tq, NA_UNION * GRID_W), NEG, F32))
    win = jnp.stack(tabs, axis=0)
    return jnp.concatenate([win * LOG2E, jnp.zeros(win.shape[:3] + (n_ctx,), F32)], axis=-1)


def _merge_kernel(xl_ref, xc_ref, y_ref, og_ref, on_ref, gt_ref, mod_ref, gw_ref, gb_ref, ws_ref, wg_ref, wn_ref,
                  wo_ref, o_ref, *, n_lat_tiles):
    d = xl_ref.shape[-1]
    y = jnp.concatenate([y_ref[q, 0].astype(F32) for q in range(SSM_QT)], axis=1)
    gy = _gelu_tanh(y)
    ys = gy * jax.nn.sigmoid(_dot(gy.astype(BF), gw_ref[...]) + gb_ref[...])
    gt = gt_ref[0]
    m = gt[:, :d].astype(F32) * _dot(ys.astype(BF), ws_ref[...])
    m += gt[:, d:2 * d].astype(F32) * _dot(og_ref[0], wg_ref[...])
    m += gt[:, 2 * d:].astype(F32) * _dot(on_ref[0], wn_ref[...])
    x = _stream_tile(xl_ref, xc_ref, n_lat_tiles)
    o_ref[0] = x + mod_ref[0][2:3] * _dot(m.astype(BF), wo_ref[...])


def merge(x_lat, x_ctx, y4, og, on, gt, mod, gw, gb, ws, wg, wn, wo, n_tiles, n_lat_tiles):
    b, _, d = x_lat.shape
    tm = TOKEN_TILE
    nb = mod.shape[0] - 1

    def row(bi, j):
        return (bi, j, 0)

    return pl.pallas_call(
        functools.partial(_merge_kernel, n_lat_tiles=n_lat_tiles), grid=(b, n_tiles),
        in_specs=_stream_specs(x_lat, x_ctx, n_lat_tiles) + [
                  pl.BlockSpec((SSM_QT, 1, tm, LANES), lambda bi, j: (0, bi, j, 0)),
                  pl.BlockSpec((1, tm, GQA_Q_W), row),
                  pl.BlockSpec((1, tm, NA_W), row),
                  pl.BlockSpec((1, tm, 3 * d), row),
                  pl.BlockSpec((1, 6, d), lambda bi, j: (jnp.where(j < n_lat_tiles, bi, nb), 0, 0)),
                  _const_spec(gw.shape), _const_spec(gb.shape), _const_spec(ws.shape),
                  _const_spec(wg.shape), _const_spec(wn.shape), _const_spec(wo.shape)],
        out_specs=pl.BlockSpec((1, tm, d), row),
        out_shape=jax.ShapeDtypeStruct((b, n_tiles * tm, d), F32),
        compiler_params=_cp(("arbitrary", "arbitrary")), name="merge",
    )(x_lat, x_ctx, y4, og, on, gt, mod, gw, gb, ws, wg, wn, wo)


def _ffn_kernel(x_ref, mod_ref, g_ref, wg_ref, wu_ref, wd_ref, o_ref):
    x = x_ref[0]
    mod = mod_ref[0]
    hb = (_rms(x, g_ref[...]) * (1.0 + mod[4:5]) + mod[3:4]).astype(BF)
    a = _silu(_dot(hb, wg_ref[...])) * _dot(hb, wu_ref[...])
    o_ref[0] = x + mod[5:6] * _dot(a.astype(BF), wd_ref[...])


def ffn(xall, mod, g, wg, wu, wd, n_lat_tiles):
    b, s, d = xall.shape
    tm = TOKEN_TILE
    nb = mod.shape[0] - 1

    def row(bi, j):
        return (bi, j, 0)

    return pl.pallas_call(
        _ffn_kernel, grid=(b, s // tm),
        in_specs=[pl.BlockSpec((1, tm, d), row),
                  pl.BlockSpec((1, 6, d), lambda bi, j: (jnp.where(j < n_lat_tiles, bi, nb), 0, 0)),
                  _const_spec(g.shape), _const_spec(wg.shape), _const_spec(wu.shape), _const_spec(wd.shape)],
        out_specs=pl.BlockSpec((1, tm, d), row),
        out_shape=jax.ShapeDtypeStruct((b, s, d), F32),
        compiler_params=_cp(("arbitrary", "arbitrary"), 56), name="ffn",
    )(xall, mod, g, wg, wu, wd)


ROW_TILES = D_MODEL // LANES


def _store_row_tiles(ref, lead, val):
    n = val.shape[0]
    for k in range(ROW_TILES):
        ref[lead + (pl.ds(k, n, stride=ROW_TILES), slice(None))] = val[:, k * LANES:(k + 1) * LANES]


def _load_row_tiles(ref):
    n = ref.shape[0] // ROW_TILES
    return jnp.concatenate([ref[pl.ds(k, n, stride=ROW_TILES), :] for k in range(ROW_TILES)], axis=1)


def _route_kernel(x_ref, mod_ref, g_ref, rw_ref, h_ref, info_ref):
    x = x_ref[0]
    mod = mod_ref[0]
    h = _rms(x, g_ref[...]) * (1.0 + mod[4:5]) + mod[3:4]
    _store_row_tiles(h_ref, (0,), h)
    logits = jnp.dot(h, rw_ref[...], preferred_element_type=F32, precision=lax.Precision.HIGHEST)
    lane = lax.broadcasted_iota(jnp.int32, logits.shape, 1)
    lanef = lane.astype(F32)
    logits = jnp.where(lane < N_EXPERTS, logits, -jnp.inf)
    m1 = jnp.max(logits, axis=-1, keepdims=True)
    i1 = jnp.min(jnp.where(logits == m1, lanef, float(LANES)), axis=-1, keepdims=True)
    rest = jnp.where(lanef == i1, -jnp.inf, logits)
    m2 = jnp.max(rest, axis=-1, keepdims=True)
    i2 = jnp.min(jnp.where(rest == m2, lanef, float(LANES)), axis=-1, keepdims=True)
    e2 = jnp.exp(m2 - m1)
    w1 = 1.0 / (1.0 + e2)
    w2 = e2 / (1.0 + e2)
    info_ref[0] = jnp.where(lane == 0, i1, jnp.where(lane == 1, i2, jnp.where(lane == 2, w1,
                            jnp.where(lane == 3, w2, 0.0))))


def moe_route(xall, mod, g, rw, n_tiles):
    b, s, d = xall.shape
    tm = TOKEN_TILE

    def row(bi, j):
        return (bi, j, 0)

    return pl.pallas_call(
        _route_kernel, grid=(b, n_tiles),
        in_specs=[pl.BlockSpec((1, tm, d), row), pl.BlockSpec((1, 6, d), lambda bi, j: (bi, 0, 0)),
                  _const_spec(g.shape), _const_spec(rw.shape)],
        out_specs=[pl.BlockSpec((1, tm * ROW_TILES, LANES), row),
                   pl.BlockSpec((1, tm, LANES), row)],
        out_shape=[jax.ShapeDtypeStruct((b, n_tiles * tm * ROW_TILES, LANES), F32),
                   jax.ShapeDtypeStruct((b, n_tiles * tm, LANES), F32)],
        compiler_params=_cp(("arbitrary", "arbitrary")), name="moe_route",
    )(xall, mod, g, rw)


def _dispatch_kernel(pos_ref, h_ref, xs_in_ref, xs_ref, sem):
    del xs_in_ref
    tm = h_ref.shape[0] // ROW_TILES

    def row_copy(r, k):
        dst = pl.multiple_of(pos_ref[0, 0, 2 * r + k] * ROW_TILES, ROW_TILES)
        src = pl.multiple_of(r * ROW_TILES, ROW_TILES)
        return pltpu.make_async_copy(h_ref.at[pl.ds(src, ROW_TILES)], xs_ref.at[pl.ds(dst, ROW_TILES)], sem)

    def issue(r, c):
        row_copy(r, 0).start(priority=0)
        row_copy(r, 1).start(priority=1)
        return c

    def drain(r, c):
        row_copy(r, 0).wait()
        row_copy(r, 1).wait()
        return c

    lax.fori_loop(0, tm, issue, 0, unroll=4)
    lax.fori_loop(0, tm, drain, 0)


def moe_dispatch(h2, pos, n_slots):
    t = h2.shape[0] // ROW_TILES
    tm = MOE_ROW_TILE
    zeros = jnp.zeros((n_slots * ROW_TILES, LANES), F32)
    return pl.pallas_call(
        _dispatch_kernel, grid=(t // tm,),
        in_specs=[pl.BlockSpec((1, 1, 2 * tm), lambda i: (i, 0, 0), memory_space=pltpu.SMEM),
                  pl.BlockSpec((tm * ROW_TILES, LANES), lambda i: (i, 0)),
                  pl.BlockSpec(memory_space=pl.ANY)],
        out_specs=pl.BlockSpec(memory_space=pl.ANY),
        out_shape=jax.ShapeDtypeStruct((n_slots * ROW_TILES, LANES), F32),
        scratch_shapes=[pltpu.SemaphoreType.DMA(())],
        input_output_aliases={2: 0},
        compiler_params=_cp(("arbitrary",)), name="moe_dispatch",
    )(pos, h2, zeros)


def _experts_kernel(te_ref, nv_ref, x_ref, wg_ref, wu_ref, wd_ref, o_ref, acc_ref):
    i = pl.program_id(0)
    j = pl.program_id(1)

    @pl.when(i < nv_ref[0])
    def _():
        xb = _load_row_tiles(x_ref).astype(BF)
        a = _silu(_dot(xb, wg_ref[0])) * _dot(xb, wu_ref[0])
        part = _dot(a.astype(BF), wd_ref[0])

        @pl.when(j == 0)
        def _():
            acc_ref[...] = part

        @pl.when(j > 0)
        def _():
            acc_ref[...] += part

        @pl.when(j == pl.num_programs(1) - 1)
        def _():
            _store_row_tiles(o_ref, (), acc_ref[...])

    @pl.when((i >= nv_ref[0]) & (j == pl.num_programs(1) - 1))
    def _():
        o_ref[...] = jnp.zeros(o_ref.shape, o_ref.dtype)


def moe_experts(xs, tile_expert, n_valid, wg, wu, wd):
    n_slots = xs.shape[0] // ROW_TILES
    d = ROW_TILES * LANES
    tm = MOE_TILE
    tf = MOE_FCHUNK
    f = wg.shape[-1]
    nf = f // tf

    def xrow(i, j, te, nv):
        return (jnp.minimum(i, nv[0] - 1), 0)

    def fcol(i, j, nv):
        return jnp.where(i < nv[0], j, nf - 1)

    grid_spec = pltpu.PrefetchScalarGridSpec(
        num_scalar_prefetch=2, grid=(n_slots // tm, nf),
        in_specs=[pl.BlockSpec((tm * ROW_TILES, LANES), xrow),
                  pl.BlockSpec((1, d, tf), lambda i, j, te, nv: (te[i], 0, fcol(i, j, nv))),
                  pl.BlockSpec((1, d, tf), lambda i, j, te, nv: (te[i], 0, fcol(i, j, nv))),
                  pl.BlockSpec((1, tf, d), lambda i, j, te, nv: (te[i], fcol(i, j, nv), 0))],
        out_specs=pl.BlockSpec((tm * ROW_TILES, LANES), lambda i, j, te, nv: (i, 0)),
        scratch_shapes=[pltpu.VMEM((tm, d), F32)])
    return pl.pallas_call(
        _experts_kernel, grid_spec=grid_spec,
        out_shape=jax.ShapeDtypeStruct((n_slots * ROW_TILES, LANES), F32),
        compiler_params=_cp(("arbitrary", "arbitrary"), 56), name="moe_experts",
    )(tile_expert, n_valid, xs, wg, wu, wd)


def _combine_kernel(pos_ref, x_ref, info_ref, mod_ref, fg_ref, ys_ref, o_ref, y1_ref, y2_ref, sem):
    tm = x_ref.shape[1]

    def row_copy(r, k, dst):
        src = pl.multiple_of(pos_ref[0, 0, 2 * r + k] * ROW_TILES, ROW_TILES)
        row = pl.multiple_of(r * ROW_TILES, ROW_TILES)
        return pltpu.make_async_copy(ys_ref.at[pl.ds(src, ROW_TILES)], dst.at[pl.ds(row, ROW_TILES)], sem)

    def issue(r, c):
        row_copy(r, 0, y1_ref).start(priority=0)
        row_copy(r, 1, y2_ref).start(priority=1)
        return c

    def drain(r, c):
        row_copy(r, 0, y1_ref).wait()
        row_copy(r, 1, y2_ref).wait()
        return c

    lax.fori_loop(0, tm, issue, 0, unroll=4)
    lax.fori_loop(0, tm, drain, 0)
    info = info_ref[0]
    y = info[:, 2:3] * _load_row_tiles(y1_ref) + info[:, 3:4] * _load_row_tiles(y2_ref)
    xn = x_ref[0] + mod_ref[0][5:6] * y
    o_ref[0] = _rms(xn, fg_ref[...])


def moe_combine(x, info, mod, fg, ys, pos, n_tiles):
    b, s, d = x.shape
    tm = MOE_ROW_TILE

    def row(bi, j):
        return (bi, j, 0)

    return pl.pallas_call(
        _combine_kernel, grid=(b, n_tiles),
        in_specs=[pl.BlockSpec((1, 1, 2 * tm), lambda bi, j: (bi * n_tiles + j, 0, 0), memory_space=pltpu.SMEM),
                  pl.BlockSpec((1, tm, d), row), pl.BlockSpec((1, tm, LANES), row),
                  pl.BlockSpec((1, 6, d), lambda bi, j: (bi, 0, 0)), _const_spec(fg.shape),
                  pl.BlockSpec(memory_space=pl.ANY)],
        out_specs=pl.BlockSpec((1, tm, d), row),
        out_shape=jax.ShapeDtypeStruct((b, n_tiles * tm, d), F32),
        scratch_shapes=[pltpu.VMEM((tm * ROW_TILES, LANES), F32), pltpu.VMEM((tm * ROW_TILES, LANES), F32),
                        pltpu.SemaphoreType.DMA(())],
        compiler_params=_cp(("arbitrary", "arbitrary")), name="moe_combine",
    )(pos, x, info, mod, fg, ys)


def moe_layer(x, mod, g2, rw, wg, wu, wd, fg, n_lat):
    b, s, d = x.shape
    n_tiles = n_lat // TOKEN_TILE
    t = b * n_lat
    rw_pad = jnp.zeros((d, LANES), F32).at[:, :N_EXPERTS].set(rw.astype(F32))
    h2, info = moe_route(x, mod, g2, rw_pad, n_tiles)
    info2 = info.reshape(t, LANES)

    e_pair = info2[:, :2].astype(jnp.int32).reshape(2 * t)
    onehot = (e_pair[:, None] == jnp.arange(N_EXPERTS)[None, :]).astype(jnp.int32)
    csum = jnp.cumsum(onehot, axis=0)
    rank = jnp.sum((csum - onehot) * onehot, axis=1)
    counts = csum[-1]
    tiles_e = (counts + MOE_TILE - 1) // MOE_TILE
    tile_end = jnp.cumsum(tiles_e)
    slot_off = (tile_end - tiles_e) * MOE_TILE
    pos = (slot_off[e_pair] + rank).astype(jnp.int32).reshape(t // MOE_ROW_TILE, 1, 2 * MOE_ROW_TILE)
    n_tiles_max = (2 * t) // MOE_TILE + N_EXPERTS
    n_valid = tile_end[-1:].astype(jnp.int32)
    tile_ids = jnp.minimum(jnp.arange(n_tiles_max), n_valid[0] - 1)
    tile_expert = jnp.sum((tile_ids[:, None] >= tile_end[None, :]).astype(jnp.int32), axis=1).astype(jnp.int32)

    xs = moe_dispatch(h2.reshape(t * ROW_TILES, LANES), pos, n_tiles_max * MOE_TILE)
    ys = moe_experts(xs, tile_expert, n_valid, wg, wu, wd)
    return moe_combine(x, info, mod, fg, ys, pos, n_lat // MOE_ROW_TILE)


def _rope_tables(n_lat, n_ctx):
    t = jnp.arange(n_lat)
    pos = jnp.stack([t // GRID_W, t % GRID_W], axis=-1).astype(F32)
    half = HEAD_DIM // 2
    inv = 1.0 / (ROPE_BASE ** (jnp.arange(0, half, 2, dtype=F32) / half))
    ang = pos[:, :, None] * inv
    ang = jnp.concatenate([ang, ang], axis=-1).reshape(n_lat, HEAD_DIM)
    cos = jnp.concatenate([jnp.cos(ang), jnp.ones((n_ctx, HEAD_DIM), F32)], axis=0)
    sin = jnp.concatenate([jnp.sin(ang), jnp.zeros((n_ctx, HEAD_DIM), F32)], axis=0)
    return jnp.tile(cos, (1, 2)), jnp.tile(sin, (1, 2))


_GQA_HEAD_ORDER = tuple(g * GQA_GROUP + t for t in range(GQA_GROUP) for g in range(GQA_KV_HEADS))


def _gqa_cols(w):
    return jnp.concatenate([w[:, h * HEAD_DIM:(h + 1) * HEAD_DIM] for h in _GQA_HEAD_ORDER], axis=1)


def _gqa_rows(w):
    return jnp.concatenate([w[h * HEAD_DIM:(h + 1) * HEAD_DIM] for h in _GQA_HEAD_ORDER], axis=0)


def kernel(x, c, ctx, c_ctx, w_mod, b_mod, norm1_g, w_in, ssm_a_re, ssm_a_im, ssm_b_re, ssm_b_im, ssm_c_re,
           ssm_c_im, ssm_log_dt, ssm_d, glu_w, glu_b, q_norm_g, k_norm_g, na_rpb, w_branch_ssm, w_branch_gqa,
           w_branch_na, w_out, norm2_g, ffn_w_gate, ffn_w_up, ffn_w_down, router_w, moe_w_gate, moe_w_up,
           moe_w_down, final_norm_g):
    b, n_lat, d = x.shape
    n_ctx = ctx.shape[1]
    s = n_lat + n_ctx
    depth = w_mod.shape[0]
    assert d == D_MODEL and n_lat % (NA_QROWS * GRID_W) == 0 and n_ctx == TOKEN_TILE and n_lat % MOE_ROW_TILE == 0
    assert depth == 2 and n_lat // GRID_W >= NA_UNION
    n_lat_tiles = n_lat // TOKEN_TILE
    n_all_tiles = s // TOKEN_TILE

    n_rows = -(-(b + 1) // 8) * 8
    cv = jnp.zeros((n_rows, d), F32).at[:b].set(c).at[b].set(c_ctx)
    mods = adaln(cv, w_mod, b_mod)[:, :b + 1].reshape(depth, b + 1, 6, d)

    cos, sin = _rope_tables(n_lat, n_ctx)
    table_shape_rows = n_lat // GRID_W
    x_lat, x_ctx = x, ctx

    mats = jax.vmap(s5_matrices)(ssm_a_re, ssm_a_im, ssm_b_re, ssm_b_im, ssm_c_re, ssm_c_im, ssm_log_dt, ssm_d)
    tables = jax.vmap(lambda r: na_table(r, table_shape_rows, n_ctx))(na_rpb)

    out = None
    for i in range(depth):
        last = i == depth - 1
        wi = w_in[i]
        c0 = SSM_WIDTH
        wu = wi[:, :c0].astype(BF)
        wkv = wi[:, c0:KV_COLS].astype(BF)
        wqg = _gqa_cols(wi[:, KV_COLS:KV_COLS + GQA_Q_W]).astype(BF)
        wqn = wi[:, KV_COLS + GQA_Q_W:KV_COLS + GQA_Q_W + NA_W].astype(BF)
        wgt = wi[:, KV_COLS + GQA_Q_W + NA_W:].astype(BF)
        qgain = jnp.tile(q_norm_g[i].astype(F32), 2)[None, :]
        kgain = jnp.tile(k_norm_g[i].astype(F32), GQA_KV_HEADS)[None, :]

        uq, kg, vg, kn, vn, qg, qn, gt = inproj(x_lat, x_ctx, mods[i], norm1_g[i][None, :], cos, sin, wu, wkv,
                                                wqg, wqn, wgt, qgain, kgain, n_lat_tiles)

        y4 = s5_branch(uq, mats, i, n_lat // SSM_CHUNK)

        n_q_tiles = n_lat_tiles if last else n_all_tiles
        og = gqa_attention(qg, kg, vg, n_lat, n_q_tiles)
        on = na_attention(qn, kn, vn, tables, i, n_lat, n_q_tiles)

        xall = merge(x_lat, x_ctx, y4, og, on, gt, mods[i], glu_w[i].astype(BF), glu_b[i][None, :].astype(F32),
                     w_branch_ssm[i].astype(BF), _gqa_rows(w_branch_gqa[i]).astype(BF),
                     w_branch_na[i].astype(BF), w_out[i].astype(BF), n_q_tiles, n_lat_tiles)

        j = i // 2
        if i % 2 == 0:
            xall = ffn(xall, mods[i], norm2_g[i][None, :], ffn_w_gate[j].astype(BF), ffn_w_up[j].astype(BF),
                       ffn_w_down[j].astype(BF), n_lat_tiles)
            x_lat = x_ctx = xall
        else:
            out = moe_layer(xall, mods[i], norm2_g[i][None, :], router_w[j], moe_w_gate[j].astype(BF),
                            moe_w_up[j].astype(BF), moe_w_down[j].astype(BF), final_norm_g[None, :], n_lat)
    return out
```
